```python
import jax, jax.numpy as jnp
from jax import lax
import numpy as np

D_MODEL = 1024
BATCH = 4
SEQ = 8192
DEPTH = 2
DEC_BATCH = 32
DEC_SEQ = 64
PAST_LEN = 4096

CHUNK = 64
Q_BLOCK = 128
EPS = 1e-6
NEG_INF = -1e30
ROPE_THETA = 10000.0
MLA_HEADS = 8
MLA_NOPE = 64
MLA_ROPE = 32
MLA_V = 64
MLA_Q_RANK = 256
MLA_KV_RANK = 256
MLA_WIDTH = MLA_HEADS * MLA_V
MLA_SCALE = (MLA_NOPE + MLA_ROPE) ** -0.5
HG_HEADS = 4
HG_DK = 64
HG_DV = 64
HG_WIDTH = HG_HEADS * HG_DV
GLA_HEADS = 4
GLA_DK = 32
GLA_DV = 64
GLA_RANK = 16
GLA_TAU = 16.0
GLA_WIDTH = GLA_HEADS * GLA_DV
MIX_WIDTH = MLA_WIDTH + HG_WIDTH + GLA_WIDTH
IN_SIZES = (MLA_Q_RANK, MLA_KV_RANK, MLA_ROPE,
            HG_HEADS * HG_DK, HG_HEADS * HG_DK, HG_WIDTH, HG_WIDTH,
            GLA_HEADS * GLA_DK, GLA_HEADS * GLA_DK, GLA_WIDTH, GLA_WIDTH, GLA_RANK)
IN_COLS = sum(IN_SIZES)
D_FF = -(-8 * D_MODEL // (3 * 256)) * 256

kernel_name = 'hymba_mla_hgrn2_gla_stream'


def rmsnorm(x, g):
    xf = x.astype(jnp.float32)
    y = xf * lax.rsqrt(jnp.mean(xf * xf, axis=-1, keepdims=True) + EPS)
    return (y * g.astype(jnp.float32)).astype(x.dtype)


def split_projection(z):
    idx = [int(v) for v in np.cumsum(IN_SIZES)[:-1]]
    return jnp.split(z, idx, axis=-1)


def rope_angles(pos):
    inv = ROPE_THETA ** (-(jnp.arange(0, MLA_ROPE, 2, dtype=jnp.float32) / MLA_ROPE))
    ang = pos.astype(jnp.float32)[:, None] * inv[None, :]
    return jnp.cos(ang), jnp.sin(ang)


def apply_rope(x, cos, sin):
    xf = x.astype(jnp.float32)
    half = xf.shape[-1] // 2
    x1, x2 = xf[..., :half], xf[..., half:]
    return jnp.concatenate([x1 * cos - x2 * sin, x2 * cos + x1 * sin], axis=-1).astype(x.dtype)


def chunk_recurrence(q, k, v, log_f, s0):
    B, T, H, _ = q.shape
    dv = v.shape[-1]
    c = min(CHUNK, T)
    n = T // c

    def to_chunks(a):
        return jnp.moveaxis(a.astype(jnp.float32).reshape(B, n, c, H, a.shape[-1]), 1, 0)

    tril = jnp.tril(jnp.ones((c, c), dtype=bool))

    def step(S, xs):
        qc, kc, vc, gc = xs
        b = jnp.cumsum(gc, axis=1)
        o_inter = jnp.einsum('bchk,bhkv->bchv', qc * jnp.exp(b), S)
        diff = b[:, :, None] - b[:, None, :]
        decay = jnp.exp(jnp.where(tril[None, :, :, None, None], diff, -jnp.inf))
        scores = jnp.sum(qc[:, :, None] * kc[:, None] * decay, axis=-1)
        o_intra = jnp.einsum('bijh,bjhv->bihv', scores, vc)
        b_last = b[:, -1]
        S = jnp.exp(b_last)[..., None] * S + jnp.einsum(
            'bjhk,bjhv->bhkv', kc * jnp.exp(b_last[:, None] - b), vc)
        return S, o_inter + o_intra

    S, o = lax.scan(step, s0.astype(jnp.float32),
                    (to_chunks(q), to_chunks(k), to_chunks(v), to_chunks(log_f)))
    o = jnp.moveaxis(o, 0, 1).reshape(B, T, H, dv)
    return o.astype(v.dtype), S.astype(s0.dtype)


def mla_block(q_nope, q_rope, q_pos, k_nope, k_rope, ckv, k_pos, w_uv):
    s = (jnp.einsum('bqhd,bkhd->bhqk', q_nope, k_nope)
         + jnp.einsum('bqhd,bkd->bhqk', q_rope, k_rope)).astype(jnp.float32) * MLA_SCALE
    visible = (k_pos // CHUNK)[None, :] <= (q_pos // CHUNK)[:, None]
    s = jnp.where(visible[None, None], s, NEG_INF)
    p = jax.nn.softmax(s, axis=-1).astype(ckv.dtype)
    o_lat = jnp.einsum('bhqk,bkr->bqhr', p, ckv)
    return jnp.einsum('bqhr,rhd->bqhd', o_lat, w_uv)


def mla_attention(q_nope, q_rope, pos, k_nope, k_rope, ckv, k_pos, w_uv):
    B, T = q_nope.shape[:2]
    qb = min(Q_BLOCK, T)
    nb = T // qb

    def blocks(a):
        return jnp.moveaxis(a.reshape((B, nb, qb) + a.shape[2:]), 1, 0)

    out = lax.map(lambda xs: mla_block(xs[0], xs[1], xs[2], k_nope, k_rope, ckv, k_pos, w_uv),
                  (blocks(q_nope), blocks(q_rope), pos.reshape(nb, qb)))
    return jnp.moveaxis(out, 0, 1).reshape(B, T, MLA_WIDTH)


def trunk_layer(x, c, pos, ckv_past, krope_past, s_hg, s_gla, lp):
    B, T, _ = x.shape
    P = ckv_past.shape[1]
    mod = jax.nn.silu(c) @ lp['w_mod'] + lp['b_mod']
    sh1, sc1, g1, sh2, sc2, g2 = jnp.split(mod[:, None, :], 6, axis=-1)
    h = rmsnorm(x, lp['norm_mix']) * (1.0 + sc1) + sh1
    cq, ckv, kr, hq, hf, hi, hg, gq, gk, gv, gg, ga = split_projection(h @ lp['w_in'])

    cq = rmsnorm(cq, lp['mla_cq_norm'])
    ckv = rmsnorm(ckv, lp['mla_ckv_norm'])
    q = jnp.einsum('btr,rhd->bthd', cq, lp['mla_w_uq'])
    cos, sin = rope_angles(pos)
    q_nope = rmsnorm(q[..., :MLA_NOPE], lp['mla_q_norm'][:MLA_NOPE])
    q_rope = apply_rope(rmsnorm(q[..., MLA_NOPE:], lp['mla_q_norm'][MLA_NOPE:]), cos[:, None], sin[:, None])
    k_rope = apply_rope(rmsnorm(kr, lp['mla_k_norm'][MLA_NOPE:]), cos, sin)
    ckv_all = jnp.concatenate([ckv_past, ckv], axis=1)
    krope_all = jnp.concatenate([krope_past, k_rope], axis=1)
    k_pos = jnp.concatenate([jnp.arange(P, dtype=jnp.int32), pos])
    k_nope = rmsnorm(jnp.einsum('bsr,rhd->bshd', ckv_all, lp['mla_w_uk']), lp['mla_k_norm'][:MLA_NOPE])
    o_mla = rmsnorm(mla_attention(q_nope, q_rope, pos, k_nope, krope_all, ckv_all, k_pos, lp['mla_w_uv']),
                    lp['mla_out_norm'])

    lb = lp['hg_lb'].reshape(HG_HEADS, HG_DK)
    f = lb + (1.0 - lb) * jax.nn.sigmoid(hf.reshape(B, T, HG_HEADS, HG_DK).astype(jnp.float32))
    o_hg, s_hg_new = chunk_recurrence(hq.reshape(B, T, HG_HEADS, HG_DK), 1.0 - f,
                                      hi.reshape(B, T, HG_HEADS, HG_DV), jnp.log(f), s_hg)
    o_hg = rmsnorm(o_hg, lp['hg_out_norm']).reshape(B, T, HG_WIDTH) * jax.nn.silu(hg)

    log_a = jax.nn.log_sigmoid((ga @ lp['gla_w_a2'] + lp['gla_b_a']).astype(jnp.float32)) / GLA_TAU
    o_gla, s_gla_new = chunk_recurrence(gq.reshape(B, T, GLA_HEADS, GLA_DK) * GLA_DK ** -0.5,
                                        gk.reshape(B, T, GLA_HEADS, GLA_DK),
                                        gv.reshape(B, T, GLA_HEADS, GLA_DV),
                                        log_a.reshape(B, T, GLA_HEADS, GLA_DK), s_gla)
    o_gla = rmsnorm(o_gla, lp['gla_out_norm']).reshape(B, T, GLA_WIDTH) * jax.nn.silu(gg)

    mix = jnp.concatenate([o_mla, o_hg, o_gla], axis=-1) @ lp['w_o']
    x = x + g1 * mix

    h2 = rmsnorm(x, lp['norm_ffn']) * (1.0 + sc2) + sh2
    a, u = jnp.split(h2 @ lp['w_ffn_in'], 2, axis=-1)
    x = x + g2 * ((jax.nn.silu(a) * u) @ lp['w_ffn_out'])
    return x, ckv, k_rope, s_hg_new, s_gla_new


def setup_inputs(seed: int = 0) -> dict:
    key = jax.random.key(seed)
    ks = jax.random.split(key, 32)

    def nrm(k, shape, scale):
        return jax.random.normal(k, shape, jnp.float32) * scale

    def gain(k, shape):
        return 1.0 + 0.02 * jax.random.normal(k, shape, jnp.float32)

    return {
        'x_prompt': nrm(ks[0], (BATCH, SEQ, D_MODEL), 1.0),
        'x_sample': nrm(ks[1], (DEC_BATCH, DEC_SEQ, D_MODEL), 1.0),
        'c_prompt': nrm(ks[2], (BATCH, D_MODEL), 1.0),
        'c_sample': nrm(ks[3], (DEC_BATCH, D_MODEL), 1.0),
        'cache_mla_ckv': nrm(ks[4], (DEPTH, DEC_BATCH, PAST_LEN, MLA_KV_RANK), 1.0),
        'cache_mla_krope': nrm(ks[5], (DEPTH, DEC_BATCH, PAST_LEN, MLA_ROPE), 1.0),
        'state_hgrn': nrm(ks[6], (DEPTH, DEC_BATCH, HG_HEADS, HG_DK, HG_DV), 0.5),
        'state_gla': nrm(ks[7], (DEPTH, DEC_BATCH, GLA_HEADS, GLA_DK, GLA_DV), 0.5),
        'w_mod': nrm(ks[8], (DEPTH, D_MODEL, 6 * D_MODEL), D_MODEL ** -0.5),
        'b_mod': nrm(ks[9], (DEPTH, 6 * D_MODEL), 0.02),
        'norm_mix': gain(ks[10], (DEPTH, D_MODEL)),
        'norm_ffn': gain(ks[11], (DEPTH, D_MODEL)),
        'w_in': nrm(ks[12], (DEPTH, D_MODEL, IN_COLS), D_MODEL ** -0.5),
        'mla_cq_norm': gain(ks[13], (DEPTH, MLA_Q_RANK)),
        'mla_ckv_norm': gain(ks[14], (DEPTH, MLA_KV_RANK)),
        'mla_w_uq': nrm(ks[15], (DEPTH, MLA_Q_RANK, MLA_HEADS, MLA_NOPE + MLA_ROPE), MLA_Q_RANK ** -0.5),
        'mla_w_uk': nrm(ks[16], (DEPTH, MLA_KV_RANK, MLA_HEADS, MLA_NOPE), MLA_KV_RANK ** -0.5),
        'mla_w_uv': nrm(ks[17], (DEPTH, MLA_KV_RANK, MLA_HEADS, MLA_V), MLA_KV_RANK ** -0.5),
        'mla_q_norm': gain(ks[18], (DEPTH, MLA_NOPE + MLA_ROPE)),
        'mla_k_norm': gain(ks[19], (DEPTH, MLA_NOPE + MLA_ROPE)),
        'mla_out_norm': gain(ks[20], (DEPTH, MLA_WIDTH)),
        'hg_lb_logits': nrm(ks[21], (DEPTH, HG_HEADS * HG_DK), 0.5),
        'hg_out_norm': gain(ks[22], (DEPTH, HG_DV)),
        'gla_w_a2': nrm(ks[23], (DEPTH, GLA_RANK, GLA_HEADS * GLA_DK), GLA_RANK ** -0.5),
        'gla_b_a': nrm(ks[24], (DEPTH, GLA_HEADS * GLA_DK), 0.1),
        'gla_out_norm': gain(ks[25], (DEPTH, GLA_DV)),
        'w_o': nrm(ks[26], (DEPTH, MIX_WIDTH, D_MODEL), MIX_WIDTH ** -0.5),
        'w_ffn_in': nrm(ks[27], (DEPTH, D_MODEL, 2 * D_FF), D_MODEL ** -0.5),
        'w_ffn_out': nrm(ks[28], (DEPTH, D_FF, D_MODEL), D_FF ** -0.5),
    }


def reference(x_prompt, x_sample, c_prompt, c_sample, cache_mla_ckv, cache_mla_krope, state_hgrn, state_gla,
              w_mod, b_mod, norm_mix, norm_ffn, w_in, mla_cq_norm, mla_ckv_norm, mla_w_uq, mla_w_uk, mla_w_uv,
              mla_q_norm, mla_k_norm, mla_out_norm, hg_lb_logits, hg_out_norm, gla_w_a2, gla_b_a, gla_out_norm,
              w_o, w_ffn_in, w_ffn_out):
    Bp, T_p = x_prompt.shape[:2]
    T_s = x_sample.shape[1]
    pos_p = jnp.arange(T_p, dtype=jnp.int32)
    pos_s = PAST_LEN + jnp.arange(T_s, dtype=jnp.int32)
    p_lb = jax.nn.softmax(hg_lb_logits.astype(jnp.float32), axis=0)
    lb_all = jnp.cumsum(p_lb, axis=0) - p_lb[0:1]

    dt = x_prompt.dtype
    ckv0 = jnp.zeros((Bp, 0, MLA_KV_RANK), dt)
    kr0 = jnp.zeros((Bp, 0, MLA_ROPE), dt)
    shg0 = jnp.zeros((Bp, HG_HEADS, HG_DK, HG_DV), dt)
    sgla0 = jnp.zeros((Bp, GLA_HEADS, GLA_DK, GLA_DV), dt)

    xp, xs = x_prompt, x_sample
    ckv_p, kr_p, hg_p, gla_p = [], [], [], []
    ckv_s, kr_s, hg_s, gla_s = [], [], [], []
    for l in range(DEPTH):
        lp = {'w_mod': w_mod[l], 'b_mod': b_mod[l], 'norm_mix': norm_mix[l], 'norm_ffn': norm_ffn[l],
              'w_in': w_in[l], 'mla_cq_norm': mla_cq_norm[l], 'mla_ckv_norm': mla_ckv_norm[l],
              'mla_w_uq': mla_w_uq[l], 'mla_w_uk': mla_w_uk[l], 'mla_w_uv': mla_w_uv[l],
              'mla_q_norm': mla_q_norm[l], 'mla_k_norm': mla_k_norm[l], 'mla_out_norm': mla_out_norm[l],
              'hg_lb': lb_all[l], 'hg_out_norm': hg_out_norm[l], 'gla_w_a2': gla_w_a2[l],
              'gla_b_a': gla_b_a[l], 'gla_out_norm': gla_out_norm[l], 'w_o': w_o[l],
              'w_ffn_in': w_ffn_in[l], 'w_ffn_out': w_ffn_out[l]}
        xp, a, b, cc, d = trunk_layer(xp, c_prompt, pos_p, ckv0, kr0, shg0, sgla0, lp)
        ckv_p.append(a); kr_p.append(b); hg_p.append(cc); gla_p.append(d)
        xs, a, b, cc, d = trunk_layer(xs, c_sample, pos_s, cache_mla_ckv[l], cache_mla_krope[l],
                                      state_hgrn[l], state_gla[l], lp)
        ckv_s.append(a); kr_s.append(b); hg_s.append(cc); gla_s.append(d)

    new_ckv_prompt = jnp.stack(ckv_p)
    new_krope_prompt = jnp.stack(kr_p)
    new_hgrn_prompt = jnp.stack(hg_p)
    new_gla_prompt = jnp.stack(gla_p)
    new_ckv_sample = jnp.stack(ckv_s)
    new_krope_sample = jnp.stack(kr_s)
    new_hgrn_sample = jnp.stack(hg_s)
    new_gla_sample = jnp.stack(gla_s)
    return (xp, xs, new_ckv_prompt, new_krope_prompt, new_hgrn_prompt, new_gla_prompt,
            new_ckv_sample, new_krope_sample, new_hgrn_sample, new_gla_sample)
```

```python
import functools

import numpy as np
import jax
import jax.numpy as jnp
from jax import lax
from jax.experimental import pallas as pl
from jax.experimental.pallas import tpu as pltpu

F32, BF16 = jnp.float32, jnp.bfloat16

D_MODEL = 1024
PAST_LEN = 4096
CHUNK = 64
EPS = 1e-6
NEG_INF = -1e30
ROPE_THETA = 10000.0
MLA_HEADS = 8
MLA_NOPE = 64
MLA_ROPE = 32
MLA_V = 64
MLA_Q_RANK = 256
MLA_KV_RANK = 256
MLA_WIDTH = MLA_HEADS * MLA_V
MLA_SCALE = (MLA_NOPE + MLA_ROPE) ** -0.5
HG_HEADS = 4
HG_DK = 64
HG_DV = 64
GLA_HEADS = 4
GLA_DK = 32
GLA_DV = 64
GLA_RANK = 16
GLA_TAU = 16.0
D_FF = 2816

LANES = 128
HEAD_SLAB = 128
SUB = 16
ZCOLS = 2432
VMEM_LIMIT = 56 * 1024 * 1024

_Z = dict(cq=0, ckv=256, hq=512, hf=768, hi=1024, hg=1280, gq=1536, gk=1664, gv=1792, gg=2048, misc=2304)


def _bdot(a, b):
    return jnp.dot(a.astype(BF16), b.astype(BF16), preferred_element_type=F32)


def _dot_nt(a, b):
    return lax.dot_general(a, b, (((1,), (1,)), ((), ())), preferred_element_type=F32)


def _dot_tn(a, b):
    return lax.dot_general(a, b, (((0,), (0,)), ((), ())), preferred_element_type=F32)


def _sigmoid(x):
    return 1.0 / (1.0 + jnp.exp(-x))


def _silu(x):
    return x * _sigmoid(x)


def _log_sigmoid(x):
    return jnp.minimum(x, 0.0) - jnp.log1p(jnp.exp(-jnp.abs(x)))


def _rms(x):
    return x * lax.rsqrt(jnp.mean(x * x, axis=-1, keepdims=True) + EPS)


def _params(sem):
    return pltpu.CompilerParams(dimension_semantics=sem, vmem_limit_bytes=VMEM_LIMIT)


def _const_spec(shape):
    nd = len(shape)
    return pl.BlockSpec(shape, lambda *_: (0,) * nd)


def _mod_body(c_ref, w_ref, b_ref, o_ref):
    o_ref[...] = _bdot(_silu(c_ref[...]), w_ref[...]) + b_ref[...]


def _mod_call(c_pad, w_mod, b_mod):
    rows, d = c_pad.shape
    n = w_mod.shape[1]
    tn = 1024
    return pl.pallas_call(
        _mod_body,
        grid=(n // tn,),
        in_specs=[pl.BlockSpec((rows, d), lambda j: (0, 0)),
                  pl.BlockSpec((d, tn), lambda j: (0, j)),
                  pl.BlockSpec((1, tn), lambda j: (0, j))],
        out_specs=pl.BlockSpec((rows, tn), lambda j: (0, j)),
        out_shape=jax.ShapeDtypeStruct((rows, n), F32),
        compiler_params=_params(("arbitrary",)),
        name="mod",
    )(c_pad, w_mod, b_mod)


def _mla_kv(ckvn_bf, kro_bf, w_uk_ref, w_uv_ref, mk_ref, kgk_ref, tile_ref):
    kn = jnp.dot(ckvn_bf, w_uk_ref[...], preferred_element_type=F32)
    krt = jnp.dot(kro_bf, tile_ref[...], preferred_element_type=F32)
    kgk2 = jnp.concatenate([kgk_ref[...], kgk_ref[...]], axis=-1)
    parts = []
    for hp in range(MLA_HEADS // 2):
        sl = slice(2 * HEAD_SLAB * hp, 2 * HEAD_SLAB * (hp + 1))
        sk = kn[:, sl]
        msk = jnp.dot((sk * sk).astype(BF16), mk_ref[...], preferred_element_type=F32)
        parts.append((sk * lax.rsqrt(msk + EPS) * kgk2 + krt[:, sl]).astype(BF16))
    k = jnp.concatenate(parts, axis=-1)
    v = jnp.dot(ckvn_bf, w_uv_ref[...], preferred_element_type=F32).astype(BF16)
    return k, v


def _inproj_body(x_ref, sh_ref, sc_ref, nm_ref, win_ref, cqn_ref, ckvn_ref, wuq_ref, wuk_ref, wuv_ref,
                 qga_ref, qgb_ref, cosq_ref, sinq_ref, mq_ref, mk_ref, kgk_ref, gkr_ref, cosk_ref, sink_ref,
                 tile_ref, lbl_ref, wa2_ref, ba_ref,
                 ckv_o, kr_o, q_o, k_o, v_o, hrec_o, grec_o, gate_o, *, layer):
    bb, tt, d = x_ref.shape
    tm = bb * tt
    x = x_ref[...]
    h = _rms(x) * nm_ref[...] * (1.0 + sc_ref[...]) + sh_ref[...]
    z = jnp.dot(h.reshape(tm, d).astype(BF16), win_ref[...], preferred_element_type=F32)

    def seg(name, w):
        return z[:, _Z[name]:_Z[name] + w]

    cqn = _rms(seg("cq", MLA_Q_RANK)) * cqn_ref[...]
    ckvn = _rms(seg("ckv", MLA_KV_RANK)) * ckvn_ref[...]
    ckv_o[...] = ckvn.reshape(bb, tt, MLA_KV_RANK)
    misc = seg("misc", LANES)

    qab = jnp.dot(cqn.astype(BF16), wuq_ref[...], preferred_element_type=F32)
    width = MLA_HEADS * HEAD_SLAB
    ca = cosq_ref[...] * qga_ref[...]
    sb = sinq_ref[...] * qgb_ref[...]
    ca2 = jnp.concatenate([ca, ca], axis=-1)
    sb2 = jnp.concatenate([sb, sb], axis=-1)
    for hp in range(MLA_HEADS // 2):
        sl = slice(2 * HEAD_SLAB * hp, 2 * HEAD_SLAB * (hp + 1))
        qa = qab[:, sl]
        qb = qab[:, width + sl.start:width + sl.stop]
        msq = jnp.dot((qa * qa).astype(BF16), mq_ref[...], preferred_element_type=F32)
        qo = lax.rsqrt(msq + EPS) * (qa * ca2 + qb * sb2) * MLA_SCALE
        q_o[:, :, sl] = qo.astype(BF16).reshape(bb, tt, 2 * HEAD_SLAB)

    kr = misc[:, 0:MLA_ROPE]
    krn = _rms(kr) * gkr_ref[...]
    half = MLA_ROPE // 2
    swapped = jnp.concatenate([krn[:, half:], krn[:, :half]], axis=-1)
    kro = krn * cosk_ref[...] + swapped * sink_ref[...]
    kr_o[...] = kro.reshape(bb, tt, MLA_ROPE)
    k, v = _mla_kv(ckvn.astype(BF16), kro.astype(BF16), wuk_ref, wuv_ref, mk_ref, kgk_ref, tile_ref)
    k_o[...] = k.reshape(bb, tt, width)
    v_o[...] = v.reshape(bb, tt, MLA_WIDTH)

    lg = lbl_ref[...]
    e = jnp.exp(lg - jnp.max(lg, axis=0, keepdims=True))
    p = e / jnp.sum(e, axis=0, keepdims=True)
    lb = jnp.sum(p[0:layer + 1], axis=0, keepdims=True) - p[0:1]
    f = lb + (1.0 - lb) * _sigmoid(seg("hf", 256))
    hrec_o[:, :, 0:256] = seg("hq", 256).reshape(bb, tt, 256)
    hrec_o[:, :, 256:512] = (1.0 - f).reshape(bb, tt, 256)
    hrec_o[:, :, 512:768] = jnp.log(f).reshape(bb, tt, 256)
    hrec_o[:, :, 768:1024] = seg("hi", 256).reshape(bb, tt, 256)

    ga = misc[:, MLA_ROPE:MLA_ROPE + GLA_RANK]
    apre = jnp.dot(ga.astype(BF16), wa2_ref[...], preferred_element_type=F32) + ba_ref[...]
    grec_o[:, :, 0:128] = (seg("gq", 128) * (GLA_DK ** -0.5)).reshape(bb, tt, 128)
    grec_o[:, :, 128:256] = seg("gk", 128).reshape(bb, tt, 128)
    grec_o[:, :, 256:384] = (_log_sigmoid(apre) / GLA_TAU).reshape(bb, tt, 128)
    grec_o[:, :, 384:640] = seg("gv", 256).reshape(bb, tt, 256)

    gate_o[:, :, 0:256] = _silu(seg("hg", 256)).reshape(bb, tt, 256)
    gate_o[:, :, 256:512] = _silu(seg("gg", 256)).reshape(bb, tt, 256)


def _inproj_call(x, sh, sc, lw, tabs, bb, tt, layer):
    b, t, d = x.shape
    tm = bb * tt
    grid = (b // bb, t // tt)
    tok = lambda w: pl.BlockSpec((bb, tt, w), lambda i, j: (i, j, 0))
    modspec = pl.BlockSpec((bb, 1, d), lambda i, j: (i, 0, 0))
    tab = lambda w: pl.BlockSpec((tm, w), lambda i, j: (j, 0))
    consts = [lw["nm"], lw["win"], lw["cqn"], lw["ckvn"], lw["wuq"], lw["wuk"], lw["wuv"], lw["qga"], lw["qgb"]]
    consts2 = [lw["mq"], lw["mk"], lw["kgk"], lw["gkr"]]
    consts3 = [lw["tile"], lw["lbl"], lw["wa2"], lw["ba"]]
    in_specs = ([tok(d), modspec, modspec] + [_const_spec(a.shape) for a in consts]
                + [tab(LANES), tab(LANES)] + [_const_spec(a.shape) for a in consts2]
                + [tab(MLA_ROPE), tab(MLA_ROPE)] + [_const_spec(a.shape) for a in consts3])
    widths = [(MLA_KV_RANK, F32), (MLA_ROPE, F32), (MLA_HEADS * HEAD_SLAB, BF16), (MLA_HEADS * HEAD_SLAB, BF16),
              (MLA_WIDTH, BF16), (1024, F32), (640, F32), (512, F32)]
    return pl.pallas_call(
        functools.partial(_inproj_body, layer=layer),
        grid=grid,
        in_specs=in_specs,
        out_specs=[tok(w) for w, _ in widths],
        out_shape=[jax.ShapeDtypeStruct((b, t, w), dt) for w, dt in widths],
        compiler_params=_params(("arbitrary", "arbitrary")),
        name="inproj",
    )(x, sh, sc, *consts, tabs["cosq"], tabs["sinq"], *consts2, tabs["cosk"], tabs["sink"], *consts3)


def _attend(q, k, v, m_ref, l_ref, acc_ref, mask):
    tq = q.shape[0]
    low = lax.broadcasted_iota(jnp.int32, (tq, LANES), 1) < MLA_V
    for hp in range(MLA_HEADS // 2):
        vp = v[:, LANES * hp:LANES * (hp + 1)]
        pvs, alphas = [], []
        for e in range(2):
            hd = 2 * hp + e
            sl = slice(HEAD_SLAB * hd, HEAD_SLAB * (hd + 1))
            s = _dot_nt(q[:, sl], k[:, sl])
            if mask is not None:
                s = jnp.where(mask, s, NEG_INF)
            m_prev = m_ref[hd]
            m_new = jnp.maximum(m_prev, jnp.max(s, axis=-1, keepdims=True))
            p = jnp.exp(s - m_new)
            alpha = jnp.exp(m_prev - m_new)
            l_ref[hd] = alpha * l_ref[hd] + jnp.sum(p, axis=-1, keepdims=True)
            m_ref[hd] = m_new
            pvs.append(jnp.dot(p.astype(BF16), vp, preferred_element_type=F32))
            alphas.append(alpha)
        sl2 = slice(LANES * hp, LANES * (hp + 1))
        acc_ref[:, sl2] = jnp.where(low, alphas[0], alphas[1]) * acc_ref[:, sl2] + jnp.where(low, pvs[0], pvs[1])


def _attn_init(m_ref, l_ref, acc_ref):
    m_ref[...] = jnp.full(m_ref.shape, NEG_INF, F32)
    l_ref[...] = jnp.zeros(l_ref.shape, F32)
    acc_ref[...] = jnp.zeros(acc_ref.shape, F32)


def _attn_finish(l_ref, acc_ref):
    tq = acc_ref.shape[0]
    low = lax.broadcasted_iota(jnp.int32, (tq, LANES), 1) < MLA_V
    outs = []
    for hp in range(MLA_HEADS // 2):
        l2 = jnp.where(low, l_ref[2 * hp], l_ref[2 * hp + 1])
        outs.append(acc_ref[:, LANES * hp:LANES * (hp + 1)] / l2)
    return jnp.concatenate(outs, axis=-1)


def _attn_prompt_body(q_ref, k_ref, v_ref, o_ref, m_ref, l_ref, acc_ref):
    i = pl.program_id(1)
    j = pl.program_id(2)
    tq = q_ref.shape[1]
    tk = k_ref.shape[1]

    @pl.when(j == 0)
    def _():
        _attn_init(m_ref, l_ref, acc_ref)

    @pl.when(j < i)
    def _():
        _attend(q_ref[0], k_ref[0], v_ref[0], m_ref, l_ref, acc_ref, None)

    @pl.when(j == i)
    def _():
        rows = lax.broadcasted_iota(jnp.int32, (tq, tk), 0)
        cols = lax.broadcasted_iota(jnp.int32, (tq, tk), 1)
        vis = (cols >> 6) <= (rows >> 6)
        _attend(q_ref[0], k_ref[0], v_ref[0], m_ref, l_ref, acc_ref, vis)
        o_ref[0] = _attn_finish(l_ref, acc_ref)


def _attn_prompt_call(q, k, v, tile):
    b, t, _ = q.shape
    n = t // tile
    kv_map = lambda bi, i, j: (bi, jnp.minimum(i, j), 0)
    return pl.pallas_call(
        _attn_prompt_body,
        grid=(b, n, n),
        in_specs=[pl.BlockSpec((1, tile, q.shape[2]), lambda bi, i, j: (bi, i, 0)),
                  pl.BlockSpec((1, tile, k.shape[2]), kv_map),
                  pl.BlockSpec((1, tile, v.shape[2]), kv_map)],
        out_specs=pl.BlockSpec((1, tile, MLA_WIDTH), lambda bi, i, j: (bi, i, 0)),
        out_shape=jax.ShapeDtypeStruct((b, t, MLA_WIDTH), F32),
        scratch_shapes=[pltpu.VMEM((MLA_HEADS, tile, 1), F32), pltpu.VMEM((MLA_HEADS, tile, 1), F32),
                        pltpu.VMEM((tile, MLA_WIDTH), F32)],
        compiler_params=_params(("arbitrary", "arbitrary", "arbitrary")),
        name="attn_prompt",
    )(q, k, v)


def _attn_sample_body(q_ref, kn_ref, vn_ref, ckv_ref, kr_ref, wuk_ref, wuv_ref, mk_ref, kgk_ref, tile_ref,
                      o_ref, m_ref, l_ref, acc_ref):
    j = pl.program_id(1)

    @pl.when(j == 0)
    def _():
        _attn_init(m_ref, l_ref, acc_ref)
        _attend(q_ref[0], kn_ref[0], vn_ref[0], m_ref, l_ref, acc_ref, None)

    k, v = _mla_kv(ckv_ref[0, 0].astype(BF16), kr_ref[0, 0].astype(BF16), wuk_ref, wuv_ref, mk_ref, kgk_ref, tile_ref)
    _attend(q_ref[0], k, v, m_ref, l_ref, acc_ref, None)

    @pl.when(j == pl.num_programs(1) - 1)
    def _():
        o_ref[0] = _attn_finish(l_ref, acc_ref)


def _attn_sample_call(q, k_new, v_new, cache_ckv, cache_kr, lw, layer, tkc):
    b, t, _ = q.shape
    past = cache_ckv.shape[2]
    consts = [lw["wuk"], lw["wuv"], lw["mk"], lw["kgk"], lw["tile"]]
    new = lambda w: pl.BlockSpec((1, t, w), lambda bi, j: (bi, 0, 0))
    return pl.pallas_call(
        _attn_sample_body,
        grid=(b, past // tkc),
        in_specs=[new(q.shape[2]), new(k_new.shape[2]), new(v_new.shape[2]),
                  pl.BlockSpec((1, 1, tkc, MLA_KV_RANK), lambda bi, j: (layer, bi, j, 0)),
                  pl.BlockSpec((1, 1, tkc, MLA_ROPE), lambda bi, j: (layer, bi, j, 0))]
                 + [_const_spec(a.shape) for a in consts],
        out_specs=new(MLA_WIDTH),
        out_shape=jax.ShapeDtypeStruct((b, t, MLA_WIDTH), F32),
        scratch_shapes=[pltpu.VMEM((MLA_HEADS, t, 1), F32), pltpu.VMEM((MLA_HEADS, t, 1), F32),
                        pltpu.VMEM((t, MLA_WIDTH), F32)],
        compiler_params=_params(("arbitrary", "arbitrary")),
        name="attn_sample",
    )(q, k_new, v_new, cache_ckv, cache_kr, *consts)


def _recur_body(rec_ref, s0_ref, e2_ref, sel_ref, bd_ref, o_ref, s_o, s_ref, *, hk):
    t = pl.program_id(1)
    tt = rec_ref.shape[1]

    @pl.when(t == 0)
    def _():
        s_ref[...] = s0_ref[0]

    rec = rec_ref[0]
    q = rec[:, 0:hk]
    k = rec[:, hk:2 * hk]
    g = rec[:, 2 * hk:3 * hk]
    v = rec[:, 3 * hk:3 * hk + 256]

    rows = lax.broadcasted_iota(jnp.int32, (2 * tt, tt), 0)
    cols = lax.broadcasted_iota(jnp.int32, (2 * tt, tt), 1)
    top = rows < tt
    rr = jnp.where(top, rows, rows - tt)
    ones = ((rr >> 4) == (cols >> 4)) & (jnp.logical_not(top) | (cols <= rr))
    lmat = jnp.where(ones, 1.0, 0.0).astype(BF16)
    g1 = g.astype(BF16)
    r1 = g - g1.astype(F32)
    g2 = r1.astype(BF16)
    g3 = (r1 - g2.astype(F32)).astype(BF16)
    bb = jnp.dot(lmat, jnp.concatenate([g1, g2, g3], axis=-1), preferred_element_type=F32)
    bsum = (bb[:, 2 * hk:3 * hk] + bb[:, hk:2 * hk]) + bb[:, 0:hk]
    b = bsum[0:tt]
    btot = bsum[tt:2 * tt]

    qe = (q * jnp.exp(b)).astype(BF16)
    ke = (k * jnp.exp(btot - b)).astype(BF16)
    dec = jnp.exp(btot)
    vb = v.astype(BF16)

    for n in range(tt // SUB):
        r0 = SUB * n
        blk = slice(r0, r0 + SUB)
        s = s_ref[...]
        o_inter = jnp.dot(qe[blk], s.astype(BF16), preferred_element_type=F32)
        bj, kj, vj = b[blk], k[blk], v[blk]
        xs = []
        for i0 in range(SUB):
            bi = b[r0 + i0:r0 + i0 + 1]
            qi = q[r0 + i0:r0 + i0 + 1]
            xs.append((jnp.exp(jnp.minimum(bi - bj, 0.0)) * (kj * qi)).astype(BF16))
        x = jnp.concatenate(xs, axis=0)
        r = jnp.dot(x, e2_ref[...], preferred_element_type=F32)
        y = (r.reshape(SUB, SUB, 256) * vj[None]).reshape(SUB * SUB, 256).astype(BF16)
        o_diag = jnp.dot(sel_ref[...], y, preferred_element_type=F32)
        o_ref[0, blk, :] = o_inter + o_diag
        u = _dot_tn(ke[blk], vb[blk])
        dcol = dec[r0:r0 + 8].T[:, 0:1]
        s_ref[...] = dcol * s + u * bd_ref[...]

    @pl.when(t == pl.num_programs(1) - 1)
    def _():
        s_o[0] = s_ref[...]


def _recur_call(rec, s0, hk, tt, name):
    b, t, _ = rec.shape
    w = 3 * hk + 256
    heads = 4
    dk = hk // heads
    e2 = np.kron(np.eye(heads), np.ones((dk, 64))).astype(np.float32)
    sel = np.zeros((SUB, SUB * SUB), np.float32)
    for i in range(SUB):
        sel[i, SUB * i:SUB * i + i + 1] = 1.0
    return pl.pallas_call(
        functools.partial(_recur_body, hk=hk),
        grid=(b, t // tt),
        in_specs=[pl.BlockSpec((1, tt, w), lambda bi, ti: (bi, ti, 0)),
                  pl.BlockSpec((1, hk, 256), lambda bi, ti: (bi, 0, 0)),
                  _const_spec((hk, 256)), _const_spec((SUB, SUB * SUB)), _const_spec((hk, 256))],
        out_specs=[pl.BlockSpec((1, tt, 256), lambda bi, ti: (bi, ti, 0)),
                   pl.BlockSpec((1, hk, 256), lambda bi, ti: (bi, 0, 0))],
        out_shape=[jax.ShapeDtypeStruct((b, t, 256), F32), jax.ShapeDtypeStruct((b, hk, 256), F32)],
        scratch_shapes=[pltpu.VMEM((hk, 256), F32)],
        compiler_params=_params(("arbitrary", "arbitrary")),
        name=name,
    )(rec, s0, jnp.asarray(e2, BF16), jnp.asarray(sel, BF16), jnp.asarray(e2, F32))


def _outffn_body(x_ref, om_ref, oh_ref, og_ref, gate_ref, g1_ref, sh2_ref, sc2_ref, g2_ref,
                 mon_ref, hon_ref, gon_ref, nf_ref, m64_ref, wo_ref, wfi_ref, wfo_ref, y_ref, *, ff_chunk):
    bb, tt, d = x_ref.shape
    tm = bb * tt
    om = om_ref[...].reshape(tm, MLA_WIDTH)
    omn = _rms(om) * mon_ref[...]

    def headnorm(o, gain_ref, gate):
        ms = jnp.dot((o * o).astype(BF16), m64_ref[...], preferred_element_type=F32)
        return o * lax.rsqrt(ms + EPS) * gain_ref[...] * gate

    gate = gate_ref[...].reshape(tm, 512)
    ohn = headnorm(oh_ref[...].reshape(tm, 256), hon_ref, gate[:, 0:256])
    ogn = headnorm(og_ref[...].reshape(tm, 256), gon_ref, gate[:, 256:512])
    mix_in = jnp.concatenate([omn, ohn, ogn], axis=-1).astype(BF16)
    mix = jnp.dot(mix_in, wo_ref[...], preferred_element_type=F32)
    x1 = x_ref[...] + g1_ref[...] * mix.reshape(bb, tt, d)
    h2 = (_rms(x1) * nf_ref[...] * (1.0 + sc2_ref[...]) + sh2_ref[...]).reshape(tm, d).astype(BF16)
    acc = jnp.zeros((tm, d), F32)
    for c in range(D_FF // ff_chunk):
        a = jnp.dot(h2, wfi_ref[:, ff_chunk * c:ff_chunk * (c + 1)], preferred_element_type=F32)
        u = jnp.dot(h2, wfi_ref[:, D_FF + ff_chunk * c:D_FF + ff_chunk * (c + 1)], preferred_element_type=F32)
        act = (_silu(a) * u).astype(BF16)
        acc = acc + jnp.dot(act, wfo_ref[ff_chunk * c:ff_chunk * (c + 1), :], preferred_element_type=F32)
    y_ref[...] = x1 + g2_ref[...] * acc.reshape(bb, tt, d)


def _outffn_call(x, om, oh, og, gate, mods, lw, bb, tt):
    b, t, d = x.shape
    tok = lambda w: pl.BlockSpec((bb, tt, w), lambda i, j: (i, j, 0))
    modspec = pl.BlockSpec((bb, 1, d), lambda i, j: (i, 0, 0))
    consts = [lw["mon"], lw["hon"], lw["gon"], lw["nf"], lw["m64"], lw["wo"], lw["wfi"], lw["wfo"]]
    single = lambda a: pl.BlockSpec(a.shape, lambda *_: (0,) * a.ndim, pipeline_mode=pl.Buffered(1))
    return pl.pallas_call(
        functools.partial(_outffn_body, ff_chunk=256),
        grid=(b // bb, t // tt),
        in_specs=[tok(d), tok(MLA_WIDTH), tok(256), tok(256), tok(512)] + [modspec] * 4 + [single(a) for a in consts],
        out_specs=tok(d),
        out_shape=jax.ShapeDtypeStruct((b, t, d), F32),
        compiler_params=_params(("arbitrary", "arbitrary")),
        name="outffn",
    )(x, om, oh, og, gate, *mods, *consts)


def _layer_weights(l, w_in, norm_mix, norm_ffn, mla_cq_norm, mla_ckv_norm, mla_w_uq, mla_w_uk, mla_w_uv, mla_q_norm,
                   mla_k_norm, mla_out_norm, hg_lb_logits, hg_out_norm, gla_w_a2, gla_b_a, gla_out_norm, w_o,
                   w_ffn_in, w_ffn_out):
    wi = w_in[l]
    d = wi.shape[0]
    o = dict(cq=0, ckv=256, kr=512, hq=544, hf=800, hi=1056, hg=1312, gq=1568, gk=1696, gv=1824, gg=2080, ga=2336)
    cols = [wi[:, o["cq"]:o["cq"] + 256], wi[:, o["ckv"]:o["ckv"] + 256], wi[:, o["hq"]:o["hq"] + 256],
            wi[:, o["hf"]:o["hf"] + 256], wi[:, o["hi"]:o["hi"] + 256], wi[:, o["hg"]:o["hg"] + 256],
            wi[:, o["gq"]:o["gq"] + 128], wi[:, o["gk"]:o["gk"] + 128], wi[:, o["gv"]:o["gv"] + 256],
            wi[:, o["gg"]:o["gg"] + 256], wi[:, o["kr"]:o["kr"] + 32], wi[:, o["ga"]:o["ga"] + 16],
            jnp.zeros((d, LANES - MLA_ROPE - GLA_RANK), wi.dtype)]
    win = jnp.concatenate(cols, axis=1).astype(BF16)
    assert win.shape[1] == ZCOLS

    uq = mla_w_uq[l]
    r = uq.shape[0]
    half = MLA_ROPE // 2
    zpad = jnp.zeros((r, MLA_HEADS, HEAD_SLAB - MLA_NOPE - MLA_ROPE), uq.dtype)
    uq_a = jnp.concatenate([uq, zpad], axis=-1).reshape(r, -1)
    uq_b = jnp.concatenate([jnp.zeros((r, MLA_HEADS, MLA_NOPE), uq.dtype), uq[..., MLA_NOPE + half:],
                            uq[..., MLA_NOPE:MLA_NOPE + half], zpad], axis=-1).reshape(r, -1)
    wuq = jnp.concatenate([uq_a, uq_b], axis=1).astype(BF16)
    uk = mla_w_uk[l]
    wuk = jnp.concatenate([uk, jnp.zeros((r, MLA_HEADS, HEAD_SLAB - MLA_NOPE), uk.dtype)], axis=-1)
    wuk = wuk.reshape(r, -1).astype(BF16)
    wuv = mla_w_uv[l].reshape(r, -1).astype(BF16)

    qn, kn = mla_q_norm[l], mla_k_norm[l]
    z32 = jnp.zeros((HEAD_SLAB - MLA_NOPE - MLA_ROPE,), F32)
    qga = jnp.concatenate([qn, z32])[None]
    qgb = jnp.concatenate([jnp.zeros((MLA_NOPE,), F32), qn[MLA_NOPE + half:], qn[MLA_NOPE:MLA_NOPE + half], z32])[None]
    kgk = jnp.concatenate([kn[:MLA_NOPE], jnp.zeros((HEAD_SLAB - MLA_NOPE,), F32)])[None]
    gkr = kn[MLA_NOPE:][None]

    blk = np.zeros((HEAD_SLAB, HEAD_SLAB), np.float32)
    blk[:MLA_NOPE, :MLA_NOPE] = 1.0 / MLA_NOPE
    mk = np.kron(np.eye(2), blk)
    blk_q = blk.copy()
    blk_q[MLA_NOPE:MLA_NOPE + MLA_ROPE, MLA_NOPE:MLA_NOPE + MLA_ROPE] = 1.0 / MLA_ROPE
    mq = np.kron(np.eye(2), blk_q)
    m64 = np.kron(np.eye(4), np.full((64, 64), 1.0 / 64, np.float32))
    tile = np.zeros((MLA_ROPE, MLA_HEADS * HEAD_SLAB), np.float32)
    for hd in range(MLA_HEADS):
        tile[np.arange(MLA_ROPE), HEAD_SLAB * hd + MLA_NOPE + np.arange(MLA_ROPE)] = 1.0

    return dict(
        nm=norm_mix[l][None], win=win, cqn=mla_cq_norm[l][None], ckvn=mla_ckv_norm[l][None], wuq=wuq, wuk=wuk,
        wuv=wuv, qga=qga, qgb=qgb, mq=jnp.asarray(mq, BF16), mk=jnp.asarray(mk, BF16), kgk=kgk, gkr=gkr,
        tile=jnp.asarray(tile, BF16), lbl=hg_lb_logits, wa2=gla_w_a2[l].astype(BF16), ba=gla_b_a[l][None],
        mon=mla_out_norm[l][None], hon=jnp.tile(hg_out_norm[l], HG_HEADS)[None],
        gon=jnp.tile(gla_out_norm[l], GLA_HEADS)[None], nf=norm_ffn[l][None], m64=jnp.asarray(m64, BF16),
        wo=w_o[l].astype(BF16), wfi=w_ffn_in[l].astype(BF16), wfo=w_ffn_out[l].astype(BF16))


def _rope_tables(pos, reps):
    inv = ROPE_THETA ** (-(jnp.arange(0, MLA_ROPE, 2, dtype=F32) / MLA_ROPE))
    ang = pos.astype(F32)[:, None] * inv[None, :]
    cos, sin = jnp.cos(ang), jnp.sin(ang)
    n = pos.shape[0]
    pad = HEAD_SLAB - MLA_NOPE - MLA_ROPE
    cosq = jnp.concatenate([jnp.ones((n, MLA_NOPE), F32), cos, cos, jnp.zeros((n, pad), F32)], axis=1)
    sinq = jnp.concatenate([jnp.zeros((n, MLA_NOPE), F32), -sin, sin, jnp.zeros((n, pad), F32)], axis=1)
    cosk = jnp.concatenate([cos, cos], axis=1)
    sink = jnp.concatenate([-sin, sin], axis=1)
    tabs = dict(cosq=cosq, sinq=sinq, cosk=cosk, sink=sink)
    return {k: jnp.tile(v, (reps, 1)) for k, v in tabs.items()}


def _block_diag_state(state):
    b, hh, dk, dv = state.shape
    s = jnp.zeros((b, hh * dk, hh * dv), state.dtype)
    for hd in range(hh):
        s = s.at[:, hd * dk:(hd + 1) * dk, hd * dv:(hd + 1) * dv].set(state[:, hd])
    return s


def _head_states(s_bd, hh):
    dk, dv = s_bd.shape[1] // hh, s_bd.shape[2] // hh
    return jnp.stack([s_bd[:, hd * dk:(hd + 1) * dk, hd * dv:(hd + 1) * dv] for hd in range(hh)], axis=1)


def _group_layer(x, mods, lw, tabs, layer, bb, tt, rtt, s_hg, s_gla, attn_fn):
    sh1, sc1, g1, sh2, sc2, g2 = mods
    ckv, kr, q, k, v, hrec, grec, gate = _inproj_call(x, sh1, sc1, lw, tabs, bb, tt, layer)
    om = attn_fn(q, k, v)
    oh, shg = _recur_call(hrec, s_hg, HG_HEADS * HG_DK, rtt, "recur_hgrn")
    og, sgl = _recur_call(grec, s_gla, GLA_HEADS * GLA_DK, rtt, "recur_gla")
    y = _outffn_call(x, om, oh, og, gate, (g1, sh2, sc2, g2), lw, bb, tt)
    return y, ckv, kr, _head_states(shg, HG_HEADS), _head_states(sgl, GLA_HEADS)


def kernel(x_prompt, x_sample, c_prompt, c_sample, cache_mla_ckv, cache_mla_krope, state_hgrn, state_gla, w_mod, b_mod, norm_mix, norm_ffn, w_in, mla_cq_norm, mla_ckv_norm, mla_w_uq, mla_w_uk, mla_w_uv, mla_q_norm, mla_k_norm, mla_out_norm, hg_lb_logits, hg_out_norm, gla_w_a2, gla_b_a, gla_out_norm, w_o, w_ffn_in, w_ffn_out):
    bp, t_p, d = x_prompt.shape
    bs, t_s, _ = x_sample.shape
    depth = w_mod.shape[0]
    tm = 512
    tt_p = min(tm, t_p)
    bb_s = max(1, min(bs, tm // t_s))
    rtt_p = min(128, t_p)
    rtt_s = min(128, t_s)
    attn_tile = min(512, t_p)
    tkc = min(1024, cache_mla_ckv.shape[2])

    tabs_p = _rope_tables(jnp.arange(t_p, dtype=jnp.int32), 1)
    tabs_s = _rope_tables(PAST_LEN + jnp.arange(t_s, dtype=jnp.int32), bb_s)

    n_c = bp + bs
    c_rows = -(-n_c // 16) * 16
    c_pad = jnp.concatenate([c_prompt, c_sample, jnp.zeros((c_rows - n_c, d), c_prompt.dtype)], axis=0)

    xp, xs = x_prompt, x_sample
    outs = [[] for _ in range(8)]
    for l in range(depth):
        lw = _layer_weights(l, w_in, norm_mix, norm_ffn, mla_cq_norm, mla_ckv_norm, mla_w_uq, mla_w_uk, mla_w_uv,
                            mla_q_norm, mla_k_norm, mla_out_norm, hg_lb_logits, hg_out_norm, gla_w_a2, gla_b_a,
                            gla_out_norm, w_o, w_ffn_in, w_ffn_out)
        mod = _mod_call(c_pad, w_mod[l], b_mod[l][None])
        mods_p = [mod[0:bp, i * d:(i + 1) * d][:, None, :] for i in range(6)]
        mods_s = [mod[bp:n_c, i * d:(i + 1) * d][:, None, :] for i in range(6)]

        zero_hg = jnp.zeros((bp, HG_HEADS * HG_DK, HG_HEADS * HG_DV), F32)
        zero_gla = jnp.zeros((bp, GLA_HEADS * GLA_DK, GLA_HEADS * GLA_DV), F32)
        xp, a, b_, c_, d_ = _group_layer(xp, mods_p, lw, tabs_p, l, 1, tt_p, rtt_p, zero_hg, zero_gla,
                                         lambda q, k, v: _attn_prompt_call(q, k, v, attn_tile))
        for lst, val in zip(outs[0:4], (a, b_, c_, d_)):
            lst.append(val)
        xs, a, b_, c_, d_ = _group_layer(xs, mods_s, lw, tabs_s, l, bb_s, t_s, rtt_s,
                                         _block_diag_state(state_hgrn[l]), _block_diag_state(state_gla[l]),
                                         lambda q, k, v: _attn_sample_call(q, k, v, cache_mla_ckv, cache_mla_krope,
                                                                           lw, l, tkc))
        for lst, val in zip(outs[4:8], (a, b_, c_, d_)):
            lst.append(val)

    return (xp, xs) + tuple(jnp.stack(o) for o in outs)
```

```python
import functools

import numpy as np
import jax
import jax.numpy as jnp
from jax import lax
from jax.experimental import pallas as pl
from jax.experimental.pallas import tpu as pltpu

F32, BF16 = jnp.float32, jnp.bfloat16

D_MODEL = 1024
PAST_LEN = 4096
CHUNK = 64
EPS = 1e-6
NEG_INF = -1e30
ROPE_THETA = 10000.0
MLA_HEADS = 8
MLA_NOPE = 64
MLA_ROPE = 32
MLA_V = 64
MLA_Q_RANK = 256
MLA_KV_RANK = 256
MLA_WIDTH = MLA_HEADS * MLA_V
MLA_SCALE = (MLA_NOPE + MLA_ROPE) ** -0.5
LOG2E = 1.4426950408889634
HG_HEADS = 4
HG_DK = 64
HG_DV = 64
GLA_HEADS = 4
GLA_DK = 32
GLA_DV = 64
GLA_RANK = 16
GLA_TAU = 16.0
D_FF = 2816

LANES = 128
HEAD_SLAB = 128
SUB = 16
ZCOLS = 2432
VMEM_LIMIT = 56 * 1024 * 1024

_Z = dict(cq=0, ckv=256, hq=512, hf=768, hi=1024, hg=1280, gq=1536, gk=1664, gv=1792, gg=2048, misc=2304)


def _bdot(a, b):
    return jnp.dot(a.astype(BF16), b.astype(BF16), preferred_element_type=F32)


def _dot_nt(a, b):
    return lax.dot_general(a, b, (((1,), (1,)), ((), ())), preferred_element_type=F32)


def _dot_tn(a, b):
    return lax.dot_general(a, b, (((0,), (0,)), ((), ())), preferred_element_type=F32)


def _sigmoid(x):
    return 1.0 / (1.0 + jnp.exp(-x))


def _silu(x):
    return x * _sigmoid(x)


def _log_sigmoid(x):
    return jnp.minimum(x, 0.0) - jnp.log1p(jnp.exp(-jnp.abs(x)))


def _rms(x):
    return x * lax.rsqrt(jnp.mean(x * x, axis=-1, keepdims=True) + EPS)


def _params(sem):
    return pltpu.CompilerParams(dimension_semantics=sem, vmem_limit_bytes=VMEM_LIMIT)


def _const_spec(shape):
    nd = len(shape)
    return pl.BlockSpec(shape, lambda *_: (0,) * nd)


def _mod_body(c_ref, w_ref, b_ref, o_ref):
    o_ref[...] = _bdot(_silu(c_ref[...]), w_ref[...]) + b_ref[...]


def _mod_call(c_pad, w_mod, b_mod):
    rows, d = c_pad.shape
    n = w_mod.shape[1]
    tn = 1024
    return pl.pallas_call(
        _mod_body,
        grid=(n // tn,),
        in_specs=[pl.BlockSpec((rows, d), lambda j: (0, 0)),
                  pl.BlockSpec((d, tn), lambda j: (0, j)),
                  pl.BlockSpec((1, tn), lambda j: (0, j))],
        out_specs=pl.BlockSpec((rows, tn), lambda j: (0, j)),
        out_shape=jax.ShapeDtypeStruct((rows, n), F32),
        compiler_params=_params(("arbitrary",)),
        name="mod",
    )(c_pad, w_mod, b_mod)


def _mla_kv(ckvn_bf, kro_bf, w_uk_ref, w_uv_ref, mk_ref, kgk_ref, tile_ref):
    kn = jnp.dot(ckvn_bf, w_uk_ref[...], preferred_element_type=F32)
    krt = jnp.dot(kro_bf, tile_ref[...], preferred_element_type=F32)
    kgk2 = jnp.concatenate([kgk_ref[...], kgk_ref[...]], axis=-1)
    parts = []
    for hp in range(MLA_HEADS // 2):
        sl = slice(2 * HEAD_SLAB * hp, 2 * HEAD_SLAB * (hp + 1))
        sk = kn[:, sl]
        msk = jnp.dot((sk * sk).astype(BF16), mk_ref[...], preferred_element_type=F32)
        parts.append((sk * lax.rsqrt(msk + EPS) * kgk2 + krt[:, sl]).astype(BF16))
    k = jnp.concatenate(parts, axis=-1)
    v = jnp.dot(ckvn_bf, w_uv_ref[...], preferred_element_type=F32).astype(BF16)
    return k, v


def _inproj_body(x_ref, sh_ref, sc_ref, nm_ref, win_ref, cqn_ref, ckvn_ref, wuq_ref, wuk_ref, wuv_ref,
                 qga_ref, qgb_ref, cosq_ref, sinq_ref, mq_ref, mk_ref, kgk_ref, gkr_ref, cosk_ref, sink_ref,
                 tile_ref, lbl_ref, wa2_ref, ba_ref,
                 ckv_o, kr_o, q_o, k_o, v_o, hrec_o, grec_o, gate_o, *, layer):
    bb, tt, d = x_ref.shape
    tm = bb * tt
    x = x_ref[...]
    h = _rms(x) * nm_ref[...] * (1.0 + sc_ref[...]) + sh_ref[...]
    z = jnp.dot(h.reshape(tm, d).astype(BF16), win_ref[...], preferred_element_type=F32)

    def seg(name, w):
        return z[:, _Z[name]:_Z[name] + w]

    cqn = _rms(seg("cq", MLA_Q_RANK)) * cqn_ref[...]
    ckvn = _rms(seg("ckv", MLA_KV_RANK)) * ckvn_ref[...]
    ckv_o[...] = ckvn.reshape(bb, tt, MLA_KV_RANK)
    misc = seg("misc", LANES)

    qab = jnp.dot(cqn.astype(BF16), wuq_ref[...], preferred_element_type=F32)
    width = MLA_HEADS * HEAD_SLAB
    ca = cosq_ref[...] * qga_ref[...]
    sb = sinq_ref[...] * qgb_ref[...]
    ca2 = jnp.concatenate([ca, ca], axis=-1)
    sb2 = jnp.concatenate([sb, sb], axis=-1)
    for hp in range(MLA_HEADS // 2):
        sl = slice(2 * HEAD_SLAB * hp, 2 * HEAD_SLAB * (hp + 1))
        qa = qab[:, sl]
        qb = qab[:, width + sl.start:width + sl.stop]
        msq = jnp.dot((qa * qa).astype(BF16), mq_ref[...], preferred_element_type=F32)
        qo = lax.rsqrt(msq + EPS) * (qa * ca2 + qb * sb2) * (MLA_SCALE * LOG2E)
        q_o[:, :, sl] = qo.astype(BF16).reshape(bb, tt, 2 * HEAD_SLAB)

    kr = misc[:, 0:MLA_ROPE]
    krn = _rms(kr) * gkr_ref[...]
    half = MLA_ROPE // 2
    swapped = jnp.concatenate([krn[:, half:], krn[:, :half]], axis=-1)
    kro = krn * cosk_ref[...] + swapped * sink_ref[...]
    kr_o[...] = kro.reshape(bb, tt, MLA_ROPE)
    k, v = _mla_kv(ckvn.astype(BF16), kro.astype(BF16), wuk_ref, wuv_ref, mk_ref, kgk_ref, tile_ref)
    k_o[...] = k.reshape(bb, tt, width)
    v_o[...] = v.reshape(bb, tt, MLA_WIDTH)

    lg = lbl_ref[...]
    e = jnp.exp(lg - jnp.max(lg, axis=0, keepdims=True))
    p = e / jnp.sum(e, axis=0, keepdims=True)
    lb = jnp.sum(p[0:layer + 1], axis=0, keepdims=True) - p[0:1]
    f = lb + (1.0 - lb) * _sigmoid(seg("hf", 256))
    hrec_o[:, :, 0:256] = seg("hq", 256).reshape(bb, tt, 256)
    hrec_o[:, :, 256:512] = (1.0 - f).reshape(bb, tt, 256)
    hrec_o[:, :, 512:768] = jnp.log(f).reshape(bb, tt, 256)
    hrec_o[:, :, 768:1024] = seg("hi", 256).reshape(bb, tt, 256)

    ga = misc[:, MLA_ROPE:MLA_ROPE + GLA_RANK]
    apre = jnp.dot(ga.astype(BF16), wa2_ref[...], preferred_element_type=F32) + ba_ref[...]
    grec_o[:, :, 0:128] = (seg("gq", 128) * (GLA_DK ** -0.5)).reshape(bb, tt, 128)
    grec_o[:, :, 128:256] = seg("gk", 128).reshape(bb, tt, 128)
    grec_o[:, :, 256:384] = (_log_sigmoid(apre) / GLA_TAU).reshape(bb, tt, 128)
    grec_o[:, :, 384:640] = seg("gv", 256).reshape(bb, tt, 256)

    gate_o[:, :, 0:256] = _silu(seg("hg", 256)).reshape(bb, tt, 256)
    gate_o[:, :, 256:512] = _silu(seg("gg", 256)).reshape(bb, tt, 256)


def _inproj_call(x, sh, sc, lw, tabs, bb, tt, layer):
    b, t, d = x.shape
    tm = bb * tt
    grid = (b // bb, t // tt)
    tok = lambda w: pl.BlockSpec((bb, tt, w), lambda i, j: (i, j, 0))
    modspec = pl.BlockSpec((bb, 1, d), lambda i, j: (i, 0, 0))
    tab = lambda w: pl.BlockSpec((tm, w), lambda i, j: (j, 0))
    consts = [lw["nm"], lw["win"], lw["cqn"], lw["ckvn"], lw["wuq"], lw["wuk"], lw["wuv"], lw["qga"], lw["qgb"]]
    consts2 = [lw["mq"], lw["mk"], lw["kgk"], lw["gkr"]]
    consts3 = [lw["tile"], lw["lbl"], lw["wa2"], lw["ba"]]
    in_specs = ([tok(d), modspec, modspec] + [_const_spec(a.shape) for a in consts]
                + [tab(LANES), tab(LANES)] + [_const_spec(a.shape) for a in consts2]
                + [tab(MLA_ROPE), tab(MLA_ROPE)] + [_const_spec(a.shape) for a in consts3])
    widths = [(MLA_KV_RANK, F32), (MLA_ROPE, F32), (MLA_HEADS * HEAD_SLAB, BF16), (MLA_HEADS * HEAD_SLAB, BF16),
              (MLA_WIDTH, BF16), (1024, F32), (640, F32), (512, F32)]
    return pl.pallas_call(
        functools.partial(_inproj_body, layer=layer),
        grid=grid,
        in_specs=in_specs,
        out_specs=[tok(w) for w, _ in widths],
        out_shape=[jax.ShapeDtypeStruct((b, t, w), dt) for w, dt in widths],
        compiler_params=_params(("arbitrary", "arbitrary")),
        name="inproj",
    )(x, sh, sc, *consts, tabs["cosq"], tabs["sinq"], *consts2, tabs["cosk"], tabs["sink"], *consts3)


MAX_UNSHIFTED_LOGIT = 60.0


def _attend_bounded(q, k, v, lf_ref, accf_ref, mask):
    tk = k.shape[0]
    for hp in range(MLA_HEADS // 2):
        vp = v[:, LANES * hp:LANES * (hp + 1)]
        for e in range(2):
            hd = 2 * hp + e
            sl = slice(HEAD_SLAB * hd, HEAD_SLAB * (hd + 1))
            p = jnp.exp2(_dot_nt(q[:, sl], k[:, sl]))
            if mask is not None:
                p = jnp.where(mask, p, 0.0)
            if tk % LANES == 0:
                part = p[:, 0:LANES]
                for c in range(1, tk // LANES):
                    part = part + p[:, LANES * c:LANES * (c + 1)]
            else:
                lane0 = lax.broadcasted_iota(jnp.int32, (p.shape[0], LANES), 1) == 0
                part = jnp.where(lane0, jnp.sum(p, axis=-1, keepdims=True), 0.0)
            lf_ref[hd] += part
            accf_ref[hd] += jnp.dot(p.astype(BF16), vp, preferred_element_type=F32)


def _finish_bounded(lf_ref, accf_ref):
    tq = accf_ref.shape[1]
    low = lax.broadcasted_iota(jnp.int32, (tq, LANES), 1) < MLA_V
    outs = []
    for hp in range(MLA_HEADS // 2):
        la = jnp.sum(lf_ref[2 * hp], axis=-1, keepdims=True)
        lb = jnp.sum(lf_ref[2 * hp + 1], axis=-1, keepdims=True)
        outs.append(jnp.where(low, accf_ref[2 * hp] / la, accf_ref[2 * hp + 1] / lb))
    return jnp.concatenate(outs, axis=-1)


def _attend_online(q, k, v, m_ref, l_ref, acc_ref, mask):
    tq = q.shape[0]
    low = lax.broadcasted_iota(jnp.int32, (tq, LANES), 1) < MLA_V
    for hp in range(MLA_HEADS // 2):
        vp = v[:, LANES * hp:LANES * (hp + 1)]
        pvs, alphas = [], []
        for e in range(2):
            hd = 2 * hp + e
            sl = slice(HEAD_SLAB * hd, HEAD_SLAB * (hd + 1))
            s = _dot_nt(q[:, sl], k[:, sl])
            if mask is not None:
                s = jnp.where(mask, s, NEG_INF)
            m_prev = m_ref[hd]
            m_new = jnp.maximum(m_prev, jnp.max(s, axis=-1, keepdims=True))
            p = jnp.exp2(s - m_new)
            alpha = jnp.exp2(m_prev - m_new)
            l_ref[hd] = alpha * l_ref[hd] + jnp.sum(p, axis=-1, keepdims=True)
            m_ref[hd] = m_new
            pvs.append(jnp.dot(p.astype(BF16), vp, preferred_element_type=F32))
            alphas.append(alpha)
        sl2 = slice(LANES * hp, LANES * (hp + 1))
        acc_ref[:, sl2] = jnp.where(low, alphas[0], alphas[1]) * acc_ref[:, sl2] + jnp.where(low, pvs[0], pvs[1])


def _finish_online(l_ref, acc_ref):
    tq = acc_ref.shape[0]
    low = lax.broadcasted_iota(jnp.int32, (tq, LANES), 1) < MLA_V
    outs = []
    for hp in range(MLA_HEADS // 2):
        l2 = jnp.where(low, l_ref[2 * hp], l_ref[2 * hp + 1])
        outs.append(acc_ref[:, LANES * hp:LANES * (hp + 1)] / l2)
    return jnp.concatenate(outs, axis=-1)


def _softmax_scratch(tq):
    return [pltpu.VMEM((MLA_HEADS, tq, 1), F32), pltpu.VMEM((MLA_HEADS, tq, 1), F32), pltpu.VMEM((tq, MLA_WIDTH), F32),
            pltpu.VMEM((MLA_HEADS, tq, LANES), F32), pltpu.VMEM((MLA_HEADS, tq, LANES), F32)]


def _softmax_init(m_ref, l_ref, acc_ref, lf_ref, accf_ref):
    m_ref[...] = jnp.full(m_ref.shape, NEG_INF, F32)
    for ref in (l_ref, acc_ref, lf_ref, accf_ref):
        ref[...] = jnp.zeros(ref.shape, F32)


def _attend(bounded, cond, get_qkv, scratch, mask):
    m_ref, l_ref, acc_ref, lf_ref, accf_ref = scratch

    @pl.when(jnp.logical_and(cond, bounded))
    def _():
        _attend_bounded(*get_qkv(), lf_ref, accf_ref, mask)

    @pl.when(jnp.logical_and(cond, jnp.logical_not(bounded)))
    def _():
        _attend_online(*get_qkv(), m_ref, l_ref, acc_ref, mask)


def _softmax_finish(bounded, cond, o_ref, scratch):
    m_ref, l_ref, acc_ref, lf_ref, accf_ref = scratch

    @pl.when(jnp.logical_and(cond, bounded))
    def _():
        o_ref[0] = _finish_bounded(lf_ref, accf_ref)

    @pl.when(jnp.logical_and(cond, jnp.logical_not(bounded)))
    def _():
        o_ref[0] = _finish_online(l_ref, acc_ref)


def _attn_prompt_body(bound_ref, q_ref, k_ref, v_ref, o_ref, *scratch):
    i = pl.program_id(1)
    j = pl.program_id(2)
    tq = q_ref.shape[1]
    tk = k_ref.shape[1]
    bounded = bound_ref[0] <= MAX_UNSHIFTED_LOGIT
    get = lambda: (q_ref[0], k_ref[0], v_ref[0])

    @pl.when(j == 0)
    def _():
        _softmax_init(*scratch)

    _attend(bounded, j < i, get, scratch, None)
    rows = lax.broadcasted_iota(jnp.int32, (tq, tk), 0)
    cols = lax.broadcasted_iota(jnp.int32, (tq, tk), 1)
    _attend(bounded, j == i, get, scratch, (cols >> 6) <= (rows >> 6))
    _softmax_finish(bounded, j == i, o_ref, scratch)


def _attn_prompt_call(bound, q, k, v, tile):
    b, t, _ = q.shape
    n = t // tile
    kv_map = lambda bi, i, j: (bi, jnp.minimum(i, j), 0)
    return pl.pallas_call(
        _attn_prompt_body,
        grid=(b, n, n),
        in_specs=[pl.BlockSpec(memory_space=pltpu.SMEM),
                  pl.BlockSpec((1, tile, q.shape[2]), lambda bi, i, j: (bi, i, 0)),
                  pl.BlockSpec((1, tile, k.shape[2]), kv_map),
                  pl.BlockSpec((1, tile, v.shape[2]), kv_map)],
        out_specs=pl.BlockSpec((1, tile, MLA_WIDTH), lambda bi, i, j: (bi, i, 0)),
        out_shape=jax.ShapeDtypeStruct((b, t, MLA_WIDTH), F32),
        scratch_shapes=_softmax_scratch(tile),
        compiler_params=_params(("arbitrary", "arbitrary", "arbitrary")),
        name="attn_prompt",
    )(bound, q, k, v)


def _attn_sample_body(bound_ref, q_ref, kn_ref, vn_ref, ckv_ref, kr_ref, wuk_ref, wuv_ref, mk_ref, kgk_ref, tile_ref,
                      o_ref, *scratch):
    j = pl.program_id(1)
    bounded = bound_ref[0] <= MAX_UNSHIFTED_LOGIT
    always = j >= 0

    @pl.when(j == 0)
    def _():
        _softmax_init(*scratch)

    _attend(bounded, j == 0, lambda: (q_ref[0], kn_ref[0], vn_ref[0]), scratch, None)

    def past_qkv():
        k, v = _mla_kv(ckv_ref[0, 0].astype(BF16), kr_ref[0, 0].astype(BF16), wuk_ref, wuv_ref, mk_ref, kgk_ref,
                       tile_ref)
        return q_ref[0], k, v

    _attend(bounded, always, past_qkv, scratch, None)
    _softmax_finish(bounded, j == pl.num_programs(1) - 1, o_ref, scratch)


def _attn_sample_call(bound, q, k_new, v_new, cache_ckv, cache_kr, lw, layer, tkc):
    b, t, _ = q.shape
    past = cache_ckv.shape[2]
    consts = [lw["wuk"], lw["wuv"], lw["mk"], lw["kgk"], lw["tile"]]
    new = lambda w: pl.BlockSpec((1, t, w), lambda bi, j: (bi, 0, 0))
    return pl.pallas_call(
        _attn_sample_body,
        grid=(b, past // tkc),
        in_specs=[pl.BlockSpec(memory_space=pltpu.SMEM), new(q.shape[2]), new(k_new.shape[2]), new(v_new.shape[2]),
                  pl.BlockSpec((1, 1, tkc, MLA_KV_RANK), lambda bi, j: (layer, bi, j, 0)),
                  pl.BlockSpec((1, 1, tkc, MLA_ROPE), lambda bi, j: (layer, bi, j, 0))]
                 + [_const_spec(a.shape) for a in consts],
        out_specs=new(MLA_WIDTH),
        out_shape=jax.ShapeDtypeStruct((b, t, MLA_WIDTH), F32),
        scratch_shapes=_softmax_scratch(t),
        compiler_params=_params(("arbitrary", "arbitrary")),
        name="attn_sample",
    )(bound, q, k_new, v_new, cache_ckv, cache_kr, *consts)


def _recur_body(rec_ref, s0_ref, e2_ref, sel_ref, bd_ref, o_ref, s_o, s_ref, *, hk):
    t = pl.program_id(1)
    tt = rec_ref.shape[1]

    @pl.when(t == 0)
    def _():
        s_ref[...] = s0_ref[0]

    rec = rec_ref[0]
    q = rec[:, 0:hk]
    k = rec[:, hk:2 * hk]
    g = rec[:, 2 * hk:3 * hk]
    v = rec[:, 3 * hk:3 * hk + 256]

    rows = lax.broadcasted_iota(jnp.int32, (2 * tt, tt), 0)
    cols = lax.broadcasted_iota(jnp.int32, (2 * tt, tt), 1)
    top = rows < tt
    rr = jnp.where(top, rows, rows - tt)
    ones = ((rr >> 4) == (cols >> 4)) & (jnp.logical_not(top) | (cols <= rr))
    lmat = jnp.where(ones, 1.0, 0.0).astype(BF16)
    g1 = g.astype(BF16)
    r1 = g - g1.astype(F32)
    g2 = r1.astype(BF16)
    g3 = (r1 - g2.astype(F32)).astype(BF16)
    bb = jnp.dot(lmat, jnp.concatenate([g1, g2, g3], axis=-1), preferred_element_type=F32)
    bsum = (bb[:, 2 * hk:3 * hk] + bb[:, hk:2 * hk]) + bb[:, 0:hk]
    b = bsum[0:tt]
    btot = bsum[tt:2 * tt]

    qe = (q * jnp.exp(b)).astype(BF16)
    ke = (k * jnp.exp(btot - b)).astype(BF16)
    dec = jnp.exp(btot)
    vb = v.astype(BF16)

    for n in range(tt // SUB):
        r0 = SUB * n
        blk = slice(r0, r0 + SUB)
        s = s_ref[...]
        o_inter = jnp.dot(qe[blk], s.astype(BF16), preferred_element_type=F32)
        bj, kj, vj = b[blk], k[blk], v[blk]
        xs = []
        for i0 in range(SUB):
            bi = b[r0 + i0:r0 + i0 + 1]
            qi = q[r0 + i0:r0 + i0 + 1]
            xs.append((jnp.exp(jnp.minimum(bi - bj, 0.0)) * (kj * qi)).astype(BF16))
        x = jnp.concatenate(xs, axis=0)
        r = jnp.dot(x, e2_ref[...], preferred_element_type=F32)
        y = (r.reshape(SUB, SUB, 256) * vj[None]).reshape(SUB * SUB, 256).astype(BF16)
        o_diag = jnp.dot(sel_ref[...], y, preferred_element_type=F32)
        o_ref[0, blk, :] = o_inter + o_diag
        u = _dot_tn(ke[blk], vb[blk])
        dcol = dec[r0:r0 + 8].T[:, 0:1]
        s_ref[...] = dcol * s + u * bd_ref[...]

    @pl.when(t == pl.num_programs(1) - 1)
    def _():
        s_o[0] = s_ref[...]


def _recur_call(rec, s0, hk, tt, name):
    b, t, _ = rec.shape
    w = 3 * hk + 256
    heads = 4
    dk = hk // heads
    e2 = np.kron(np.eye(heads), np.ones((dk, 64))).astype(np.float32)
    sel = np.zeros((SUB, SUB * SUB), np.float32)
    for i in range(SUB):
        sel[i, SUB * i:SUB * i + i + 1] = 1.0
    return pl.pallas_call(
        functools.partial(_recur_body, hk=hk),
        grid=(b, t // tt),
        in_specs=[pl.BlockSpec((1, tt, w), lambda bi, ti: (bi, ti, 0)),
                  pl.BlockSpec((1, hk, 256), lambda bi, ti: (bi, 0, 0)),
                  _const_spec((hk, 256)), _const_spec((SUB, SUB * SUB)), _const_spec((hk, 256))],
        out_specs=[pl.BlockSpec((1, tt, 256), lambda bi, ti: (bi, ti, 0)),
                   pl.BlockSpec((1, hk, 256), lambda bi, ti: (bi, 0, 0))],
        out_shape=[jax.ShapeDtypeStruct((b, t, 256), F32), jax.ShapeDtypeStruct((b, hk, 256), F32)],
        scratch_shapes=[pltpu.VMEM((hk, 256), F32)],
        compiler_params=_params(("arbitrary", "arbitrary")),
        name=name,
    )(rec, s0, jnp.asarray(e2, BF16), jnp.asarray(sel, BF16), jnp.asarray(e2, F32))


def _outffn_body(x_ref, om_ref, oh_ref, og_ref, gate_ref, g1_ref, sh2_ref, sc2_ref, g2_ref,
                 mon_ref, hon_ref, gon_ref, nf_ref, m64_ref, wo_ref, wfi_ref, wfo_ref, y_ref, *, ff_chunk):
    bb, tt, d = x_ref.shape
    tm = bb * tt
    om = om_ref[...].reshape(tm, MLA_WIDTH)
    omn = _rms(om) * mon_ref[...]

    def headnorm(o, gain_ref, gate):
        ms = jnp.dot((o * o).astype(BF16), m64_ref[...], preferred_element_type=F32)
        return o * lax.rsqrt(ms + EPS) * gain_ref[...] * gate

    gate = gate_ref[...].reshape(tm, 512)
    ohn = headnorm(oh_ref[...].reshape(tm, 256), hon_ref, gate[:, 0:256])
    ogn = headnorm(og_ref[...].reshape(tm, 256), gon_ref, gate[:, 256:512])
    mix_in = jnp.concatenate([omn, ohn, ogn], axis=-1).astype(BF16)
    mix = jnp.dot(mix_in, wo_ref[...], preferred_element_type=F32)
    x1 = x_ref[...] + g1_ref[...] * mix.reshape(bb, tt, d)
    h2 = (_rms(x1) * nf_ref[...] * (1.0 + sc2_ref[...]) + sh2_ref[...]).reshape(tm, d).astype(BF16)
    acc = jnp.zeros((tm, d), F32)
    for c in range(D_FF // ff_chunk):
        a = jnp.dot(h2, wfi_ref[:, ff_chunk * c:ff_chunk * (c + 1)], preferred_element_type=F32)
        u = jnp.dot(h2, wfi_ref[:, D_FF + ff_chunk * c:D_FF + ff_chunk * (c + 1)], preferred_element_type=F32)
        act = (_silu(a) * u).astype(BF16)
        acc = acc + jnp.dot(act, wfo_ref[ff_chunk * c:ff_chunk * (c + 1), :], preferred_element_type=F32)
    y_ref[...] = x1 + g2_ref[...] * acc.reshape(bb, tt, d)


def _outffn_call(x, om, oh, og, gate, mods, lw, bb, tt):
    b, t, d = x.shape
    tok = lambda w: pl.BlockSpec((bb, tt, w), lambda i, j: (i, j, 0))
    modspec = pl.BlockSpec((bb, 1, d), lambda i, j: (i, 0, 0))
    consts = [lw["mon"], lw["hon"], lw["gon"], lw["nf"], lw["m64"], lw["wo"], lw["wfi"], lw["wfo"]]
    single = lambda a: pl.BlockSpec(a.shape, lambda *_: (0,) * a.ndim, pipeline_mode=pl.Buffered(1))
    return pl.pallas_call(
        functools.partial(_outffn_body, ff_chunk=256),
        grid=(b // bb, t // tt),
        in_specs=[tok(d), tok(MLA_WIDTH), tok(256), tok(256), tok(512)] + [modspec] * 4 + [single(a) for a in consts],
        out_specs=tok(d),
        out_shape=jax.ShapeDtypeStruct((b, t, d), F32),
        compiler_params=_params(("arbitrary", "arbitrary")),
        name="outffn",
    )(x, om, oh, og, gate, *mods, *consts)


def _layer_weights(l, w_in, norm_mix, norm_ffn, mla_cq_norm, mla_ckv_norm, mla_w_uq, mla_w_uk, mla_w_uv, mla_q_norm,
                   mla_k_norm, mla_out_norm, hg_lb_logits, hg_out_norm, gla_w_a2, gla_b_a, gla_out_norm, w_o,
                   w_ffn_in, w_ffn_out):
    wi = w_in[l]
    d = wi.shape[0]
    o = dict(cq=0, ckv=256, kr=512, hq=544, hf=800, hi=1056, hg=1312, gq=1568, gk=1696, gv=1824, gg=2080, ga=2336)
    cols = [wi[:, o["cq"]:o["cq"] + 256], wi[:, o["ckv"]:o["ckv"] + 256], wi[:, o["hq"]:o["hq"] + 256],
            wi[:, o["hf"]:o["hf"] + 256], wi[:, o["hi"]:o["hi"] + 256], wi[:, o["hg"]:o["hg"] + 256],
            wi[:, o["gq"]:o["gq"] + 128], wi[:, o["gk"]:o["gk"] + 128], wi[:, o["gv"]:o["gv"] + 256],
            wi[:, o["gg"]:o["gg"] + 256], wi[:, o["kr"]:o["kr"] + 32], wi[:, o["ga"]:o["ga"] + 16],
            jnp.zeros((d, LANES - MLA_ROPE - GLA_RANK), wi.dtype)]
    win = jnp.concatenate(cols, axis=1).astype(BF16)
    assert win.shape[1] == ZCOLS

    uq = mla_w_uq[l]
    r = uq.shape[0]
    half = MLA_ROPE // 2
    zpad = jnp.zeros((r, MLA_HEADS, HEAD_SLAB - MLA_NOPE - MLA_ROPE), uq.dtype)
    uq_a = jnp.concatenate([uq, zpad], axis=-1).reshape(r, -1)
    uq_b = jnp.concatenate([jnp.zeros((r, MLA_HEADS, MLA_NOPE), uq.dtype), uq[..., MLA_NOPE + half:],
                            uq[..., MLA_NOPE:MLA_NOPE + half], zpad], axis=-1).reshape(r, -1)
    wuq = jnp.concatenate([uq_a, uq_b], axis=1).astype(BF16)
    uk = mla_w_uk[l]
    wuk = jnp.concatenate([uk, jnp.zeros((r, MLA_HEADS, HEAD_SLAB - MLA_NOPE), uk.dtype)], axis=-1)
    wuk = wuk.reshape(r, -1).astype(BF16)
    wuv = mla_w_uv[l].reshape(r, -1).astype(BF16)

    qn, kn = mla_q_norm[l], mla_k_norm[l]
    z32 = jnp.zeros((HEAD_SLAB - MLA_NOPE - MLA_ROPE,), F32)
    qga = jnp.concatenate([qn, z32])[None]
    qgb = jnp.concatenate([jnp.zeros((MLA_NOPE,), F32), qn[MLA_NOPE + half:], qn[MLA_NOPE:MLA_NOPE + half], z32])[None]
    kgk = jnp.concatenate([kn[:MLA_NOPE], jnp.zeros((HEAD_SLAB - MLA_NOPE,), F32)])[None]
    gkr = kn[MLA_NOPE:][None]

    blk = np.zeros((HEAD_SLAB, HEAD_SLAB), np.float32)
    blk[:MLA_NOPE, :MLA_NOPE] = 1.0 / MLA_NOPE
    mk = np.kron(np.eye(2), blk)
    blk_q = blk.copy()
    blk_q[MLA_NOPE:MLA_NOPE + MLA_ROPE, MLA_NOPE:MLA_NOPE + MLA_ROPE] = 1.0 / MLA_ROPE
    mq = np.kron(np.eye(2), blk_q)
    m64 = np.kron(np.eye(4), np.full((64, 64), 1.0 / 64, np.float32))
    tile = np.zeros((MLA_ROPE, MLA_HEADS * HEAD_SLAB), np.float32)
    for hd in range(MLA_HEADS):
        tile[np.arange(MLA_ROPE), HEAD_SLAB * hd + MLA_NOPE + np.arange(MLA_ROPE)] = 1.0

    def norm_bound(g):
        return jnp.sqrt(MLA_NOPE * jnp.max(jnp.abs(g[:MLA_NOPE])) ** 2 + MLA_ROPE * jnp.max(jnp.abs(g[MLA_NOPE:])) ** 2)
    logit_bound = (MLA_SCALE * norm_bound(qn) * norm_bound(kn)).reshape(1).astype(F32)

    return dict(
        logit_bound=logit_bound, nm=norm_mix[l][None], win=win, cqn=mla_cq_norm[l][None], ckvn=mla_ckv_norm[l][None], wuq=wuq, wuk=wuk,
        wuv=wuv, qga=qga, qgb=qgb, mq=jnp.asarray(mq, BF16), mk=jnp.asarray(mk, BF16), kgk=kgk, gkr=gkr,
        tile=jnp.asarray(tile, BF16), lbl=hg_lb_logits, wa2=gla_w_a2[l].astype(BF16), ba=gla_b_a[l][None],
        mon=mla_out_norm[l][None], hon=jnp.tile(hg_out_norm[l], HG_HEADS)[None],
        gon=jnp.tile(gla_out_norm[l], GLA_HEADS)[None], nf=norm_ffn[l][None], m64=jnp.asarray(m64, BF16),
        wo=w_o[l].astype(BF16), wfi=w_ffn_in[l].astype(BF16), wfo=w_ffn_out[l].astype(BF16))


def _rope_tables(pos, reps):
    inv = ROPE_THETA ** (-(jnp.arange(0, MLA_ROPE, 2, dtype=F32) / MLA_ROPE))
    ang = pos.astype(F32)[:, None] * inv[None, :]
    cos, sin = jnp.cos(ang), jnp.sin(ang)
    n = pos.shape[0]
    pad = HEAD_SLAB - MLA_NOPE - MLA_ROPE
    cosq = jnp.concatenate([jnp.ones((n, MLA_NOPE), F32), cos, cos, jnp.zeros((n, pad), F32)], axis=1)
    sinq = jnp.concatenate([jnp.zeros((n, MLA_NOPE), F32), -sin, sin, jnp.zeros((n, pad), F32)], axis=1)
    cosk = jnp.concatenate([cos, cos], axis=1)
    sink = jnp.concatenate([-sin, sin], axis=1)
    tabs = dict(cosq=cosq, sinq=sinq, cosk=cosk, sink=sink)
    return {k: jnp.tile(v, (reps, 1)) for k, v in tabs.items()}


def _block_diag_state(state):
    b, hh, dk, dv = state.shape
    s = jnp.zeros((b, hh * dk, hh * dv), state.dtype)
    for hd in range(hh):
        s = s.at[:, hd * dk:(hd + 1) * dk, hd * dv:(hd + 1) * dv].set(state[:, hd])
    return s


def _head_states(s_bd, hh):
    dk, dv = s_bd.shape[1] // hh, s_bd.shape[2] // hh
    return jnp.stack([s_bd[:, hd * dk:(hd + 1) * dk, hd * dv:(hd + 1) * dv] for hd in range(hh)], axis=1)


def _group_layer(x, mods, lw, tabs, layer, bb, tt, rtt, s_hg, s_gla, attn_fn):
    sh1, sc1, g1, sh2, sc2, g2 = mods
    ckv, kr, q, k, v, hrec, grec, gate = _inproj_call(x, sh1, sc1, lw, tabs, bb, tt, layer)
    om = attn_fn(q, k, v)
    oh, shg = _recur_call(hrec, s_hg, HG_HEADS * HG_DK, rtt, "recur_hgrn")
    og, sgl = _recur_call(grec, s_gla, GLA_HEADS * GLA_DK, rtt, "recur_gla")
    y = _outffn_call(x, om, oh, og, gate, (g1, sh2, sc2, g2), lw, bb, tt)
    return y, ckv, kr, _head_states(shg, HG_HEADS), _head_states(sgl, GLA_HEADS)


def kernel(x_prompt, x_sample, c_prompt, c_sample, cache_mla_ckv, cache_mla_krope, state_hgrn, state_gla, w_mod, b_mod, norm_mix, norm_ffn, w_in, mla_cq_norm, mla_ckv_norm, mla_w_uq, mla_w_uk, mla_w_uv, mla_q_norm, mla_k_norm, mla_out_norm, hg_lb_logits, hg_out_norm, gla_w_a2, gla_b_a, gla_out_norm, w_o, w_ffn_in, w_ffn_out):
    bp, t_p, d = x_prompt.shape
    bs, t_s, _ = x_sample.shape
    depth = w_mod.shape[0]
    tm = 512
    tt_p = min(tm, t_p)
    bb_s = max(1, min(bs, tm // t_s))
    rtt_p = min(128, t_p)
    rtt_s = min(128, t_s)
    attn_tile = min(512, t_p)
    tkc = min(1024, cache_mla_ckv.shape[2])

    tabs_p = _rope_tables(jnp.arange(t_p, dtype=jnp.int32), 1)
    tabs_s = _rope_tables(PAST_LEN + jnp.arange(t_s, dtype=jnp.int32), bb_s)

    n_c = bp + bs
    c_rows = -(-n_c // 16) * 16
    c_pad = jnp.concatenate([c_prompt, c_sample, jnp.zeros((c_rows - n_c, d), c_prompt.dtype)], axis=0)

    xp, xs = x_prompt, x_sample
    outs = [[] for _ in range(8)]
    for l in range(depth):
        lw = _layer_weights(l, w_in, norm_mix, norm_ffn, mla_cq_norm, mla_ckv_norm, mla_w_uq, mla_w_uk, mla_w_uv,
                            mla_q_norm, mla_k_norm, mla_out_norm, hg_lb_logits, hg_out_norm, gla_w_a2, gla_b_a,
                            gla_out_norm, w_o, w_ffn_in, w_ffn_out)
        mod = _mod_call(c_pad, w_mod[l], b_mod[l][None])
        mods_p = [mod[0:bp, i * d:(i + 1) * d][:, None, :] for i in range(6)]
        mods_s = [mod[bp:n_c, i * d:(i + 1) * d][:, None, :] for i in range(6)]

        zero_hg = jnp.zeros((bp, HG_HEADS * HG_DK, HG_HEADS * HG_DV), F32)
        zero_gla = jnp.zeros((bp, GLA_HEADS * GLA_DK, GLA_HEADS * GLA_DV), F32)
        xp, a, b_, c_, d_ = _group_layer(xp, mods_p, lw, tabs_p, l, 1, tt_p, rtt_p, zero_hg, zero_gla,
                                         lambda q, k, v: _attn_prompt_call(lw["logit_bound"], q, k, v, attn_tile))
        for lst, val in zip(outs[0:4], (a, b_, c_, d_)):
            lst.append(val)
        xs, a, b_, c_, d_ = _group_layer(xs, mods_s, lw, tabs_s, l, bb_s, t_s, rtt_s,
                                         _block_diag_state(state_hgrn[l]), _block_diag_state(state_gla[l]),
                                         lambda q, k, v: _attn_sample_call(lw["logit_bound"], q, k, v, cache_mla_ckv,
                                                                           cache_mla_krope, lw, l, tkc))
        for lst, val in zip(outs[4:8], (a, b_, c_, d_)):
            lst.append(val)

    return (xp, xs) + tuple(jnp.stack(o) for o in outs)
```

```python
import functools

import numpy as np
import jax
import jax.numpy as jnp
from jax import lax
from jax.experimental import pallas as pl
from jax.experimental.pallas import tpu as pltpu

F32, BF16 = jnp.float32, jnp.bfloat16

D_MODEL = 1024
PAST_LEN = 4096
CHUNK = 64
EPS = 1e-6
NEG_INF = -1e30
ROPE_THETA = 10000.0
MLA_HEADS = 8
MLA_NOPE = 64
MLA_ROPE = 32
MLA_V = 64
MLA_Q_RANK = 256
MLA_KV_RANK = 256
MLA_WIDTH = MLA_HEADS * MLA_V
MLA_SCALE = (MLA_NOPE + MLA_ROPE) ** -0.5
LOG2E = 1.4426950408889634
HG_HEADS = 4
HG_DK = 64
HG_DV = 64
GLA_HEADS = 4
GLA_DK = 32
GLA_DV = 64
GLA_RANK = 16
GLA_TAU = 16.0
D_FF = 2816

LANES = 128
HEAD_SLAB = 128
SUB = 16
ZCOLS = 2432
VMEM_LIMIT = 56 * 1024 * 1024

_Z = dict(cq=0, ckv=256, hq=512, hf=768, hi=1024, hg=1280, gq=1536, gk=1664, gv=1792, gg=2048, misc=2304)


def _bdot(a, b):
    return jnp.dot(a.astype(BF16), b.astype(BF16), preferred_element_type=F32)


def _dot_nt(a, b):
    return lax.dot_general(a, b, (((1,), (1,)), ((), ())), preferred_element_type=F32)


def _dot_tn(a, b):
    return lax.dot_general(a, b, (((0,), (0,)), ((), ())), preferred_element_type=F32)


def _sigmoid(x):
    return 1.0 / (1.0 + jnp.exp(-x))


def _silu(x):
    return x * _sigmoid(x)


def _log_sigmoid(x):
    return jnp.minimum(x, 0.0) - jnp.log1p(jnp.exp(-jnp.abs(x)))


def _rms(x):
    return x * lax.rsqrt(jnp.mean(x * x, axis=-1, keepdims=True) + EPS)


def _params(sem):
    return pltpu.CompilerParams(dimension_semantics=sem, vmem_limit_bytes=VMEM_LIMIT)


def _const_spec(shape):
    nd = len(shape)
    return pl.BlockSpec(shape, lambda *_: (0,) * nd)


def _mod_body(c_ref, w_ref, b_ref, o_ref):
    o_ref[...] = _bdot(_silu(c_ref[...]), w_ref[...]) + b_ref[...]


def _mod_call(c_pad, w_mod, b_mod):
    rows, d = c_pad.shape
    n = w_mod.shape[1]
    tn = 1024
    return pl.pallas_call(
        _mod_body,
        grid=(n // tn,),
        in_specs=[pl.BlockSpec((rows, d), lambda j: (0, 0)),
                  pl.BlockSpec((d, tn), lambda j: (0, j)),
                  pl.BlockSpec((1, tn), lambda j: (0, j))],
        out_specs=pl.BlockSpec((rows, tn), lambda j: (0, j)),
        out_shape=jax.ShapeDtypeStruct((rows, n), F32),
        compiler_params=_params(("arbitrary",)),
        name="mod",
    )(c_pad, w_mod, b_mod)


def _mla_kv(ckvn_bf, kro_bf, w_uk_ref, w_uv_ref, mk_ref, kgk_ref, tile_ref):
    krt = jnp.dot(kro_bf, tile_ref[...], preferred_element_type=F32)
    kgk2 = jnp.concatenate([kgk_ref[...], kgk_ref[...]], axis=-1)
    parts = []
    for hp in range(MLA_HEADS // 2):
        sl = slice(2 * HEAD_SLAB * hp, 2 * HEAD_SLAB * (hp + 1))
        sk = jnp.dot(ckvn_bf, w_uk_ref[:, sl], preferred_element_type=F32)
        msk = jnp.dot((sk * sk).astype(BF16), mk_ref[...], preferred_element_type=F32)
        parts.append((sk * lax.rsqrt(msk + EPS) * kgk2 + krt).astype(BF16))
    k = jnp.concatenate(parts, axis=-1)
    v = jnp.dot(ckvn_bf, w_uv_ref[...], preferred_element_type=F32).astype(BF16)
    return k, v


def _inproj_body(x_ref, sh_ref, sc_ref, nm_ref, win_ref, cqn_ref, ckvn_ref, wuq_ref, wuk_ref, wuv_ref,
                 qga_ref, qgb_ref, cosq_ref, sinq_ref, mq_ref, mk_ref, kgk_ref, gkr_ref, cosk_ref, sink_ref,
                 tile_ref, lbl_ref, wa2_ref, ba_ref,
                 ckv_o, kr_o, q_o, k_o, v_o, hrec_o, grec_o, gate_o, *, layer):
    bb, tt, d = x_ref.shape
    tm = bb * tt
    x = x_ref[...]
    h = _rms(x) * nm_ref[...] * (1.0 + sc_ref[...]) + sh_ref[...]
    z = jnp.dot(h.reshape(tm, d).astype(BF16), win_ref[...], preferred_element_type=F32)

    def seg(name, w):
        return z[:, _Z[name]:_Z[name] + w]

    cqn = _rms(seg("cq", MLA_Q_RANK)) * cqn_ref[...]
    ckvn = _rms(seg("ckv", MLA_KV_RANK)) * ckvn_ref[...]
    ckv_o[...] = ckvn.reshape(bb, tt, MLA_KV_RANK)
    misc = seg("misc", LANES)

    qab = jnp.dot(cqn.astype(BF16), wuq_ref[...], preferred_element_type=F32)
    width = MLA_HEADS * HEAD_SLAB
    ca = cosq_ref[...] * qga_ref[...]
    sb = sinq_ref[...] * qgb_ref[...]
    ca2 = jnp.concatenate([ca, ca], axis=-1)
    sb2 = jnp.concatenate([sb, sb], axis=-1)
    for hp in range(MLA_HEADS // 2):
        sl = slice(2 * HEAD_SLAB * hp, 2 * HEAD_SLAB * (hp + 1))
        qa = qab[:, sl]
        qb = qab[:, width + sl.start:width + sl.stop]
        msq = jnp.dot((qa * qa).astype(BF16), mq_ref[...], preferred_element_type=F32)
        qo = lax.rsqrt(msq + EPS) * (qa * ca2 + qb * sb2) * (MLA_SCALE * LOG2E)
        q_o[:, :, sl] = qo.astype(BF16).reshape(bb, tt, 2 * HEAD_SLAB)

    kr = misc[:, 0:MLA_ROPE]
    krn = _rms(kr) * gkr_ref[...]
    half = MLA_ROPE // 2
    swapped = jnp.concatenate([krn[:, half:], krn[:, :half]], axis=-1)
    kro = krn * cosk_ref[...] + swapped * sink_ref[...]
    kr_o[...] = kro.reshape(bb, tt, MLA_ROPE)
    k, v = _mla_kv(ckvn.astype(BF16), kro.astype(BF16), wuk_ref, wuv_ref, mk_ref, kgk_ref, tile_ref)
    k_o[...] = k.reshape(bb, tt, width)
    v_o[...] = v.reshape(bb, tt, MLA_WIDTH)

    lg = lbl_ref[...]
    e = jnp.exp(lg - jnp.max(lg, axis=0, keepdims=True))
    p = e / jnp.sum(e, axis=0, keepdims=True)
    lb = jnp.sum(p[0:layer + 1], axis=0, keepdims=True) - p[0:1]
    f = lb + (1.0 - lb) * _sigmoid(seg("hf", 256))
    hrec_o[:, :, 0:256] = seg("hq", 256).reshape(bb, tt, 256)
    hrec_o[:, :, 256:512] = (1.0 - f).reshape(bb, tt, 256)
    hrec_o[:, :, 512:768] = jnp.log(f).reshape(bb, tt, 256)
    hrec_o[:, :, 768:1024] = seg("hi", 256).reshape(bb, tt, 256)

    ga = misc[:, MLA_ROPE:MLA_ROPE + GLA_RANK]
    apre = jnp.dot(ga.astype(BF16), wa2_ref[...], preferred_element_type=F32) + ba_ref[...]
    grec_o[:, :, 0:128] = (seg("gq", 128) * (GLA_DK ** -0.5)).reshape(bb, tt, 128)
    grec_o[:, :, 128:256] = seg("gk", 128).reshape(bb, tt, 128)
    grec_o[:, :, 256:384] = (_log_sigmoid(apre) / GLA_TAU).reshape(bb, tt, 128)
    grec_o[:, :, 384:640] = seg("gv", 256).reshape(bb, tt, 256)

    gate_o[:, :, 0:256] = _silu(seg("hg", 256)).reshape(bb, tt, 256)
    gate_o[:, :, 256:512] = _silu(seg("gg", 256)).reshape(bb, tt, 256)


def _inproj_call(x, sh, sc, lw, tabs, bb, tt, layer):
    b, t, d = x.shape
    tm = bb * tt
    grid = (b // bb, t // tt)
    tok = lambda w: pl.BlockSpec((bb, tt, w), lambda i, j: (i, j, 0))
    modspec = pl.BlockSpec((bb, 1, d), lambda i, j: (i, 0, 0))
    tab = lambda w: pl.BlockSpec((tm, w), lambda i, j: (j, 0))
    consts = [lw["nm"], lw["win"], lw["cqn"], lw["ckvn"], lw["wuq"], lw["wuk"], lw["wuv"], lw["qga"], lw["qgb"]]
    consts2 = [lw["mq"], lw["mk"], lw["kgk"], lw["gkr"]]
    consts3 = [lw["tile"], lw["lbl"], lw["wa2"], lw["ba"]]
    in_specs = ([tok(d), modspec, modspec] + [_const_spec(a.shape) for a in consts]
                + [tab(LANES), tab(LANES)] + [_const_spec(a.shape) for a in consts2]
                + [tab(MLA_ROPE), tab(MLA_ROPE)] + [_const_spec(a.shape) for a in consts3])
    widths = [(MLA_KV_RANK, F32), (MLA_ROPE, F32), (MLA_HEADS * HEAD_SLAB, BF16), (MLA_HEADS * HEAD_SLAB, BF16),
              (MLA_WIDTH, BF16), (1024, F32), (640, F32), (512, F32)]
    return pl.pallas_call(
        functools.partial(_inproj_body, layer=layer),
        grid=grid,
        in_specs=in_specs,
        out_specs=[tok(w) for w, _ in widths],
        out_shape=[jax.ShapeDtypeStruct((b, t, w), dt) for w, dt in widths],
        compiler_params=_params(("arbitrary", "arbitrary")),
        name="inproj",
    )(x, sh, sc, *consts, tabs["cosq"], tabs["sinq"], *consts2, tabs["cosk"], tabs["sink"], *consts3)


MAX_UNSHIFTED_LOGIT = 60.0
SMALL_SCORE_TILE = 64 * 1024


def _attend_bounded(q, k, v, lf_ref, accf_ref, bias):
    tq, tk = q.shape[0], k.shape[0]

    def scores(hd):
        sl = slice(HEAD_SLAB * hd, HEAD_SLAB * (hd + 1))
        s = _dot_nt(q[:, sl], k[:, sl])
        return s if bias is None else s + bias

    def accumulate(hd, s):
        p = jnp.exp2(s)
        if tk % LANES == 0:
            part = p[:, 0:LANES]
            for c in range(1, tk // LANES):
                part = part + p[:, LANES * c:LANES * (c + 1)]
        else:
            lane0 = lax.broadcasted_iota(jnp.int32, (tq, LANES), 1) == 0
            part = jnp.where(lane0, jnp.sum(p, axis=-1, keepdims=True), 0.0)
        lf_ref[hd] += part
        hp = hd // 2
        accf_ref[hd] += jnp.dot(p.astype(BF16), v[:, LANES * hp:LANES * (hp + 1)], preferred_element_type=F32)

    if tq * tk <= SMALL_SCORE_TILE:
        ss = [scores(hd) for hd in range(MLA_HEADS)]
        for hd in range(MLA_HEADS):
            accumulate(hd, ss[hd])
    else:
        for hd in range(MLA_HEADS):
            accumulate(hd, scores(hd))


def _finish_bounded(lf_ref, accf_ref):
    tq = accf_ref.shape[1]
    low = lax.broadcasted_iota(jnp.int32, (tq, LANES), 1) < MLA_V
    outs = []
    for hp in range(MLA_HEADS // 2):
        la = jnp.sum(lf_ref[2 * hp], axis=-1, keepdims=True)
        lb = jnp.sum(lf_ref[2 * hp + 1], axis=-1, keepdims=True)
        outs.append(jnp.where(low, accf_ref[2 * hp] / la, accf_ref[2 * hp + 1] / lb))
    return jnp.concatenate(outs, axis=-1)


def _attend_online(q, k, v, m_ref, l_ref, acc_ref, bias):
    tq = q.shape[0]
    low = lax.broadcasted_iota(jnp.int32, (tq, LANES), 1) < MLA_V
    for hp in range(MLA_HEADS // 2):
        vp = v[:, LANES * hp:LANES * (hp + 1)]
        pvs, alphas = [], []
        for e in range(2):
            hd = 2 * hp + e
            sl = slice(HEAD_SLAB * hd, HEAD_SLAB * (hd + 1))
            s = _dot_nt(q[:, sl], k[:, sl])
            if bias is not None:
                s = s + bias
            m_prev = m_ref[hd]
            m_new = jnp.maximum(m_prev, jnp.max(s, axis=-1, keepdims=True))
            p = jnp.exp2(s - m_new)
            alpha = jnp.exp2(m_prev - m_new)
            l_ref[hd] = alpha * l_ref[hd] + jnp.sum(p, axis=-1, keepdims=True)
            m_ref[hd] = m_new
            pvs.append(jnp.dot(p.astype(BF16), vp, preferred_element_type=F32))
            alphas.append(alpha)
        sl2 = slice(LANES * hp, LANES * (hp + 1))
        acc_ref[:, sl2] = jnp.where(low, alphas[0], alphas[1]) * acc_ref[:, sl2] + jnp.where(low, pvs[0], pvs[1])


def _finish_online(l_ref, acc_ref):
    tq = acc_ref.shape[0]
    low = lax.broadcasted_iota(jnp.int32, (tq, LANES), 1) < MLA_V
    outs = []
    for hp in range(MLA_HEADS // 2):
        l2 = jnp.where(low, l_ref[2 * hp], l_ref[2 * hp + 1])
        outs.append(acc_ref[:, LANES * hp:LANES * (hp + 1)] / l2)
    return jnp.concatenate(outs, axis=-1)


def _softmax_scratch(tq):
    return [pltpu.VMEM((MLA_HEADS, tq, 1), F32), pltpu.VMEM((MLA_HEADS, tq, 1), F32), pltpu.VMEM((tq, MLA_WIDTH), F32),
            pltpu.VMEM((MLA_HEADS, tq, LANES), F32), pltpu.VMEM((MLA_HEADS, tq, LANES), F32)]


def _softmax_init(m_ref, l_ref, acc_ref, lf_ref, accf_ref):
    m_ref[...] = jnp.full(m_ref.shape, NEG_INF, F32)
    for ref in (l_ref, acc_ref, lf_ref, accf_ref):
        ref[...] = jnp.zeros(ref.shape, F32)


def _attend(bounded, cond, get_qkv, scratch, bias_ref):
    m_ref, l_ref, acc_ref, lf_ref, accf_ref = scratch

    @pl.when(jnp.logical_and(cond, bounded))
    def _():
        _attend_bounded(*get_qkv(), lf_ref, accf_ref, None if bias_ref is None else bias_ref[...])

    @pl.when(jnp.logical_and(cond, jnp.logical_not(bounded)))
    def _():
        _attend_online(*get_qkv(), m_ref, l_ref, acc_ref, None if bias_ref is None else bias_ref[...])


def _softmax_finish(bounded, cond, o_ref, scratch):
    m_ref, l_ref, acc_ref, lf_ref, accf_ref = scratch

    @pl.when(jnp.logical_and(cond, bounded))
    def _():
        o_ref[0] = _finish_bounded(lf_ref, accf_ref)

    @pl.when(jnp.logical_and(cond, jnp.logical_not(bounded)))
    def _():
        o_ref[0] = _finish_online(l_ref, acc_ref)


def _attn_prompt_body(qi_ref, ki_ref, bound_ref, bias_ref, q_ref, k_ref, v_ref, o_ref, *scratch):
    step = pl.program_id(1)
    i = qi_ref[step]
    j = ki_ref[step]
    bounded = bound_ref[0] <= MAX_UNSHIFTED_LOGIT
    get = lambda: (q_ref[0], k_ref[0], v_ref[0])

    @pl.when(j == 0)
    def _():
        _softmax_init(*scratch)

    _attend(bounded, j < i, get, scratch, None)
    _attend(bounded, j == i, get, scratch, bias_ref)
    _softmax_finish(bounded, j == i, o_ref, scratch)


def _attn_prompt_call(bound, q, k, v, tile):
    b, t, _ = q.shape
    n = t // tile
    pairs = [(i, j) for i in range(n) for j in range(i + 1)]
    qi = jnp.asarray([p[0] for p in pairs], jnp.int32)
    ki = jnp.asarray([p[1] for p in pairs], jnp.int32)
    q_map = lambda bi, st, qi_ref, ki_ref: (bi, qi_ref[st], 0)
    kv_map = lambda bi, st, qi_ref, ki_ref: (bi, ki_ref[st], 0)
    chunk_of = np.arange(tile) // CHUNK
    bias = jnp.asarray(np.where(chunk_of[None, :] <= chunk_of[:, None], 0.0, NEG_INF), F32)
    grid_spec = pltpu.PrefetchScalarGridSpec(
        num_scalar_prefetch=2,
        grid=(b, len(pairs)),
        in_specs=[pl.BlockSpec(memory_space=pltpu.SMEM), pl.BlockSpec((tile, tile), lambda *_: (0, 0)),
                  pl.BlockSpec((1, tile, q.shape[2]), q_map),
                  pl.BlockSpec((1, tile, k.shape[2]), kv_map),
                  pl.BlockSpec((1, tile, v.shape[2]), kv_map)],
        out_specs=pl.BlockSpec((1, tile, MLA_WIDTH), q_map),
        scratch_shapes=_softmax_scratch(tile))
    return pl.pallas_call(
        _attn_prompt_body,
        grid_spec=grid_spec,
        out_shape=jax.ShapeDtypeStruct((b, t, MLA_WIDTH), F32),
        compiler_params=_params(("arbitrary", "arbitrary")),
        name="attn_prompt",
    )(qi, ki, bound, bias, q, k, v)


def _attn_sample_body(bound_ref, q_ref, kn_ref, vn_ref, ckv_ref, kr_ref, wuk_ref, wuv_ref, mk_ref, kgk_ref, tile_ref,
                      o_ref, *scratch):
    j = pl.program_id(1)
    bounded = bound_ref[0] <= MAX_UNSHIFTED_LOGIT
    always = j >= 0

    @pl.when(j == 0)
    def _():
        _softmax_init(*scratch)

    _attend(bounded, j == 0, lambda: (q_ref[0], kn_ref[0], vn_ref[0]), scratch, None)

    def past_qkv():
        k, v = _mla_kv(ckv_ref[0, 0].astype(BF16), kr_ref[0, 0].astype(BF16), wuk_ref, wuv_ref, mk_ref, kgk_ref,
                       tile_ref)
        return q_ref[0], k, v

    _attend(bounded, always, past_qkv, scratch, None)
    _softmax_finish(bounded, j == pl.num_programs(1) - 1, o_ref, scratch)


def _attn_sample_call(bound, q, k_new, v_new, cache_ckv, cache_kr, lw, layer, tkc):
    b, t, _ = q.shape
    past = cache_ckv.shape[2]
    consts = [lw["wuk"], lw["wuv"], lw["mk"], lw["kgk"], lw["tile"]]
    new = lambda w: pl.BlockSpec((1, t, w), lambda bi, j: (bi, 0, 0))
    return pl.pallas_call(
        _attn_sample_body,
        grid=(b, past // tkc),
        in_specs=[pl.BlockSpec(memory_space=pltpu.SMEM), new(q.shape[2]), new(k_new.shape[2]), new(v_new.shape[2]),
                  pl.BlockSpec((1, 1, tkc, MLA_KV_RANK), lambda bi, j: (layer, bi, j, 0)),
                  pl.BlockSpec((1, 1, tkc, MLA_ROPE), lambda bi, j: (layer, bi, j, 0))]
                 + [_const_spec(a.shape) for a in consts],
        out_specs=new(MLA_WIDTH),
        out_shape=jax.ShapeDtypeStruct((b, t, MLA_WIDTH), F32),
        scratch_shapes=_softmax_scratch(t),
        compiler_params=_params(("arbitrary", "arbitrary")),
        name="attn_sample",
    )(bound, q, k_new, v_new, cache_ckv, cache_kr, *consts)


def _recur_body(rec_ref, s0_ref, e2_ref, sel_ref, bd_ref, o_ref, s_o, s_ref, *, hk):
    t = pl.program_id(1)
    tt = rec_ref.shape[1]

    @pl.when(t == 0)
    def _():
        s_ref[...] = s0_ref[0]

    rec = rec_ref[0]
    q = rec[:, 0:hk]
    k = rec[:, hk:2 * hk]
    g = rec[:, 2 * hk:3 * hk]
    v = rec[:, 3 * hk:3 * hk + 256]

    rows = lax.broadcasted_iota(jnp.int32, (2 * tt, tt), 0)
    cols = lax.broadcasted_iota(jnp.int32, (2 * tt, tt), 1)
    top = rows < tt
    rr = jnp.where(top, rows, rows - tt)
    ones = ((rr >> 4) == (cols >> 4)) & (jnp.logical_not(top) | (cols <= rr))
    lmat = jnp.where(ones, 1.0, 0.0).astype(BF16)
    g1 = g.astype(BF16)
    r1 = g - g1.astype(F32)
    g2 = r1.astype(BF16)
    g3 = (r1 - g2.astype(F32)).astype(BF16)
    bb = jnp.dot(lmat, jnp.concatenate([g1, g2, g3], axis=-1), preferred_element_type=F32)
    bsum = (bb[:, 2 * hk:3 * hk] + bb[:, hk:2 * hk]) + bb[:, 0:hk]
    b = bsum[0:tt]
    btot = bsum[tt:2 * tt]

    qe = (q * jnp.exp(b)).astype(BF16)
    ke = (k * jnp.exp(btot - b)).astype(BF16)
    dec = jnp.exp(btot)
    vb = v.astype(BF16)

    nb = tt // SUB
    xs = []
    b2 = b * LOG2E
    for n in range(nb):
        r0 = SUB * n
        bj, kj = b2[r0:r0 + SUB], k[r0:r0 + SUB]
        for i0 in range(SUB):
            bi = b2[r0 + i0:r0 + i0 + 1]
            qi = q[r0 + i0:r0 + i0 + 1]
            xs.append((jnp.exp2(jnp.minimum(bi - bj, 0.0)) * (kj * qi)).astype(BF16))
    x = jnp.concatenate(xs, axis=0)
    r = jnp.dot(x, e2_ref[...], preferred_element_type=F32)
    y = (r.reshape(nb, SUB, SUB, 256) * v.reshape(nb, 1, SUB, 256)).reshape(nb * SUB * SUB, 256).astype(BF16)

    dsel = jnp.concatenate([dec[SUB * n:SUB * n + 1] for n in range(nb)], axis=0).T
    us, o_diags = [], []
    for n in range(nb):
        blk = slice(SUB * n, SUB * (n + 1))
        us.append(_dot_tn(ke[blk], vb[blk]) * bd_ref[...])
        o_diags.append(jnp.dot(sel_ref[...], y[SUB * SUB * n:SUB * SUB * (n + 1)], preferred_element_type=F32))
    s = s_ref[...]
    for n in range(nb):
        blk = slice(SUB * n, SUB * (n + 1))
        o_ref[0, blk, :] = jnp.dot(qe[blk], s.astype(BF16), preferred_element_type=F32) + o_diags[n]
        s = dsel[:, n:n + 1] * s + us[n]
    s_ref[...] = s

    @pl.when(t == pl.num_programs(1) - 1)
    def _():
        s_o[0] = s_ref[...]


def _recur_call(rec, s0, hk, tt, name):
    b, t, _ = rec.shape
    w = 3 * hk + 256
    heads = 4
    dk = hk // heads
    e2 = np.kron(np.eye(heads), np.ones((dk, 64))).astype(np.float32)
    sel = np.zeros((SUB, SUB * SUB), np.float32)
    for i in range(SUB):
        sel[i, SUB * i:SUB * i + i + 1] = 1.0
    return pl.pallas_call(
        functools.partial(_recur_body, hk=hk),
        grid=(b, t // tt),
        in_specs=[pl.BlockSpec((1, tt, w), lambda bi, ti: (bi, ti, 0)),
                  pl.BlockSpec((1, hk, 256), lambda bi, ti: (bi, 0, 0)),
                  _const_spec((hk, 256)), _const_spec((SUB, SUB * SUB)), _const_spec((hk, 256))],
        out_specs=[pl.BlockSpec((1, tt, 256), lambda bi, ti: (bi, ti, 0)),
                   pl.BlockSpec((1, hk, 256), lambda bi, ti: (bi, 0, 0))],
        out_shape=[jax.ShapeDtypeStruct((b, t, 256), F32), jax.ShapeDtypeStruct((b, hk, 256), F32)],
        scratch_shapes=[pltpu.VMEM((hk, 256), F32)],
        compiler_params=_params(("arbitrary", "arbitrary")),
        name=name,
    )(rec, s0, jnp.asarray(e2, BF16), jnp.asarray(sel, BF16), jnp.asarray(e2, F32))


def _outffn_body(x_ref, om_ref, oh_ref, og_ref, gate_ref, g1_ref, sh2_ref, sc2_ref, g2_ref,
                 mon_ref, hon_ref, gon_ref, nf_ref, m64_ref, wo_ref, wfi_ref, wfo_ref, y_ref, *, ff_chunk):
    bb, tt, d = x_ref.shape
    tm = bb * tt
    om = om_ref[...].reshape(tm, MLA_WIDTH)
    omn = _rms(om) * mon_ref[...]

    def headnorm(o, gain_ref, gate):
        ms = jnp.dot((o * o).astype(BF16), m64_ref[...], preferred_element_type=F32)
        return o * lax.rsqrt(ms + EPS) * gain_ref[...] * gate

    gate = gate_ref[...].reshape(tm, 512)
    ohn = headnorm(oh_ref[...].reshape(tm, 256), hon_ref, gate[:, 0:256])
    ogn = headnorm(og_ref[...].reshape(tm, 256), gon_ref, gate[:, 256:512])
    mix_in = jnp.concatenate([omn, ohn, ogn], axis=-1).astype(BF16)
    mix = jnp.dot(mix_in, wo_ref[...], preferred_element_type=F32)
    x1 = x_ref[...] + g1_ref[...] * mix.reshape(bb, tt, d)
    h2 = (_rms(x1) * nf_ref[...] * (1.0 + sc2_ref[...]) + sh2_ref[...]).reshape(tm, d).astype(BF16)
    acc = jnp.zeros((tm, d), F32)
    for c in range(D_FF // ff_chunk):
        a = jnp.dot(h2, wfi_ref[:, ff_chunk * c:ff_chunk * (c + 1)], preferred_element_type=F32)
        u = jnp.dot(h2, wfi_ref[:, D_FF + ff_chunk * c:D_FF + ff_chunk * (c + 1)], preferred_element_type=F32)
        act = (_silu(a) * u).astype(BF16)
        acc = acc + jnp.dot(act, wfo_ref[ff_chunk * c:ff_chunk * (c + 1), :], preferred_element_type=F32)
    y_ref[...] = x1 + g2_ref[...] * acc.reshape(bb, tt, d)


def _outffn_call(x, om, oh, og, gate, mods, lw, bb, tt):
    b, t, d = x.shape
    tok = lambda w: pl.BlockSpec((bb, tt, w), lambda i, j: (i, j, 0))
    modspec = pl.BlockSpec((bb, 1, d), lambda i, j: (i, 0, 0))
    consts = [lw["mon"], lw["hon"], lw["gon"], lw["nf"], lw["m64"], lw["wo"], lw["wfi"], lw["wfo"]]
    single = lambda a: pl.BlockSpec(a.shape, lambda *_: (0,) * a.ndim, pipeline_mode=pl.Buffered(1))
    return pl.pallas_call(
        functools.partial(_outffn_body, ff_chunk=256),
        grid=(b // bb, t // tt),
        in_specs=[tok(d), tok(MLA_WIDTH), tok(256), tok(256), tok(512)] + [modspec] * 4 + [single(a) for a in consts],
        out_specs=tok(d),
        out_shape=jax.ShapeDtypeStruct((b, t, d), F32),
        compiler_params=_params(("arbitrary", "arbitrary")),
        name="outffn",
    )(x, om, oh, og, gate, *mods, *consts)


def _layer_weights(l, w_in, norm_mix, norm_ffn, mla_cq_norm, mla_ckv_norm, mla_w_uq, mla_w_uk, mla_w_uv, mla_q_norm,
                   mla_k_norm, mla_out_norm, hg_lb_logits, hg_out_norm, gla_w_a2, gla_b_a, gla_out_norm, w_o,
                   w_ffn_in, w_ffn_out):
    wi = w_in[l]
    d = wi.shape[0]
    o = dict(cq=0, ckv=256, kr=512, hq=544, hf=800, hi=1056, hg=1312, gq=1568, gk=1696, gv=1824, gg=2080, ga=2336)
    cols = [wi[:, o["cq"]:o["cq"] + 256], wi[:, o["ckv"]:o["ckv"] + 256], wi[:, o["hq"]:o["hq"] + 256],
            wi[:, o["hf"]:o["hf"] + 256], wi[:, o["hi"]:o["hi"] + 256], wi[:, o["hg"]:o["hg"] + 256],
            wi[:, o["gq"]:o["gq"] + 128], wi[:, o["gk"]:o["gk"] + 128], wi[:, o["gv"]:o["gv"] + 256],
            wi[:, o["gg"]:o["gg"] + 256], wi[:, o["kr"]:o["kr"] + 32], wi[:, o["ga"]:o["ga"] + 16],
            jnp.zeros((d, LANES - MLA_ROPE - GLA_RANK), wi.dtype)]
    win = jnp.concatenate(cols, axis=1).astype(BF16)
    assert win.shape[1] == ZCOLS

    uq = mla_w_uq[l]
    r = uq.shape[0]
    half = MLA_ROPE // 2
    zpad = jnp.zeros((r, MLA_HEADS, HEAD_SLAB - MLA_NOPE - MLA_ROPE), uq.dtype)
    uq_a = jnp.concatenate([uq, zpad], axis=-1).reshape(r, -1)
    uq_b = jnp.concatenate([jnp.zeros((r, MLA_HEADS, MLA_NOPE), uq.dtype), uq[..., MLA_NOPE + half:],
                            uq[..., MLA_NOPE:MLA_NOPE + half], zpad], axis=-1).reshape(r, -1)
    wuq = jnp.concatenate([uq_a, uq_b], axis=1).astype(BF16)
    uk = mla_w_uk[l]
    wuk = jnp.concatenate([uk, jnp.zeros((r, MLA_HEADS, HEAD_SLAB - MLA_NOPE), uk.dtype)], axis=-1)
    wuk = wuk.reshape(r, -1).astype(BF16)
    wuv = mla_w_uv[l].reshape(r, -1).astype(BF16)

    qn, kn = mla_q_norm[l], mla_k_norm[l]
    z32 = jnp.zeros((HEAD_SLAB - MLA_NOPE - MLA_ROPE,), F32)
    qga = jnp.concatenate([qn, z32])[None]
    qgb = jnp.concatenate([jnp.zeros((MLA_NOPE,), F32), qn[MLA_NOPE + half:], qn[MLA_NOPE:MLA_NOPE + half], z32])[None]
    kgk = jnp.concatenate([kn[:MLA_NOPE], jnp.zeros((HEAD_SLAB - MLA_NOPE,), F32)])[None]
    gkr = kn[MLA_NOPE:][None]

    blk = np.zeros((HEAD_SLAB, HEAD_SLAB), np.float32)
    blk[:MLA_NOPE, :MLA_NOPE] = 1.0 / MLA_NOPE
    mk = np.kron(np.eye(2), blk)
    blk_q = blk.copy()
    blk_q[MLA_NOPE:MLA_NOPE + MLA_ROPE, MLA_NOPE:MLA_NOPE + MLA_ROPE] = 1.0 / MLA_ROPE
    mq = np.kron(np.eye(2), blk_q)
    m64 = np.kron(np.eye(4), np.full((64, 64), 1.0 / 64, np.float32))
    tile = np.zeros((MLA_ROPE, 2 * HEAD_SLAB), np.float32)
    for hd in range(2):
        tile[np.arange(MLA_ROPE), HEAD_SLAB * hd + MLA_NOPE + np.arange(MLA_ROPE)] = 1.0

    def norm_bound(g):
        return jnp.sqrt(MLA_NOPE * jnp.max(jnp.abs(g[:MLA_NOPE])) ** 2 + MLA_ROPE * jnp.max(jnp.abs(g[MLA_NOPE:])) ** 2)
    logit_bound = (MLA_SCALE * norm_bound(qn) * norm_bound(kn)).reshape(1).astype(F32)

    return dict(
        logit_bound=logit_bound, nm=norm_mix[l][None], win=win, cqn=mla_cq_norm[l][None], ckvn=mla_ckv_norm[l][None], wuq=wuq, wuk=wuk,
        wuv=wuv, qga=qga, qgb=qgb, mq=jnp.asarray(mq, BF16), mk=jnp.asarray(mk, BF16), kgk=kgk, gkr=gkr,
        tile=jnp.asarray(tile, BF16), lbl=hg_lb_logits, wa2=gla_w_a2[l].astype(BF16), ba=gla_b_a[l][None],
        mon=mla_out_norm[l][None], hon=jnp.tile(hg_out_norm[l], HG_HEADS)[None],
        gon=jnp.tile(gla_out_norm[l], GLA_HEADS)[None], nf=norm_ffn[l][None], m64=jnp.asarray(m64, BF16),
        wo=w_o[l].astype(BF16), wfi=w_ffn_in[l].astype(BF16), wfo=w_ffn_out[l].astype(BF16))


def _rope_tables(pos, reps):
    inv = ROPE_THETA ** (-(jnp.arange(0, MLA_ROPE, 2, dtype=F32) / MLA_ROPE))
    ang = pos.astype(F32)[:, None] * inv[None, :]
    cos, sin = jnp.cos(ang), jnp.sin(ang)
    n = pos.shape[0]
    pad = HEAD_SLAB - MLA_NOPE - MLA_ROPE
    cosq = jnp.concatenate([jnp.ones((n, MLA_NOPE), F32), cos, cos, jnp.zeros((n, pad), F32)], axis=1)
    sinq = jnp.concatenate([jnp.zeros((n, MLA_NOPE), F32), -sin, sin, jnp.zeros((n, pad), F32)], axis=1)
    cosk = jnp.concatenate([cos, cos], axis=1)
    sink = jnp.concatenate([-sin, sin], axis=1)
    tabs = dict(cosq=cosq, sinq=sinq, cosk=cosk, sink=sink)
    return {k: jnp.tile(v, (reps, 1)) for k, v in tabs.items()}


def _block_diag_state(state):
    b, hh, dk, dv = state.shape
    eye = jnp.eye(hh, dtype=state.dtype)
    return (state[:, :, :, None, :] * eye[None, :, None, :, None]).reshape(b, hh * dk, hh * dv)


def _head_states(s_bd, hh):
    dk, dv = s_bd.shape[1] // hh, s_bd.shape[2] // hh
    return jnp.stack([s_bd[:, hd * dk:(hd + 1) * dk, hd * dv:(hd + 1) * dv] for hd in range(hh)], axis=1)


def _group_layer(x, mods, lw, tabs, layer, bb, tt, rtt, s_hg, s_gla, attn_fn):
    sh1, sc1, g1, sh2, sc2, g2 = mods
    ckv, kr, q, k, v, hrec, grec, gate = _inproj_call(x, sh1, sc1, lw, tabs, bb, tt, layer)
    om = attn_fn(q, k, v)
    oh, shg = _recur_call(hrec, s_hg, HG_HEADS * HG_DK, rtt, "recur_hgrn")
    og, sgl = _recur_call(grec, s_gla, GLA_HEADS * GLA_DK, rtt, "recur_gla")
    y = _outffn_call(x, om, oh, og, gate, (g1, sh2, sc2, g2), lw, bb, tt)
    return y, ckv, kr, _head_states(shg, HG_HEADS), _head_states(sgl, GLA_HEADS)


def kernel(x_prompt, x_sample, c_prompt, c_sample, cache_mla_ckv, cache_mla_krope, state_hgrn, state_gla, w_mod, b_mod, norm_mix, norm_ffn, w_in, mla_cq_norm, mla_ckv_norm, mla_w_uq, mla_w_uk, mla_w_uv, mla_q_norm, mla_k_norm, mla_out_norm, hg_lb_logits, hg_out_norm, gla_w_a2, gla_b_a, gla_out_norm, w_o, w_ffn_in, w_ffn_out):
    bp, t_p, d = x_prompt.shape
    bs, t_s, _ = x_sample.shape
    depth = w_mod.shape[0]
    tm = 512
    tt_p = min(tm, t_p)
    bb_s = max(1, min(bs, tm // t_s))
    rtt_p = min(256, t_p)
    rtt_s = min(128, t_s)
    attn_tile = min(512, t_p)
    tkc = min(1024, cache_mla_ckv.shape[2])

    tabs_p = _rope_tables(jnp.arange(t_p, dtype=jnp.int32), 1)
    tabs_s = _rope_tables(PAST_LEN + jnp.arange(t_s, dtype=jnp.int32), bb_s)

    n_c = bp + bs
    c_rows = -(-n_c // 16) * 16
    c_pad = jnp.concatenate([c_prompt, c_sample, jnp.zeros((c_rows - n_c, d), c_prompt.dtype)], axis=0)

    xp, xs = x_prompt, x_sample
    outs = [[] for _ in range(8)]
    for l in range(depth):
        lw = _layer_weights(l, w_in, norm_mix, norm_ffn, mla_cq_norm, mla_ckv_norm, mla_w_uq, mla_w_uk, mla_w_uv,
                            mla_q_norm, mla_k_norm, mla_out_norm, hg_lb_logits, hg_out_norm, gla_w_a2, gla_b_a,
                            gla_out_norm, w_o, w_ffn_in, w_ffn_out)
        mod = _mod_call(c_pad, w_mod[l], b_mod[l][None])
        mods_p = [mod[0:bp, i * d:(i + 1) * d][:, None, :] for i in range(6)]
        mods_s = [mod[bp:n_c, i * d:(i + 1) * d][:, None, :] for i in range(6)]

        zero_hg = jnp.zeros((bp, HG_HEADS * HG_DK, HG_HEADS * HG_DV), F32)
        zero_gla = jnp.zeros((bp, GLA_HEADS * GLA_DK, GLA_HEADS * GLA_DV), F32)
        xp, a, b_, c_, d_ = _group_layer(xp, mods_p, lw, tabs_p, l, 1, tt_p, rtt_p, zero_hg, zero_gla,
                                         lambda q, k, v: _attn_prompt_call(lw["logit_bound"], q, k, v, attn_tile))
        for lst, val in zip(outs[0:4], (a, b_, c_, d_)):
            lst.append(val)
        xs, a, b_, c_, d_ = _group_layer(xs, mods_s, lw, tabs_s, l, bb_s, t_s, rtt_s,
                                         _block_diag_state(state_hgrn[l]), _block_diag_state(state_gla[l]),
                                         lambda q, k, v: _attn_sample_call(lw["logit_bound"], q, k, v, cache_mla_ckv,
                                                                           cache_mla_krope, lw, l, tkc))
        for lst, val in zip(outs[4:8], (a, b_, c_, d_)):
            lst.append(val)

    return (xp, xs) + tuple(jnp.stack(o) for o in outs)
```

```python
import functools

import numpy as np
import jax
import jax.numpy as jnp
from jax import lax
from jax.experimental import pallas as pl
from jax.experimental.pallas import tpu as pltpu

F32, BF16 = jnp.float32, jnp.bfloat16

D_MODEL = 1024
PAST_LEN = 4096
CHUNK = 64
EPS = 1e-6
NEG_INF = -1e30
ROPE_THETA = 10000.0
MLA_HEADS = 8
MLA_NOPE = 64
MLA_ROPE = 32
MLA_V = 64
MLA_Q_RANK = 256
MLA_KV_RANK = 256
MLA_WIDTH = MLA_HEADS * MLA_V
MLA_SCALE = (MLA_NOPE + MLA_ROPE) ** -0.5
LOG2E = 1.4426950408889634
HG_HEADS = 4
HG_DK = 64
HG_DV = 64
GLA_HEADS = 4
GLA_DK = 32
GLA_DV = 64
GLA_RANK = 16
GLA_TAU = 16.0
D_FF = 2816

LANES = 128
HEAD_SLAB = 128
SUB = 16
ZCOLS = 2432
VMEM_LIMIT = 56 * 1024 * 1024

_Z = dict(cq=0, ckv=256, hq=512, hf=768, hi=1024, hg=1280, gq=1536, gk=1664, gv=1792, gg=2048, misc=2304)


def _bdot(a, b):
    return jnp.dot(a.astype(BF16), b.astype(BF16), preferred_element_type=F32)


def _dot_nt(a, b):
    return lax.dot_general(a, b, (((1,), (1,)), ((), ())), preferred_element_type=F32)


def _dot_tn(a, b):
    return lax.dot_general(a, b, (((0,), (0,)), ((), ())), preferred_element_type=F32)


def _sigmoid(x):
    return 1.0 / (1.0 + jnp.exp(-x))


def _silu(x):
    return x * _sigmoid(x)


def _log_sigmoid(x):
    return jnp.minimum(x, 0.0) - jnp.log1p(jnp.exp(-jnp.abs(x)))


def _rms(x):
    return x * lax.rsqrt(jnp.mean(x * x, axis=-1, keepdims=True) + EPS)


def _params(sem):
    return pltpu.CompilerParams(dimension_semantics=sem, vmem_limit_bytes=VMEM_LIMIT)


def _const_spec(shape):
    nd = len(shape)
    return pl.BlockSpec(shape, lambda *_: (0,) * nd)


def _mod_body(c_ref, w_ref, b_ref, o_ref):
    o_ref[...] = _bdot(_silu(c_ref[...]), w_ref[...]) + b_ref[...]


def _mod_call(c_pad, w_mod, b_mod):
    rows, d = c_pad.shape
    n = w_mod.shape[1]
    tn = 1024
    return pl.pallas_call(
        _mod_body,
        grid=(n // tn,),
        in_specs=[pl.BlockSpec((rows, d), lambda j: (0, 0)),
                  pl.BlockSpec((d, tn), lambda j: (0, j)),
                  pl.BlockSpec((1, tn), lambda j: (0, j))],
        out_specs=pl.BlockSpec((rows, tn), lambda j: (0, j)),
        out_shape=jax.ShapeDtypeStruct((rows, n), F32),
        compiler_params=_params(("arbitrary",)),
        name="mod",
    )(c_pad, w_mod, b_mod)


def _mla_kv(ckvn_bf, kro_bf, w_uk_ref, w_uv_ref, mk_ref, kgk_ref, tile_ref, transpose_v=False):
    krt = jnp.dot(kro_bf, tile_ref[...], preferred_element_type=F32)
    kgk2 = jnp.concatenate([kgk_ref[...], kgk_ref[...]], axis=-1)
    parts = []
    for hp in range(MLA_HEADS // 2):
        sl = slice(2 * HEAD_SLAB * hp, 2 * HEAD_SLAB * (hp + 1))
        sk = jnp.dot(ckvn_bf, w_uk_ref[:, sl], preferred_element_type=F32)
        msk = jnp.dot((sk * sk).astype(BF16), mk_ref[...], preferred_element_type=F32)
        parts.append((sk * lax.rsqrt(msk + EPS) * kgk2 + krt).astype(BF16))
    k = jnp.concatenate(parts, axis=-1)
    if transpose_v:
        v = _dot_nt(w_uv_ref[...], ckvn_bf).astype(BF16)
    else:
        v = jnp.dot(ckvn_bf, w_uv_ref[...], preferred_element_type=F32).astype(BF16)
    return k, v


def _inproj_body(x_ref, sh_ref, sc_ref, nm_ref, win_ref, cqn_ref, ckvn_ref, wuq_ref, wuk_ref, wuv_ref,
                 qga_ref, qgb_ref, cosq_ref, sinq_ref, mq_ref, mk_ref, kgk_ref, gkr_ref, cosk_ref, sink_ref,
                 tile_ref, lbl_ref, wa2_ref, ba_ref,
                 ckv_o, kr_o, q_o, k_o, v_o, hrec_o, grec_o, gate_o, *, layer, transpose_v):
    bb, tt, d = x_ref.shape
    tm = bb * tt
    x = x_ref[...]
    h = _rms(x) * nm_ref[...] * (1.0 + sc_ref[...]) + sh_ref[...]
    z = jnp.dot(h.reshape(tm, d).astype(BF16), win_ref[...], preferred_element_type=F32)

    def seg(name, w):
        return z[:, _Z[name]:_Z[name] + w]

    cqn = _rms(seg("cq", MLA_Q_RANK)) * cqn_ref[...]
    ckvn = _rms(seg("ckv", MLA_KV_RANK)) * ckvn_ref[...]
    ckv_o[...] = ckvn.reshape(bb, tt, MLA_KV_RANK)
    misc = seg("misc", LANES)

    qab = jnp.dot(cqn.astype(BF16), wuq_ref[...], preferred_element_type=F32)
    width = MLA_HEADS * HEAD_SLAB
    ca = cosq_ref[...] * qga_ref[...]
    sb = sinq_ref[...] * qgb_ref[...]
    ca2 = jnp.concatenate([ca, ca], axis=-1)
    sb2 = jnp.concatenate([sb, sb], axis=-1)
    for hp in range(MLA_HEADS // 2):
        sl = slice(2 * HEAD_SLAB * hp, 2 * HEAD_SLAB * (hp + 1))
        qa = qab[:, sl]
        qb = qab[:, width + sl.start:width + sl.stop]
        msq = jnp.dot((qa * qa).astype(BF16), mq_ref[...], preferred_element_type=F32)
        qo = lax.rsqrt(msq + EPS) * (qa * ca2 + qb * sb2) * (MLA_SCALE * LOG2E)
        q_o[:, :, sl] = qo.astype(BF16).reshape(bb, tt, 2 * HEAD_SLAB)

    kr = misc[:, 0:MLA_ROPE]
    krn = _rms(kr) * gkr_ref[...]
    half = MLA_ROPE // 2
    swapped = jnp.concatenate([krn[:, half:], krn[:, :half]], axis=-1)
    kro = krn * cosk_ref[...] + swapped * sink_ref[...]
    kr_o[...] = kro.reshape(bb, tt, MLA_ROPE)
    k, v = _mla_kv(ckvn.astype(BF16), kro.astype(BF16), wuk_ref, wuv_ref, mk_ref, kgk_ref, tile_ref, transpose_v)
    k_o[...] = k.reshape(bb, tt, width)
    if transpose_v:
        v_o[0] = v
    else:
        v_o[...] = v.reshape(bb, tt, MLA_WIDTH)

    lg = lbl_ref[...]
    e = jnp.exp(lg - jnp.max(lg, axis=0, keepdims=True))
    p = e / jnp.sum(e, axis=0, keepdims=True)
    lb = jnp.sum(p[0:layer + 1], axis=0, keepdims=True) - p[0:1]
    f = lb + (1.0 - lb) * _sigmoid(seg("hf", 256))
    hrec_o[:, :, 0:256] = seg("hq", 256).reshape(bb, tt, 256)
    hrec_o[:, :, 256:512] = (1.0 - f).reshape(bb, tt, 256)
    hrec_o[:, :, 512:768] = jnp.log(f).reshape(bb, tt, 256)
    hrec_o[:, :, 768:1024] = seg("hi", 256).reshape(bb, tt, 256)

    ga = misc[:, MLA_ROPE:MLA_ROPE + GLA_RANK]
    apre = jnp.dot(ga.astype(BF16), wa2_ref[...], preferred_element_type=F32) + ba_ref[...]
    grec_o[:, :, 0:128] = (seg("gq", 128) * (GLA_DK ** -0.5)).reshape(bb, tt, 128)
    grec_o[:, :, 128:256] = seg("gk", 128).reshape(bb, tt, 128)
    grec_o[:, :, 256:384] = (_log_sigmoid(apre) / GLA_TAU).reshape(bb, tt, 128)
    grec_o[:, :, 384:640] = seg("gv", 256).reshape(bb, tt, 256)

    gate_o[:, :, 0:256] = _silu(seg("hg", 256)).reshape(bb, tt, 256)
    gate_o[:, :, 256:512] = _silu(seg("gg", 256)).reshape(bb, tt, 256)


def _inproj_call(x, sh, sc, lw, tabs, bb, tt, layer, transpose_v):
    b, t, d = x.shape
    tm = bb * tt
    grid = (b // bb, t // tt)
    tok = lambda w: pl.BlockSpec((bb, tt, w), lambda i, j: (i, j, 0))
    modspec = pl.BlockSpec((bb, 1, d), lambda i, j: (i, 0, 0))
    tab = lambda w: pl.BlockSpec((tm, w), lambda i, j: (j, 0))
    assert bb == 1 or not transpose_v
    consts = [lw["nm"], lw["win"], lw["cqn"], lw["ckvn"], lw["wuq"], lw["wuk"], lw["wuvt" if transpose_v else "wuv"],
              lw["qga"], lw["qgb"]]
    consts2 = [lw["mq"], lw["mk"], lw["kgk"], lw["gkr"]]
    consts3 = [lw["tile"], lw["lbl"], lw["wa2"], lw["ba"]]
    in_specs = ([tok(d), modspec, modspec] + [_const_spec(a.shape) for a in consts]
                + [tab(LANES), tab(LANES)] + [_const_spec(a.shape) for a in consts2]
                + [tab(MLA_ROPE), tab(MLA_ROPE)] + [_const_spec(a.shape) for a in consts3])
    widths = [(MLA_KV_RANK, F32), (MLA_ROPE, F32), (MLA_HEADS * HEAD_SLAB, BF16), (MLA_HEADS * HEAD_SLAB, BF16),
              (MLA_WIDTH, BF16), (1024, F32), (640, F32), (512, F32)]
    out_specs = [tok(w) for w, _ in widths]
    out_shape = [jax.ShapeDtypeStruct((b, t, w), dt) for w, dt in widths]
    if transpose_v:
        out_specs[4] = pl.BlockSpec((1, MLA_WIDTH, tt), lambda i, j: (i, 0, j))
        out_shape[4] = jax.ShapeDtypeStruct((b, MLA_WIDTH, t), BF16)
    return pl.pallas_call(
        functools.partial(_inproj_body, layer=layer, transpose_v=transpose_v),
        grid=grid,
        in_specs=in_specs,
        out_specs=out_specs,
        out_shape=out_shape,
        compiler_params=_params(("arbitrary", "arbitrary")),
        name="inproj",
    )(x, sh, sc, *consts, tabs["cosq"], tabs["sinq"], *consts2, tabs["cosk"], tabs["sink"], *consts3)


MAX_UNSHIFTED_LOGIT = 60.0
SMALL_SCORE_TILE = 64 * 1024


def _attend_bounded(q, k, v, lf_ref, accf_ref, bias):
    tq, tk = q.shape[0], k.shape[0]

    def scores(hd):
        sl = slice(HEAD_SLAB * hd, HEAD_SLAB * (hd + 1))
        s = _dot_nt(q[:, sl], k[:, sl])
        return s if bias is None else s + bias

    def accumulate(hd, s):
        p = jnp.exp2(s)
        if tk % LANES == 0:
            part = p[:, 0:LANES]
            for c in range(1, tk // LANES):
                part = part + p[:, LANES * c:LANES * (c + 1)]
        else:
            lane0 = lax.broadcasted_iota(jnp.int32, (tq, LANES), 1) == 0
            part = jnp.where(lane0, jnp.sum(p, axis=-1, keepdims=True), 0.0)
        lf_ref[hd] += part
        hp = hd // 2
        accf_ref[hd] += jnp.dot(p.astype(BF16), v[:, LANES * hp:LANES * (hp + 1)], preferred_element_type=F32)

    if tq * tk <= SMALL_SCORE_TILE:
        ss = [scores(hd) for hd in range(MLA_HEADS)]
        for hd in range(MLA_HEADS):
            accumulate(hd, ss[hd])
    else:
        for hd in range(MLA_HEADS):
            accumulate(hd, scores(hd))


def _finish_bounded(lf_ref, accf_ref):
    tq = accf_ref.shape[1]
    low = lax.broadcasted_iota(jnp.int32, (tq, LANES), 1) < MLA_V
    outs = []
    for hp in range(MLA_HEADS // 2):
        la = jnp.sum(lf_ref[2 * hp], axis=-1, keepdims=True)
        lb = jnp.sum(lf_ref[2 * hp + 1], axis=-1, keepdims=True)
        outs.append(jnp.where(low, accf_ref[2 * hp] / la, accf_ref[2 * hp + 1] / lb))
    return jnp.concatenate(outs, axis=-1)


def _attend_online(q, k, v, m_ref, l_ref, acc_ref, bias):
    tq = q.shape[0]
    low = lax.broadcasted_iota(jnp.int32, (tq, LANES), 1) < MLA_V
    for hp in range(MLA_HEADS // 2):
        vp = v[:, LANES * hp:LANES * (hp + 1)]
        pvs, alphas = [], []
        for e in range(2):
            hd = 2 * hp + e
            sl = slice(HEAD_SLAB * hd, HEAD_SLAB * (hd + 1))
            s = _dot_nt(q[:, sl], k[:, sl])
            if bias is not None:
                s = s + bias
            m_prev = m_ref[hd]
            m_new = jnp.maximum(m_prev, jnp.max(s, axis=-1, keepdims=True))
            p = jnp.exp2(s - m_new)
            alpha = jnp.exp2(m_prev - m_new)
            l_ref[hd] = alpha * l_ref[hd] + jnp.sum(p, axis=-1, keepdims=True)
            m_ref[hd] = m_new
            pvs.append(jnp.dot(p.astype(BF16), vp, preferred_element_type=F32))
            alphas.append(alpha)
        sl2 = slice(LANES * hp, LANES * (hp + 1))
        acc_ref[:, sl2] = jnp.where(low, alphas[0], alphas[1]) * acc_ref[:, sl2] + jnp.where(low, pvs[0], pvs[1])


def _finish_online(l_ref, acc_ref):
    tq = acc_ref.shape[0]
    low = lax.broadcasted_iota(jnp.int32, (tq, LANES), 1) < MLA_V
    outs = []
    for hp in range(MLA_HEADS // 2):
        l2 = jnp.where(low, l_ref[2 * hp], l_ref[2 * hp + 1])
        outs.append(acc_ref[:, LANES * hp:LANES * (hp + 1)] / l2)
    return jnp.concatenate(outs, axis=-1)


def _softmax_scratch(tq):
    return [pltpu.VMEM((MLA_HEADS, tq, 1), F32), pltpu.VMEM((MLA_HEADS, tq, 1), F32), pltpu.VMEM((tq, MLA_WIDTH), F32),
            pltpu.VMEM((MLA_HEADS, tq, LANES), F32), pltpu.VMEM((MLA_HEADS, tq, LANES), F32)]


def _softmax_init(m_ref, l_ref, acc_ref, lf_ref, accf_ref):
    m_ref[...] = jnp.full(m_ref.shape, NEG_INF, F32)
    for ref in (l_ref, acc_ref, lf_ref, accf_ref):
        ref[...] = jnp.zeros(ref.shape, F32)


def _attend(bounded, cond, get_qkv, scratch, bias_ref):
    m_ref, l_ref, acc_ref, lf_ref, accf_ref = scratch

    @pl.when(jnp.logical_and(cond, bounded))
    def _():
        _attend_bounded(*get_qkv(), lf_ref, accf_ref, None if bias_ref is None else bias_ref[...])

    @pl.when(jnp.logical_and(cond, jnp.logical_not(bounded)))
    def _():
        _attend_online(*get_qkv(), m_ref, l_ref, acc_ref, None if bias_ref is None else bias_ref[...])


def _softmax_finish(bounded, cond, o_ref, scratch):
    m_ref, l_ref, acc_ref, lf_ref, accf_ref = scratch

    @pl.when(jnp.logical_and(cond, bounded))
    def _():
        o_ref[0] = _finish_bounded(lf_ref, accf_ref)

    @pl.when(jnp.logical_and(cond, jnp.logical_not(bounded)))
    def _():
        o_ref[0] = _finish_online(l_ref, acc_ref)


def _attend_t_bounded(q, k, vt, lf_ref, acc_ref, bias_t):
    tq, tk = q.shape[0], k.shape[0]

    def scores_t(hd):
        sl = slice(HEAD_SLAB * hd, HEAD_SLAB * (hd + 1))
        st = _dot_nt(k[:, sl], q[:, sl])
        return st if bias_t is None else st + bias_t

    st_next = scores_t(0)
    for hd in range(MLA_HEADS):
        st = st_next
        if hd + 1 < MLA_HEADS:
            st_next = scores_t(hd + 1)
        pt = jnp.exp2(st)
        lf_ref[hd] += jnp.sum(pt.reshape(tk // 8, 8, tq), axis=0)
        rows = slice(MLA_V * hd, MLA_V * (hd + 1))
        acc_ref[rows, :] += jnp.dot(vt[rows, :], pt.astype(BF16), preferred_element_type=F32)


def _attend_t_online(q, k, vt, m_ref, l_ref, acc_ref, bias_t):
    for hd in range(MLA_HEADS):
        sl = slice(HEAD_SLAB * hd, HEAD_SLAB * (hd + 1))
        st = _dot_nt(k[:, sl], q[:, sl])
        if bias_t is not None:
            st = st + bias_t
        m_prev = m_ref[hd]
        m_new = jnp.maximum(m_prev, jnp.max(st, axis=0, keepdims=True))
        pt = jnp.exp2(st - m_new)
        alpha = jnp.exp2(m_prev - m_new)
        l_ref[hd] = alpha * l_ref[hd] + jnp.sum(pt, axis=0, keepdims=True)
        m_ref[hd] = m_new
        rows = slice(MLA_V * hd, MLA_V * (hd + 1))
        acc_ref[rows, :] = alpha * acc_ref[rows, :] + jnp.dot(vt[rows, :], pt.astype(BF16), preferred_element_type=F32)


def _finish_t(l_of_head, acc_ref):
    outs = []
    for hd in range(MLA_HEADS):
        outs.append(acc_ref[MLA_V * hd:MLA_V * (hd + 1), :] / l_of_head(hd))
    return jnp.concatenate(outs, axis=0).T


def _attn_prompt_body(qi_ref, ki_ref, bound_ref, bias_ref, q_ref, k_ref, vt_ref, o_ref, m_ref, l_ref, lf_ref, acc_ref):
    step = pl.program_id(1)
    i = qi_ref[step]
    j = ki_ref[step]
    bounded = bound_ref[0] <= MAX_UNSHIFTED_LOGIT
    unbounded = jnp.logical_not(bounded)
    diag = j == i

    @pl.when(j == 0)
    def _():
        m_ref[...] = jnp.full(m_ref.shape, NEG_INF, F32)
        for ref in (l_ref, lf_ref, acc_ref):
            ref[...] = jnp.zeros(ref.shape, F32)

    for on_diag in (False, True):
        cond = diag if on_diag else j < i

        @pl.when(jnp.logical_and(cond, bounded))
        def _():
            _attend_t_bounded(q_ref[0], k_ref[0], vt_ref[0], lf_ref, acc_ref, bias_ref[...] if on_diag else None)

        @pl.when(jnp.logical_and(cond, unbounded))
        def _():
            _attend_t_online(q_ref[0], k_ref[0], vt_ref[0], m_ref, l_ref, acc_ref, bias_ref[...] if on_diag else None)

    @pl.when(jnp.logical_and(diag, bounded))
    def _():
        o_ref[0] = _finish_t(lambda hd: jnp.sum(lf_ref[hd], axis=0, keepdims=True), acc_ref)

    @pl.when(jnp.logical_and(diag, unbounded))
    def _():
        o_ref[0] = _finish_t(lambda hd: l_ref[hd], acc_ref)


def _attn_prompt_call(bound, q, k, vt, tile):
    b, t, _ = q.shape
    n = t // tile
    pairs = [(i, j) for i in range(n) for j in range(i + 1)]
    qi = jnp.asarray([p[0] for p in pairs], jnp.int32)
    ki = jnp.asarray([p[1] for p in pairs], jnp.int32)
    q_map = lambda bi, st, qi_ref, ki_ref: (bi, qi_ref[st], 0)
    k_map = lambda bi, st, qi_ref, ki_ref: (bi, ki_ref[st], 0)
    vt_map = lambda bi, st, qi_ref, ki_ref: (bi, 0, ki_ref[st])
    chunk_of = np.arange(tile) // CHUNK
    bias_t = jnp.asarray(np.where(chunk_of[:, None] <= chunk_of[None, :], 0.0, NEG_INF), F32)
    grid_spec = pltpu.PrefetchScalarGridSpec(
        num_scalar_prefetch=2,
        grid=(b, len(pairs)),
        in_specs=[pl.BlockSpec(memory_space=pltpu.SMEM), pl.BlockSpec((tile, tile), lambda *_: (0, 0)),
                  pl.BlockSpec((1, tile, q.shape[2]), q_map),
                  pl.BlockSpec((1, tile, k.shape[2]), k_map),
                  pl.BlockSpec((1, MLA_WIDTH, tile), vt_map)],
        out_specs=pl.BlockSpec((1, tile, MLA_WIDTH), q_map),
        scratch_shapes=[pltpu.VMEM((MLA_HEADS, 1, tile), F32), pltpu.VMEM((MLA_HEADS, 1, tile), F32),
                        pltpu.VMEM((MLA_HEADS, 8, tile), F32), pltpu.VMEM((MLA_WIDTH, tile), F32)])
    return pl.pallas_call(
        _attn_prompt_body,
        grid_spec=grid_spec,
        out_shape=jax.ShapeDtypeStruct((b, t, MLA_WIDTH), F32),
        compiler_params=_params(("arbitrary", "arbitrary")),
        name="attn_prompt",
    )(qi, ki, bound, bias_t, q, k, vt)


def _attn_sample_body(bound_ref, q_ref, kn_ref, vn_ref, ckv_ref, kr_ref, wuk_ref, wuv_ref, mk_ref, kgk_ref, tile_ref,
                      o_ref, *scratch):
    j = pl.program_id(1)
    bounded = bound_ref[0] <= MAX_UNSHIFTED_LOGIT
    always = j >= 0

    @pl.when(j == 0)
    def _():
        _softmax_init(*scratch)

    _attend(bounded, j == 0, lambda: (q_ref[0], kn_ref[0], vn_ref[0]), scratch, None)

    def past_qkv():
        k, v = _mla_kv(ckv_ref[0, 0].astype(BF16), kr_ref[0, 0].astype(BF16), wuk_ref, wuv_ref, mk_ref, kgk_ref,
                       tile_ref)
        return q_ref[0], k, v

    _attend(bounded, always, past_qkv, scratch, None)
    _softmax_finish(bounded, j == pl.num_programs(1) - 1, o_ref, scratch)


def _attn_sample_call(bound, q, k_new, v_new, cache_ckv, cache_kr, lw, layer, tkc):
    b, t, _ = q.shape
    past = cache_ckv.shape[2]
    consts = [lw["wuk"], lw["wuv"], lw["mk"], lw["kgk"], lw["tile"]]
    new = lambda w: pl.BlockSpec((1, t, w), lambda bi, j: (bi, 0, 0))
    return pl.pallas_call(
        _attn_sample_body,
        grid=(b, past // tkc),
        in_specs=[pl.BlockSpec(memory_space=pltpu.SMEM), new(q.shape[2]), new(k_new.shape[2]), new(v_new.shape[2]),
                  pl.BlockSpec((1, 1, tkc, MLA_KV_RANK), lambda bi, j: (layer, bi, j, 0)),
                  pl.BlockSpec((1, 1, tkc, MLA_ROPE), lambda bi, j: (layer, bi, j, 0))]
                 + [_const_spec(a.shape) for a in consts],
        out_specs=new(MLA_WIDTH),
        out_shape=jax.ShapeDtypeStruct((b, t, MLA_WIDTH), F32),
        scratch_shapes=_softmax_scratch(t),
        compiler_params=_params(("arbitrary", "arbitrary")),
        name="attn_sample",
    )(bound, q, k_new, v_new, cache_ckv, cache_kr, *consts)


def _recur_body(rec_ref, s0_ref, e2_ref, sel_ref, bd_ref, o_ref, s_o, s_ref, *, hk):
    t = pl.program_id(1)
    tt = rec_ref.shape[1]

    @pl.when(t == 0)
    def _():
        s_ref[...] = s0_ref[0]

    rec = rec_ref[0]
    q = rec[:, 0:hk]
    k = rec[:, hk:2 * hk]
    g = rec[:, 2 * hk:3 * hk]
    v = rec[:, 3 * hk:3 * hk + 256]

    rows = lax.broadcasted_iota(jnp.int32, (2 * tt, tt), 0)
    cols = lax.broadcasted_iota(jnp.int32, (2 * tt, tt), 1)
    top = rows < tt
    rr = jnp.where(top, rows, rows - tt)
    ones = ((rr >> 4) == (cols >> 4)) & (jnp.logical_not(top) | (cols <= rr))
    lmat = jnp.where(ones, 1.0, 0.0).astype(BF16)
    g1 = g.astype(BF16)
    r1 = g - g1.astype(F32)
    g2 = r1.astype(BF16)
    g3 = (r1 - g2.astype(F32)).astype(BF16)
    bb = jnp.dot(lmat, jnp.concatenate([g1, g2, g3], axis=-1), preferred_element_type=F32)
    bsum = (bb[:, 2 * hk:3 * hk] + bb[:, hk:2 * hk]) + bb[:, 0:hk]
    b = bsum[0:tt]
    btot = bsum[tt:2 * tt]

    qe = (q * jnp.exp(b)).astype(BF16)
    ke = (k * jnp.exp(btot - b)).astype(BF16)
    dec = jnp.exp(btot)
    vb = v.astype(BF16)

    nb = tt // SUB
    xs = []
    b2 = b * LOG2E
    for n in range(nb):
        r0 = SUB * n
        bj, kj = b2[r0:r0 + SUB], k[r0:r0 + SUB]
        for i0 in range(SUB):
            bi = b2[r0 + i0:r0 + i0 + 1]
            qi = q[r0 + i0:r0 + i0 + 1]
            xs.append((jnp.exp2(jnp.minimum(bi - bj, 0.0)) * (kj * qi)).astype(BF16))
    x = jnp.concatenate(xs, axis=0)
    r = jnp.dot(x, e2_ref[...], preferred_element_type=F32)
    y = (r.reshape(nb, SUB, SUB, 256) * v.reshape(nb, 1, SUB, 256)).reshape(nb * SUB * SUB, 256).astype(BF16)

    dsel = jnp.concatenate([dec[SUB * n:SUB * n + 1] for n in range(nb)], axis=0).T
    us, o_diags = [], []
    for n in range(nb):
        blk = slice(SUB * n, SUB * (n + 1))
        us.append(_dot_tn(ke[blk], vb[blk]) * bd_ref[...])
        o_diags.append(jnp.dot(sel_ref[...], y[SUB * SUB * n:SUB * SUB * (n + 1)], preferred_element_type=F32))
    s = s_ref[...]
    for n in range(nb):
        blk = slice(SUB * n, SUB * (n + 1))
        o_ref[0, blk, :] = jnp.dot(qe[blk], s.astype(BF16), preferred_element_type=F32) + o_diags[n]
        s = dsel[:, n:n + 1] * s + us[n]
    s_ref[...] = s

    @pl.when(t == pl.num_programs(1) - 1)
    def _():
        s_o[0] = s_ref[...]


def _recur_call(rec, s0, hk, tt, name):
    b, t, _ = rec.shape
    w = 3 * hk + 256
    heads = 4
    dk = hk // heads
    e2 = np.kron(np.eye(heads), np.ones((dk, 64))).astype(np.float32)
    sel = np.zeros((SUB, SUB * SUB), np.float32)
    for i in range(SUB):
        sel[i, SUB * i:SUB * i + i + 1] = 1.0
    return pl.pallas_call(
        functools.partial(_recur_body, hk=hk),
        grid=(b, t // tt),
        in_specs=[pl.BlockSpec((1, tt, w), lambda bi, ti: (bi, ti, 0)),
                  pl.BlockSpec((1, hk, 256), lambda bi, ti: (bi, 0, 0)),
                  _const_spec((hk, 256)), _const_spec((SUB, SUB * SUB)), _const_spec((hk, 256))],
        out_specs=[pl.BlockSpec((1, tt, 256), lambda bi, ti: (bi, ti, 0)),
                   pl.BlockSpec((1, hk, 256), lambda bi, ti: (bi, 0, 0))],
        out_shape=[jax.ShapeDtypeStruct((b, t, 256), F32), jax.ShapeDtypeStruct((b, hk, 256), F32)],
        scratch_shapes=[pltpu.VMEM((hk, 256), F32)],
        compiler_params=_params(("arbitrary", "arbitrary")),
        name=name,
    )(rec, s0, jnp.asarray(e2, BF16), jnp.asarray(sel, BF16), jnp.asarray(e2, F32))


def _outffn_body(x_ref, om_ref, oh_ref, og_ref, gate_ref, g1_ref, sh2_ref, sc2_ref, g2_ref,
                 mon_ref, hon_ref, gon_ref, nf_ref, m64_ref, wo_ref, wfi_ref, wfo_ref, y_ref, *, ff_chunk):
    bb, tt, d = x_ref.shape
    tm = bb * tt
    om = om_ref[...].reshape(tm, MLA_WIDTH)
    omn = _rms(om) * mon_ref[...]

    def headnorm(o, gain_ref, gate):
        ms = jnp.dot((o * o).astype(BF16), m64_ref[...], preferred_element_type=F32)
        return o * lax.rsqrt(ms + EPS) * gain_ref[...] * gate

    gate = gate_ref[...].reshape(tm, 512)
    ohn = headnorm(oh_ref[...].reshape(tm, 256), hon_ref, gate[:, 0:256])
    ogn = headnorm(og_ref[...].reshape(tm, 256), gon_ref, gate[:, 256:512])
    mix_in = jnp.concatenate([omn, ohn, ogn], axis=-1).astype(BF16)
    mix = jnp.dot(mix_in, wo_ref[...], preferred_element_type=F32)
    x1 = x_ref[...] + g1_ref[...] * mix.reshape(bb, tt, d)
    h2 = (_rms(x1) * nf_ref[...] * (1.0 + sc2_ref[...]) + sh2_ref[...]).reshape(tm, d).astype(BF16)
    acc = jnp.zeros((tm, d), F32)
    for c in range(D_FF // ff_chunk):
        a = jnp.dot(h2, wfi_ref[:, ff_chunk * c:ff_chunk * (c + 1)], preferred_element_type=F32)
        u = jnp.dot(h2, wfi_ref[:, D_FF + ff_chunk * c:D_FF + ff_chunk * (c + 1)], preferred_element_type=F32)
        act = (_silu(a) * u).astype(BF16)
        acc = acc + jnp.dot(act, wfo_ref[ff_chunk * c:ff_chunk * (c + 1), :], preferred_element_type=F32)
    y_ref[...] = x1 + g2_ref[...] * acc.reshape(bb, tt, d)


def _outffn_call(x, om, oh, og, gate, mods, lw, bb, tt):
    b, t, d = x.shape
    tok = lambda w: pl.BlockSpec((bb, tt, w), lambda i, j: (i, j, 0))
    modspec = pl.BlockSpec((bb, 1, d), lambda i, j: (i, 0, 0))
    consts = [lw["mon"], lw["hon"], lw["gon"], lw["nf"], lw["m64"], lw["wo"], lw["wfi"], lw["wfo"]]
    single = lambda a: pl.BlockSpec(a.shape, lambda *_: (0,) * a.ndim, pipeline_mode=pl.Buffered(1))
    return pl.pallas_call(
        functools.partial(_outffn_body, ff_chunk=256),
        grid=(b // bb, t // tt),
        in_specs=[tok(d), tok(MLA_WIDTH), tok(256), tok(256), tok(512)] + [modspec] * 4 + [single(a) for a in consts],
        out_specs=tok(d),
        out_shape=jax.ShapeDtypeStruct((b, t, d), F32),
        compiler_params=_params(("arbitrary", "arbitrary")),
        name="outffn",
    )(x, om, oh, og, gate, *mods, *consts)


def _layer_weights(l, w_in, norm_mix, norm_ffn, mla_cq_norm, mla_ckv_norm, mla_w_uq, mla_w_uk, mla_w_uv, mla_q_norm,
                   mla_k_norm, mla_out_norm, hg_lb_logits, hg_out_norm, gla_w_a2, gla_b_a, gla_out_norm, w_o,
                   w_ffn_in, w_ffn_out):
    wi = w_in[l]
    d = wi.shape[0]
    o = dict(cq=0, ckv=256, kr=512, hq=544, hf=800, hi=1056, hg=1312, gq=1568, gk=1696, gv=1824, gg=2080, ga=2336)
    cols = [wi[:, o["cq"]:o["cq"] + 256], wi[:, o["ckv"]:o["ckv"] + 256], wi[:, o["hq"]:o["hq"] + 256],
            wi[:, o["hf"]:o["hf"] + 256], wi[:, o["hi"]:o["hi"] + 256], wi[:, o["hg"]:o["hg"] + 256],
            wi[:, o["gq"]:o["gq"] + 128], wi[:, o["gk"]:o["gk"] + 128], wi[:, o["gv"]:o["gv"] + 256],
            wi[:, o["gg"]:o["gg"] + 256], wi[:, o["kr"]:o["kr"] + 32], wi[:, o["ga"]:o["ga"] + 16],
            jnp.zeros((d, LANES - MLA_ROPE - GLA_RANK), wi.dtype)]
    win = jnp.concatenate(cols, axis=1).astype(BF16)
    assert win.shape[1] == ZCOLS

    uq = mla_w_uq[l]
    r = uq.shape[0]
    half = MLA_ROPE // 2
    zpad = jnp.zeros((r, MLA_HEADS, HEAD_SLAB - MLA_NOPE - MLA_ROPE), uq.dtype)
    uq_a = jnp.concatenate([uq, zpad], axis=-1).reshape(r, -1)
    uq_b = jnp.concatenate([jnp.zeros((r, MLA_HEADS, MLA_NOPE), uq.dtype), uq[..., MLA_NOPE + half:],
                            uq[..., MLA_NOPE:MLA_NOPE + half], zpad], axis=-1).reshape(r, -1)
    wuq = jnp.concatenate([uq_a, uq_b], axis=1).astype(BF16)
    uk = mla_w_uk[l]
    wuk = jnp.concatenate([uk, jnp.zeros((r, MLA_HEADS, HEAD_SLAB - MLA_NOPE), uk.dtype)], axis=-1)
    wuk = wuk.reshape(r, -1).astype(BF16)
    wuv = mla_w_uv[l].reshape(r, -1).astype(BF16)
    wuvt = wuv.T

    qn, kn = mla_q_norm[l], mla_k_norm[l]
    z32 = jnp.zeros((HEAD_SLAB - MLA_NOPE - MLA_ROPE,), F32)
    qga = jnp.concatenate([qn, z32])[None]
    qgb = jnp.concatenate([jnp.zeros((MLA_NOPE,), F32), qn[MLA_NOPE + half:], qn[MLA_NOPE:MLA_NOPE + half], z32])[None]
    kgk = jnp.concatenate([kn[:MLA_NOPE], jnp.zeros((HEAD_SLAB - MLA_NOPE,), F32)])[None]
    gkr = kn[MLA_NOPE:][None]

    blk = np.zeros((HEAD_SLAB, HEAD_SLAB), np.float32)
    blk[:MLA_NOPE, :MLA_NOPE] = 1.0 / MLA_NOPE
    mk = np.kron(np.eye(2), blk)
    blk_q = blk.copy()
    blk_q[MLA_NOPE:MLA_NOPE + MLA_ROPE, MLA_NOPE:MLA_NOPE + MLA_ROPE] = 1.0 / MLA_ROPE
    mq = np.kron(np.eye(2), blk_q)
    m64 = np.kron(np.eye(4), np.full((64, 64), 1.0 / 64, np.float32))
    tile = np.zeros((MLA_ROPE, 2 * HEAD_SLAB), np.float32)
    for hd in range(2):
        tile[np.arange(MLA_ROPE), HEAD_SLAB * hd + MLA_NOPE + np.arange(MLA_ROPE)] = 1.0

    def norm_bound(g):
        return jnp.sqrt(MLA_NOPE * jnp.max(jnp.abs(g[:MLA_NOPE])) ** 2 + MLA_ROPE * jnp.max(jnp.abs(g[MLA_NOPE:])) ** 2)
    logit_bound = (MLA_SCALE * norm_bound(qn) * norm_bound(kn)).reshape(1).astype(F32)

    return dict(
        logit_bound=logit_bound, nm=norm_mix[l][None], win=win, cqn=mla_cq_norm[l][None], ckvn=mla_ckv_norm[l][None], wuq=wuq, wuk=wuk,
        wuv=wuv, wuvt=wuvt, qga=qga, qgb=qgb, mq=jnp.asarray(mq, BF16), mk=jnp.asarray(mk, BF16), kgk=kgk, gkr=gkr,
        tile=jnp.asarray(tile, BF16), lbl=hg_lb_logits, wa2=gla_w_a2[l].astype(BF16), ba=gla_b_a[l][None],
        mon=mla_out_norm[l][None], hon=jnp.tile(hg_out_norm[l], HG_HEADS)[None],
        gon=jnp.tile(gla_out_norm[l], GLA_HEADS)[None], nf=norm_ffn[l][None], m64=jnp.asarray(m64, BF16),
        wo=w_o[l].astype(BF16), wfi=w_ffn_in[l].astype(BF16), wfo=w_ffn_out[l].astype(BF16))


def _rope_tables(pos, reps):
    inv = (np.float32(ROPE_THETA) ** (-(np.arange(0, MLA_ROPE, 2, dtype=np.float32) / np.float32(MLA_ROPE)))).astype(np.float32)
    ang = pos.astype(np.float32)[:, None] * inv[None, :]
    cos, sin = np.cos(ang).astype(np.float32), np.sin(ang).astype(np.float32)
    n = pos.shape[0]
    pad = HEAD_SLAB - MLA_NOPE - MLA_ROPE
    cosq = np.concatenate([np.ones((n, MLA_NOPE), np.float32), cos, cos, np.zeros((n, pad), np.float32)], axis=1)
    sinq = np.concatenate([np.zeros((n, MLA_NOPE), np.float32), -sin, sin, np.zeros((n, pad), np.float32)], axis=1)
    cosk = np.concatenate([cos, cos], axis=1)
    sink = np.concatenate([-sin, sin], axis=1)
    tabs = dict(cosq=cosq, sinq=sinq, cosk=cosk, sink=sink)
    return {k: jnp.asarray(np.tile(v, (reps, 1))) for k, v in tabs.items()}


def _block_diag_state(state):
    b, hh, dk, dv = state.shape
    eye = jnp.eye(hh, dtype=state.dtype)
    return (state[:, :, :, None, :] * eye[None, :, None, :, None]).reshape(b, hh * dk, hh * dv)


def _head_states(s_bd, hh):
    dk, dv = s_bd.shape[1] // hh, s_bd.shape[2] // hh
    return jnp.stack([s_bd[:, hd * dk:(hd + 1) * dk, hd * dv:(hd + 1) * dv] for hd in range(hh)], axis=1)


def _group_layer(x, mods, lw, tabs, layer, bb, tt, rtt, s_hg, s_gla, attn_fn, transpose_v):
    sh1, sc1, g1, sh2, sc2, g2 = mods
    ckv, kr, q, k, v, hrec, grec, gate = _inproj_call(x, sh1, sc1, lw, tabs, bb, tt, layer, transpose_v)
    om = attn_fn(q, k, v)
    oh, shg = _recur_call(hrec, s_hg, HG_HEADS * HG_DK, rtt, "recur_hgrn")
    og, sgl = _recur_call(grec, s_gla, GLA_HEADS * GLA_DK, rtt, "recur_gla")
    y = _outffn_call(x, om, oh, og, gate, (g1, sh2, sc2, g2), lw, bb, tt)
    return y, ckv, kr, _head_states(shg, HG_HEADS), _head_states(sgl, GLA_HEADS)


def kernel(x_prompt, x_sample, c_prompt, c_sample, cache_mla_ckv, cache_mla_krope, state_hgrn, state_gla, w_mod, b_mod, norm_mix, norm_ffn, w_in, mla_cq_norm, mla_ckv_norm, mla_w_uq, mla_w_uk, mla_w_uv, mla_q_norm, mla_k_norm, mla_out_norm, hg_lb_logits, hg_out_norm, gla_w_a2, gla_b_a, gla_out_norm, w_o, w_ffn_in, w_ffn_out):
    bp, t_p, d = x_prompt.shape
    bs, t_s, _ = x_sample.shape
    depth = w_mod.shape[0]
    tm = 512
    tt_p = min(tm, t_p)
    bb_s = max(1, min(bs, tm // t_s))
    rtt_p = min(256, t_p)
    rtt_s = min(128, t_s)
    attn_tile = min(512, t_p)
    tkc = min(1024, cache_mla_ckv.shape[2])

    tabs_p = _rope_tables(np.arange(t_p), 1)
    tabs_s = _rope_tables(PAST_LEN + np.arange(t_s), bb_s)

    n_c = bp + bs
    c_rows = -(-n_c // 16) * 16
    c_pad = jnp.concatenate([c_prompt, c_sample, jnp.zeros((c_rows - n_c, d), c_prompt.dtype)], axis=0)

    xp, xs = x_prompt, x_sample
    outs = [[] for _ in range(8)]
    for l in range(depth):
        lw = _layer_weights(l, w_in, norm_mix, norm_ffn, mla_cq_norm, mla_ckv_norm, mla_w_uq, mla_w_uk, mla_w_uv,
                            mla_q_norm, mla_k_norm, mla_out_norm, hg_lb_logits, hg_out_norm, gla_w_a2, gla_b_a,
                            gla_out_norm, w_o, w_ffn_in, w_ffn_out)
        mod = _mod_call(c_pad, w_mod[l], b_mod[l][None])
        mods_p = [mod[0:bp, i * d:(i + 1) * d][:, None, :] for i in range(6)]
        mods_s = [mod[bp:n_c, i * d:(i + 1) * d][:, None, :] for i in range(6)]

        zero_hg = jnp.zeros((bp, HG_HEADS * HG_DK, HG_HEADS * HG_DV), F32)
        zero_gla = jnp.zeros((bp, GLA_HEADS * GLA_DK, GLA_HEADS * GLA_DV), F32)
        xp, a, b_, c_, d_ = _group_layer(xp, mods_p, lw, tabs_p, l, 1, tt_p, rtt_p, zero_hg, zero_gla,
                                         lambda q, k, vt: _attn_prompt_call(lw["logit_bound"], q, k, vt, attn_tile), True)
        for lst, val in zip(outs[0:4], (a, b_, c_, d_)):
            lst.append(val)
        xs, a, b_, c_, d_ = _group_layer(xs, mods_s, lw, tabs_s, l, bb_s, t_s, rtt_s,
                                         _block_diag_state(state_hgrn[l]), _block_diag_state(state_gla[l]),
                                         lambda q, k, v: _attn_sample_call(lw["logit_bound"], q, k, v, cache_mla_ckv,
                                                                           cache_mla_krope, lw, l, tkc), False)
        for lst, val in zip(outs[4:8], (a, b_, c_, d_)):
            lst.append(val)

    return (xp, xs) + tuple(jnp.stack(o) for o in outs)
```

```python
import functools

import numpy as np
import jax
import jax.numpy as jnp
from jax import lax
from jax.experimental import pallas as pl
from jax.experimental.pallas import tpu as pltpu

F32, BF16 = jnp.float32, jnp.bfloat16

D_MODEL = 1024
PAST_LEN = 4096
CHUNK = 64
EPS = 1e-6
NEG_INF = -1e30
ROPE_THETA = 10000.0
MLA_HEADS = 8
MLA_NOPE = 64
MLA_ROPE = 32
MLA_V = 64
MLA_Q_RANK = 256
MLA_KV_RANK = 256
MLA_WIDTH = MLA_HEADS * MLA_V
MLA_SCALE = (MLA_NOPE + MLA_ROPE) ** -0.5
LOG2E = 1.4426950408889634
HG_HEADS = 4
HG_DK = 64
HG_DV = 64
GLA_HEADS = 4
GLA_DK = 32
GLA_DV = 64
GLA_RANK = 16
GLA_TAU = 16.0
D_FF = 2816

LANES = 128
HEAD_SLAB = 128
SUB = 16
MAX_CHUNK_DECAY_LOG2 = 110.0
ZCOLS = 2432
VMEM_LIMIT = 56 * 1024 * 1024

_Z = dict(cq=0, ckv=256, hq=512, hf=768, hi=1024, hg=1280, gq=1536, gk=1664, gv=1792, gg=2048, misc=2304)


def _bdot(a, b):
    return jnp.dot(a.astype(BF16), b.astype(BF16), preferred_element_type=F32)


def _dot_nt(a, b):
    return lax.dot_general(a, b, (((1,), (1,)), ((), ())), preferred_element_type=F32)


def _dot_tn(a, b):
    return lax.dot_general(a, b, (((0,), (0,)), ((), ())), preferred_element_type=F32)


def _sigmoid(x):
    return 1.0 / (1.0 + jnp.exp(-x))


def _silu(x):
    return x * _sigmoid(x)


def _log_sigmoid(x):
    return jnp.minimum(x, 0.0) - jnp.log1p(jnp.exp(-jnp.abs(x)))


def _rms(x):
    return x * lax.rsqrt(jnp.mean(x * x, axis=-1, keepdims=True) + EPS)


def _params(sem):
    return pltpu.CompilerParams(dimension_semantics=sem, vmem_limit_bytes=VMEM_LIMIT)


def _const_spec(shape):
    nd = len(shape)
    return pl.BlockSpec(shape, lambda *_: (0,) * nd)


def _mod_body(c_ref, w_ref, b_ref, o_ref):
    o_ref[...] = _bdot(_silu(c_ref[...]), w_ref[0]) + b_ref[0]


def _mod_call(c_pad, w_mod, b_mod, layer):
    rows, d = c_pad.shape
    n = w_mod.shape[2]
    tn = 1024
    return pl.pallas_call(
        _mod_body,
        grid=(n // tn,),
        in_specs=[pl.BlockSpec((rows, d), lambda j: (0, 0)),
                  pl.BlockSpec((1, d, tn), lambda j: (layer, 0, j)),
                  pl.BlockSpec((1, 1, tn), lambda j: (layer, 0, j))],
        out_specs=pl.BlockSpec((rows, tn), lambda j: (0, j)),
        out_shape=jax.ShapeDtypeStruct((rows, n), F32),
        compiler_params=_params(("arbitrary",)),
        name="mod",
    )(c_pad, w_mod, b_mod)


def _mla_kv(ckvn_bf, kro_bf, w_uk_ref, w_uv_ref, mk_ref, kgk_ref, tile_ref, transpose_v=False, kro_transposed=False):
    place = _dot_tn if kro_transposed else functools.partial(jnp.dot, preferred_element_type=F32)
    krt = place(kro_bf, tile_ref[...])
    kgk2 = jnp.concatenate([kgk_ref[...], kgk_ref[...]], axis=-1)
    parts = []
    for hp in range(MLA_HEADS // 2):
        sl = slice(2 * HEAD_SLAB * hp, 2 * HEAD_SLAB * (hp + 1))
        sk = jnp.dot(ckvn_bf, w_uk_ref[:, sl], preferred_element_type=F32)
        msk = jnp.dot((sk * sk).astype(BF16), mk_ref[...], preferred_element_type=F32)
        parts.append((sk * lax.rsqrt(msk + EPS) * kgk2 + krt).astype(BF16))
    k = jnp.concatenate(parts, axis=-1)
    if transpose_v:
        v = _dot_nt(w_uv_ref[...], ckvn_bf).astype(BF16)
    else:
        v = jnp.dot(ckvn_bf, w_uv_ref[...], preferred_element_type=F32).astype(BF16)
    return k, v


def _inproj_body(x_ref, sh_ref, sc_ref, nm_ref, win_ref, cqn_ref, ckvn_ref, wuq_ref, wuk_ref, wuv_ref,
                 qga_ref, qgb_ref, cosq_ref, sinq_ref, mq_ref, mk_ref, kgk_ref, gkr_ref, cosk_ref, sink_ref,
                 tile_ref, lbl_ref, wa2_ref, ba_ref,
                 ckv_o, kr_o, q_o, k_o, v_o, hrec_o, grec_o, gate_o, *, layer, transpose_v):
    bb, tt, d = x_ref.shape
    tm = bb * tt
    x = x_ref[...]
    h = _rms(x) * nm_ref[...] * (1.0 + sc_ref[...]) + sh_ref[...]
    z = jnp.dot(h.reshape(tm, d).astype(BF16), win_ref[...], preferred_element_type=F32)

    def seg(name, w):
        return z[:, _Z[name]:_Z[name] + w]

    cqn = _rms(seg("cq", MLA_Q_RANK)) * cqn_ref[...]
    ckvn = _rms(seg("ckv", MLA_KV_RANK)) * ckvn_ref[...]
    ckv_o[...] = ckvn.reshape(bb, tt, MLA_KV_RANK)
    misc = seg("misc", LANES)

    qab = jnp.dot(cqn.astype(BF16), wuq_ref[...], preferred_element_type=F32)
    width = MLA_HEADS * HEAD_SLAB
    ca = cosq_ref[...] * qga_ref[...]
    sb = sinq_ref[...] * qgb_ref[...]
    ca2 = jnp.concatenate([ca, ca], axis=-1)
    sb2 = jnp.concatenate([sb, sb], axis=-1)
    for hp in range(MLA_HEADS // 2):
        sl = slice(2 * HEAD_SLAB * hp, 2 * HEAD_SLAB * (hp + 1))
        qa = qab[:, sl]
        qb = qab[:, width + sl.start:width + sl.stop]
        msq = jnp.dot((qa * qa).astype(BF16), mq_ref[...], preferred_element_type=F32)
        qo = lax.rsqrt(msq + EPS) * (qa * ca2 + qb * sb2) * (MLA_SCALE * LOG2E)
        q_o[:, :, sl] = qo.astype(BF16).reshape(bb, tt, 2 * HEAD_SLAB)

    kr = misc[:, 0:MLA_ROPE]
    krn = _rms(kr) * gkr_ref[...]
    half = MLA_ROPE // 2
    swapped = jnp.concatenate([krn[:, half:], krn[:, :half]], axis=-1)
    kro = krn * cosk_ref[...] + swapped * sink_ref[...]
    kr_o[...] = kro.reshape(bb, tt, MLA_ROPE)
    k, v = _mla_kv(ckvn.astype(BF16), kro.astype(BF16), wuk_ref, wuv_ref, mk_ref, kgk_ref, tile_ref, transpose_v)
    k_o[...] = k.reshape(bb, tt, width)
    if transpose_v:
        v_o[0] = v
    else:
        v_o[...] = v.reshape(bb, tt, MLA_WIDTH)

    lg = lbl_ref[...]
    e = jnp.exp(lg - jnp.max(lg, axis=0, keepdims=True))
    p = e / jnp.sum(e, axis=0, keepdims=True)
    lb = jnp.sum(p[0:layer + 1], axis=0, keepdims=True) - p[0:1]
    f = lb + (1.0 - lb) * _sigmoid(seg("hf", 256))
    hrec_o[:, :, 0:256] = seg("hq", 256).reshape(bb, tt, 256)
    hrec_o[:, :, 256:512] = (1.0 - f).reshape(bb, tt, 256)
    hrec_o[:, :, 512:768] = jnp.log(f).reshape(bb, tt, 256)
    hrec_o[:, :, 768:1024] = seg("hi", 256).reshape(bb, tt, 256)

    ga = misc[:, MLA_ROPE:MLA_ROPE + GLA_RANK]
    apre = jnp.dot(ga.astype(BF16), wa2_ref[...], preferred_element_type=F32) + ba_ref[...]
    grec_o[:, :, 0:128] = (seg("gq", 128) * (GLA_DK ** -0.5)).reshape(bb, tt, 128)
    grec_o[:, :, 128:256] = seg("gk", 128).reshape(bb, tt, 128)
    grec_o[:, :, 256:384] = (_log_sigmoid(apre) / GLA_TAU).reshape(bb, tt, 128)
    grec_o[:, :, 384:640] = seg("gv", 256).reshape(bb, tt, 256)

    gate_o[:, :, 0:256] = _silu(seg("hg", 256)).reshape(bb, tt, 256)
    gate_o[:, :, 256:512] = _silu(seg("gg", 256)).reshape(bb, tt, 256)


def _inproj_call(x, sh, sc, lw, tabs, bb, tt, layer, transpose_v):
    b, t, d = x.shape
    tm = bb * tt
    grid = (b // bb, t // tt)
    tok = lambda w: pl.BlockSpec((bb, tt, w), lambda i, j: (i, j, 0))
    modspec = pl.BlockSpec((bb, 1, d), lambda i, j: (i, 0, 0))
    tab = lambda w: pl.BlockSpec((tm, w), lambda i, j: (j, 0))
    assert bb == 1 or not transpose_v
    consts = [lw["nm"], lw["win"], lw["cqn"], lw["ckvn"], lw["wuq"], lw["wuk"], lw["wuvt" if transpose_v else "wuv"],
              lw["qga"], lw["qgb"]]
    consts2 = [lw["mq"], lw["mk"], lw["kgk"], lw["gkr"]]
    consts3 = [lw["tile"], lw["lbl"], lw["wa2"], lw["ba"]]
    in_specs = ([tok(d), modspec, modspec] + [_const_spec(a.shape) for a in consts]
                + [tab(LANES), tab(LANES)] + [_const_spec(a.shape) for a in consts2]
                + [tab(MLA_ROPE), tab(MLA_ROPE)] + [_const_spec(a.shape) for a in consts3])
    widths = [(MLA_KV_RANK, F32), (MLA_ROPE, F32), (MLA_HEADS * HEAD_SLAB, BF16), (MLA_HEADS * HEAD_SLAB, BF16),
              (MLA_WIDTH, BF16), (1024, F32), (640, F32), (512, F32)]
    out_specs = [tok(w) for w, _ in widths]
    out_shape = [jax.ShapeDtypeStruct((b, t, w), dt) for w, dt in widths]
    if transpose_v:
        out_specs[4] = pl.BlockSpec((1, MLA_WIDTH, tt), lambda i, j: (i, 0, j))
        out_shape[4] = jax.ShapeDtypeStruct((b, MLA_WIDTH, t), BF16)
    return pl.pallas_call(
        functools.partial(_inproj_body, layer=layer, transpose_v=transpose_v),
        grid=grid,
        in_specs=in_specs,
        out_specs=out_specs,
        out_shape=out_shape,
        compiler_params=_params(("arbitrary", "arbitrary")),
        name="inproj",
    )(x, sh, sc, *consts, tabs["cosq"], tabs["sinq"], *consts2, tabs["cosk"], tabs["sink"], *consts3)


MAX_UNSHIFTED_LOGIT = 60.0
SMALL_SCORE_TILE = 64 * 1024


def _attend_bounded(q, k, v, lf_ref, accf_ref, bias):
    tq, tk = q.shape[0], k.shape[0]

    def scores(hd):
        sl = slice(HEAD_SLAB * hd, HEAD_SLAB * (hd + 1))
        s = _dot_nt(q[:, sl], k[:, sl])
        return s if bias is None else s + bias

    def accumulate(hd, s):
        p = jnp.exp2(s)
        if tk % LANES == 0:
            part = p[:, 0:LANES]
            for c in range(1, tk // LANES):
                part = part + p[:, LANES * c:LANES * (c + 1)]
        else:
            lane0 = lax.broadcasted_iota(jnp.int32, (tq, LANES), 1) == 0
            part = jnp.where(lane0, jnp.sum(p, axis=-1, keepdims=True), 0.0)
        lf_ref[hd] += part
        hp = hd // 2
        accf_ref[hd] += jnp.dot(p.astype(BF16), v[:, LANES * hp:LANES * (hp + 1)], preferred_element_type=F32)

    if tq * tk <= SMALL_SCORE_TILE:
        ss = [scores(hd) for hd in range(MLA_HEADS)]
        for hd in range(MLA_HEADS):
            accumulate(hd, ss[hd])
    else:
        for hd in range(MLA_HEADS):
            accumulate(hd, scores(hd))


def _finish_bounded(lf_ref, accf_ref):
    tq = accf_ref.shape[1]
    low = lax.broadcasted_iota(jnp.int32, (tq, LANES), 1) < MLA_V
    outs = []
    for hp in range(MLA_HEADS // 2):
        la = jnp.sum(lf_ref[2 * hp], axis=-1, keepdims=True)
        lb = jnp.sum(lf_ref[2 * hp + 1], axis=-1, keepdims=True)
        outs.append(jnp.where(low, accf_ref[2 * hp] / la, accf_ref[2 * hp + 1] / lb))
    return jnp.concatenate(outs, axis=-1)


def _attend_online(q, k, v, m_ref, l_ref, acc_ref, bias):
    tq = q.shape[0]
    low = lax.broadcasted_iota(jnp.int32, (tq, LANES), 1) < MLA_V
    for hp in range(MLA_HEADS // 2):
        vp = v[:, LANES * hp:LANES * (hp + 1)]
        pvs, alphas = [], []
        for e in range(2):
            hd = 2 * hp + e
            sl = slice(HEAD_SLAB * hd, HEAD_SLAB * (hd + 1))
            s = _dot_nt(q[:, sl], k[:, sl])
            if bias is not None:
                s = s + bias
            m_prev = m_ref[hd]
            m_new = jnp.maximum(m_prev, jnp.max(s, axis=-1, keepdims=True))
            p = jnp.exp2(s - m_new)
            alpha = jnp.exp2(m_prev - m_new)
            l_ref[hd] = alpha * l_ref[hd] + jnp.sum(p, axis=-1, keepdims=True)
            m_ref[hd] = m_new
            pvs.append(jnp.dot(p.astype(BF16), vp, preferred_element_type=F32))
            alphas.append(alpha)
        sl2 = slice(LANES * hp, LANES * (hp + 1))
        acc_ref[:, sl2] = jnp.where(low, alphas[0], alphas[1]) * acc_ref[:, sl2] + jnp.where(low, pvs[0], pvs[1])


def _finish_online(l_ref, acc_ref):
    tq = acc_ref.shape[0]
    low = lax.broadcasted_iota(jnp.int32, (tq, LANES), 1) < MLA_V
    outs = []
    for hp in range(MLA_HEADS // 2):
        l2 = jnp.where(low, l_ref[2 * hp], l_ref[2 * hp + 1])
        outs.append(acc_ref[:, LANES * hp:LANES * (hp + 1)] / l2)
    return jnp.concatenate(outs, axis=-1)


def _softmax_scratch(tq):
    return [pltpu.VMEM((MLA_HEADS, tq, 1), F32), pltpu.VMEM((MLA_HEADS, tq, 1), F32), pltpu.VMEM((tq, MLA_WIDTH), F32),
            pltpu.VMEM((MLA_HEADS, tq, LANES), F32), pltpu.VMEM((MLA_HEADS, tq, LANES), F32)]


def _softmax_init(m_ref, l_ref, acc_ref, lf_ref, accf_ref):
    m_ref[...] = jnp.full(m_ref.shape, NEG_INF, F32)
    for ref in (l_ref, acc_ref, lf_ref, accf_ref):
        ref[...] = jnp.zeros(ref.shape, F32)


def _attend(bounded, cond, get_qkv, scratch, bias_ref):
    m_ref, l_ref, acc_ref, lf_ref, accf_ref = scratch

    @pl.when(jnp.logical_and(cond, bounded))
    def _():
        _attend_bounded(*get_qkv(), lf_ref, accf_ref, None if bias_ref is None else bias_ref[...])

    @pl.when(jnp.logical_and(cond, jnp.logical_not(bounded)))
    def _():
        _attend_online(*get_qkv(), m_ref, l_ref, acc_ref, None if bias_ref is None else bias_ref[...])


def _softmax_finish(bounded, cond, o_ref, scratch):
    m_ref, l_ref, acc_ref, lf_ref, accf_ref = scratch

    @pl.when(jnp.logical_and(cond, bounded))
    def _():
        o_ref[0] = _finish_bounded(lf_ref, accf_ref)

    @pl.when(jnp.logical_and(cond, jnp.logical_not(bounded)))
    def _():
        o_ref[0] = _finish_online(l_ref, acc_ref)


def _attend_t_bounded(q, k, vt, lf_ref, acc_ref, bias_t):
    tq, tk = q.shape[0], k.shape[0]

    def scores_t(hd):
        sl = slice(HEAD_SLAB * hd, HEAD_SLAB * (hd + 1))
        st = _dot_nt(k[:, sl], q[:, sl])
        return st if bias_t is None else st + bias_t

    st_next = scores_t(0)
    for hd in range(MLA_HEADS):
        st = st_next
        if hd + 1 < MLA_HEADS:
            st_next = scores_t(hd + 1)
        pt = jnp.exp2(st)
        lf_ref[hd] += jnp.sum(pt.reshape(tk // 8, 8, tq), axis=0)
        rows = slice(MLA_V * hd, MLA_V * (hd + 1))
        acc_ref[rows, :] += jnp.dot(vt[rows, :], pt.astype(BF16), preferred_element_type=F32)


def _attend_t_online(q, k, vt, m_ref, l_ref, acc_ref, bias_t):
    for hd in range(MLA_HEADS):
        sl = slice(HEAD_SLAB * hd, HEAD_SLAB * (hd + 1))
        st = _dot_nt(k[:, sl], q[:, sl])
        if bias_t is not None:
            st = st + bias_t
        m_prev = m_ref[hd]
        m_new = jnp.maximum(m_prev, jnp.max(st, axis=0, keepdims=True))
        pt = jnp.exp2(st - m_new)
        alpha = jnp.exp2(m_prev - m_new)
        l_ref[hd] = alpha * l_ref[hd] + jnp.sum(pt, axis=0, keepdims=True)
        m_ref[hd] = m_new
        rows = slice(MLA_V * hd, MLA_V * (hd + 1))
        acc_ref[rows, :] = alpha * acc_ref[rows, :] + jnp.dot(vt[rows, :], pt.astype(BF16), preferred_element_type=F32)


def _finish_t(l_of_head, acc_ref):
    outs = []
    for hd in range(MLA_HEADS):
        outs.append(acc_ref[MLA_V * hd:MLA_V * (hd + 1), :] / l_of_head(hd))
    return jnp.concatenate(outs, axis=0).T


def _attn_prompt_body(qi_ref, ki_ref, bound_ref, bias_ref, q_ref, k_ref, vt_ref, o_ref, m_ref, l_ref, lf_ref, acc_ref):
    step = pl.program_id(1)
    i = qi_ref[step]
    j = ki_ref[step]
    bounded = bound_ref[0] <= MAX_UNSHIFTED_LOGIT
    unbounded = jnp.logical_not(bounded)
    diag = j == i

    @pl.when(j == 0)
    def _():
        m_ref[...] = jnp.full(m_ref.shape, NEG_INF, F32)
        for ref in (l_ref, lf_ref, acc_ref):
            ref[...] = jnp.zeros(ref.shape, F32)

    for on_diag in (False, True):
        cond = diag if on_diag else j < i

        @pl.when(jnp.logical_and(cond, bounded))
        def _():
            _attend_t_bounded(q_ref[0], k_ref[0], vt_ref[0], lf_ref, acc_ref, bias_ref[...] if on_diag else None)

        @pl.when(jnp.logical_and(cond, unbounded))
        def _():
            _attend_t_online(q_ref[0], k_ref[0], vt_ref[0], m_ref, l_ref, acc_ref, bias_ref[...] if on_diag else None)

    @pl.when(jnp.logical_and(diag, bounded))
    def _():
        o_ref[0] = _finish_t(lambda hd: jnp.sum(lf_ref[hd], axis=0, keepdims=True), acc_ref)

    @pl.when(jnp.logical_and(diag, unbounded))
    def _():
        o_ref[0] = _finish_t(lambda hd: l_ref[hd], acc_ref)


def _attn_prompt_call(bound, q, k, vt, tile):
    b, t, _ = q.shape
    n = t // tile
    pairs = [(i, j) for i in range(n) for j in range(i + 1)]
    qi = jnp.asarray([p[0] for p in pairs], jnp.int32)
    ki = jnp.asarray([p[1] for p in pairs], jnp.int32)
    q_map = lambda bi, st, qi_ref, ki_ref: (bi, qi_ref[st], 0)
    k_map = lambda bi, st, qi_ref, ki_ref: (bi, ki_ref[st], 0)
    vt_map = lambda bi, st, qi_ref, ki_ref: (bi, 0, ki_ref[st])
    chunk_of = np.arange(tile) // CHUNK
    bias_t = jnp.asarray(np.where(chunk_of[:, None] <= chunk_of[None, :], 0.0, NEG_INF), F32)
    grid_spec = pltpu.PrefetchScalarGridSpec(
        num_scalar_prefetch=2,
        grid=(b, len(pairs)),
        in_specs=[pl.BlockSpec(memory_space=pltpu.SMEM), pl.BlockSpec((tile, tile), lambda *_: (0, 0)),
                  pl.BlockSpec((1, tile, q.shape[2]), q_map),
                  pl.BlockSpec((1, tile, k.shape[2]), k_map),
                  pl.BlockSpec((1, MLA_WIDTH, tile), vt_map)],
        out_specs=pl.BlockSpec((1, tile, MLA_WIDTH), q_map),
        scratch_shapes=[pltpu.VMEM((MLA_HEADS, 1, tile), F32), pltpu.VMEM((MLA_HEADS, 1, tile), F32),
                        pltpu.VMEM((MLA_HEADS, 8, tile), F32), pltpu.VMEM((MLA_WIDTH, tile), F32)])
    return pl.pallas_call(
        _attn_prompt_body,
        grid_spec=grid_spec,
        out_shape=jax.ShapeDtypeStruct((b, t, MLA_WIDTH), F32),
        compiler_params=_params(("arbitrary", "arbitrary")),
        name="attn_prompt",
    )(qi, ki, bound, bias_t, q, k, vt)


def _attn_sample_body(bound_ref, q_ref, kn_ref, vn_ref, ckv_ref, kr_ref, wuk_ref, wuv_ref, mk_ref, kgk_ref, tile_ref,
                      o_ref, *scratch):
    j = pl.program_id(1)
    bounded = bound_ref[0] <= MAX_UNSHIFTED_LOGIT
    always = j >= 0

    @pl.when(j == 0)
    def _():
        _softmax_init(*scratch)

    _attend(bounded, j == 0, lambda: (q_ref[0], kn_ref[0], vn_ref[0]), scratch, None)

    def past_qkv():
        k, v = _mla_kv(ckv_ref[0, 0].astype(BF16), kr_ref[0, 0].astype(BF16), wuk_ref, wuv_ref, mk_ref, kgk_ref,
                       tile_ref, kro_transposed=True)
        return q_ref[0], k, v

    _attend(bounded, always, past_qkv, scratch, None)
    _softmax_finish(bounded, j == pl.num_programs(1) - 1, o_ref, scratch)


def _attn_sample_call(bound, q, k_new, v_new, cache_ckv, cache_kr_t, lw, layer, tkc):
    b, t, _ = q.shape
    past = cache_ckv.shape[2]
    consts = [lw["wuk"], lw["wuv"], lw["mk"], lw["kgk"], lw["tile"]]
    new = lambda w: pl.BlockSpec((1, t, w), lambda bi, j: (bi, 0, 0))
    return pl.pallas_call(
        _attn_sample_body,
        grid=(b, past // tkc),
        in_specs=[pl.BlockSpec(memory_space=pltpu.SMEM), new(q.shape[2]), new(k_new.shape[2]), new(v_new.shape[2]),
                  pl.BlockSpec((1, 1, tkc, MLA_KV_RANK), lambda bi, j: (layer, bi, j, 0)),
                  pl.BlockSpec((1, 1, MLA_ROPE, tkc), lambda bi, j: (layer, bi, 0, j))]
                 + [_const_spec(a.shape) for a in consts],
        out_specs=new(MLA_WIDTH),
        out_shape=jax.ShapeDtypeStruct((b, t, MLA_WIDTH), F32),
        scratch_shapes=_softmax_scratch(t),
        compiler_params=_params(("arbitrary", "arbitrary")),
        name="attn_sample",
    )(bound, q, k_new, v_new, cache_ckv, cache_kr_t, *consts)


def _cumsum_matrix(tt, blk):
    r = np.arange(tt)
    return (((r[:, None] // blk) == (r[None, :] // blk)) & (r[None, :] <= r[:, None])).astype(np.float32)


def _block_cumsum(g, lmat_ref, blk):
    tt, c = g.shape
    lmat = lmat_ref[...]
    g1 = g.astype(BF16)
    r1 = g - g1.astype(F32)
    g2 = r1.astype(BF16)
    g3 = (r1 - g2.astype(F32)).astype(BF16)
    bb = jnp.dot(lmat, jnp.concatenate([g1, g2, g3], axis=-1), preferred_element_type=F32)
    b = (bb[:, 2 * c:3 * c] + bb[:, c:2 * c]) + bb[:, 0:c]
    last = [jnp.broadcast_to(b[blk * (n + 1) - 1:blk * (n + 1)], (blk, c)) for n in range(tt // blk)]
    return b, jnp.concatenate(last, axis=0)


def _state_chain(blk, qe, ke, vb, dec, o_in_block, bd_ref, o_ref, s_ref):
    tt = qe.shape[0]
    nb = tt // blk
    picked = [dec[blk * n:blk * n + 1] for n in range(nb)]
    if nb % 8:
        picked.append(jnp.zeros((8 - nb % 8, dec.shape[1]), F32))
    dsel = jnp.concatenate(picked, axis=0).T
    us = [_dot_tn(ke[blk * n:blk * (n + 1)], vb[blk * n:blk * (n + 1)]) * bd_ref[...] for n in range(nb)]
    s = s_ref[...]
    for n in range(nb):
        rows = slice(blk * n, blk * (n + 1))
        o_ref[0, rows, :] = jnp.dot(qe[rows], s.astype(BF16), preferred_element_type=F32) + o_in_block(n)
        s = dsel[:, n:n + 1] * s + us[n]
    s_ref[...] = s


def _recur_factorised(q, k, v, b2, bt2, bd_ref, o_ref, s_ref, hk):
    heads = 4
    dk = hk // heads
    tt = q.shape[0]
    qe = (q * jnp.exp2(b2)).astype(BF16)
    kh = (k * jnp.exp2(-b2)).astype(BF16)
    ke = (k * jnp.exp2(bt2 - b2)).astype(BF16)
    dec = jnp.exp2(bt2)
    vb = v.astype(BF16)
    rows_k = lax.broadcasted_iota(jnp.int32, (heads * CHUNK, hk), 0) // CHUNK
    same_k = rows_k == lax.broadcasted_iota(jnp.int32, (heads * CHUNK, hk), 1) // dk
    rows_v = lax.broadcasted_iota(jnp.int32, (heads * CHUNK, 256), 0) // CHUNK
    same_v = rows_v == lax.broadcasted_iota(jnp.int32, (heads * CHUNK, 256), 1) // 64
    ii = lax.broadcasted_iota(jnp.int32, (CHUNK, heads * CHUNK), 0)
    jj = lax.broadcasted_iota(jnp.int32, (CHUNK, heads * CHUNK), 1) % CHUNK
    causal = jj <= ii
    chunks = [slice(CHUNK * n, CHUNK * (n + 1)) for n in range(tt // CHUNK)]
    scs = []
    for rows in chunks:
        kbd = jnp.where(same_k, jnp.concatenate([kh[rows]] * heads, axis=0), jnp.zeros(same_k.shape, BF16))
        scs.append(_dot_nt(qe[rows], kbd))
    ods = []
    for rows, sc in zip(chunks, scs):
        vbd = jnp.where(same_v, jnp.concatenate([vb[rows]] * heads, axis=0), jnp.zeros(same_v.shape, BF16))
        ods.append(jnp.dot(jnp.where(causal, sc, 0.0).astype(BF16), vbd, preferred_element_type=F32))
    _state_chain(CHUNK, qe, ke, vb, dec, lambda n: ods[n], bd_ref, o_ref, s_ref)


def _recur_pairwise(q, k, g, v, lsub_ref, e2_ref, sel_ref, bd_ref, o_ref, s_ref):
    tt = q.shape[0]
    nb = tt // SUB
    b, btot = _block_cumsum(g, lsub_ref, SUB)
    b2 = b * LOG2E
    bt2 = btot * LOG2E
    qe = (q * jnp.exp2(b2)).astype(BF16)
    ke = (k * jnp.exp2(bt2 - b2)).astype(BF16)
    dec = jnp.exp2(bt2)
    vb = v.astype(BF16)
    xs = []
    for n in range(nb):
        r0 = SUB * n
        bj, kj = b2[r0:r0 + SUB], k[r0:r0 + SUB]
        for i0 in range(SUB):
            bi = b2[r0 + i0:r0 + i0 + 1]
            qi = q[r0 + i0:r0 + i0 + 1]
            xs.append((jnp.exp2(jnp.minimum(bi - bj, 0.0)) * (kj * qi)).astype(BF16))
    x = jnp.concatenate(xs, axis=0)
    r = jnp.dot(x, e2_ref[...], preferred_element_type=F32)
    y = (r.reshape(nb, SUB, SUB, 256) * v.reshape(nb, 1, SUB, 256)).reshape(nb * SUB * SUB, 256).astype(BF16)
    ods = [jnp.dot(sel_ref[...], y[SUB * SUB * n:SUB * SUB * (n + 1)], preferred_element_type=F32) for n in range(nb)]
    _state_chain(SUB, qe, ke, vb, dec, lambda n: ods[n], bd_ref, o_ref, s_ref)


def _recur_body(rec_ref, s0_ref, lchunk_ref, lsub_ref, e2_ref, sel_ref, bd_ref, o_ref, s_o, s_ref, *, hk):
    t = pl.program_id(1)

    @pl.when(t == 0)
    def _():
        s_ref[...] = s0_ref[0]

    rec = rec_ref[0]
    q = rec[:, 0:hk]
    k = rec[:, hk:2 * hk]
    g = rec[:, 2 * hk:3 * hk]
    v = rec[:, 3 * hk:3 * hk + 256]

    b, btot = _block_cumsum(g, lchunk_ref, min(CHUNK, g.shape[0]))
    b2 = b * LOG2E
    factorised = jnp.min(b2) >= -MAX_CHUNK_DECAY_LOG2

    @pl.when(factorised)
    def _():
        _recur_factorised(q, k, v, b2, btot * LOG2E, bd_ref, o_ref, s_ref, hk)

    @pl.when(jnp.logical_not(factorised))
    def _():
        _recur_pairwise(q, k, g, v, lsub_ref, e2_ref, sel_ref, bd_ref, o_ref, s_ref)

    @pl.when(t == pl.num_programs(1) - 1)
    def _():
        s_o[0] = s_ref[...]


def _recur_call(rec, s0, hk, tt, name):
    b, t, _ = rec.shape
    w = 3 * hk + 256
    heads = 4
    dk = hk // heads
    e2 = np.kron(np.eye(heads), np.ones((dk, 64))).astype(np.float32)
    sel = np.zeros((SUB, SUB * SUB), np.float32)
    for i in range(SUB):
        sel[i, SUB * i:SUB * i + i + 1] = 1.0
    return pl.pallas_call(
        functools.partial(_recur_body, hk=hk),
        grid=(b, t // tt),
        in_specs=[pl.BlockSpec((1, tt, w), lambda bi, ti: (bi, ti, 0)),
                  pl.BlockSpec((1, hk, 256), lambda bi, ti: (bi, 0, 0)),
                  _const_spec((tt, tt)), _const_spec((tt, tt)),
                  _const_spec((hk, 256)), _const_spec((SUB, SUB * SUB)), _const_spec((hk, 256))],
        out_specs=[pl.BlockSpec((1, tt, 256), lambda bi, ti: (bi, ti, 0)),
                   pl.BlockSpec((1, hk, 256), lambda bi, ti: (bi, 0, 0))],
        out_shape=[jax.ShapeDtypeStruct((b, t, 256), F32), jax.ShapeDtypeStruct((b, hk, 256), F32)],
        scratch_shapes=[pltpu.VMEM((hk, 256), F32)],
        compiler_params=_params(("arbitrary", "arbitrary")),
        name=name,
    )(rec, s0, jnp.asarray(_cumsum_matrix(tt, min(CHUNK, tt)), BF16), jnp.asarray(_cumsum_matrix(tt, SUB), BF16),
      jnp.asarray(e2, BF16), jnp.asarray(sel, BF16), jnp.asarray(e2, F32))


def _outffn_body(x_ref, om_ref, oh_ref, og_ref, gate_ref, g1_ref, sh2_ref, sc2_ref, g2_ref,
                 mon_ref, hon_ref, gon_ref, nf_ref, m64_ref, wo_ref, wfi_ref, wfo_ref, y_ref, *, ff_chunk):
    bb, tt, d = x_ref.shape
    tm = bb * tt
    om = om_ref[...].reshape(tm, MLA_WIDTH)
    omn = _rms(om) * mon_ref[...]

    def headnorm(o, gain_ref, gate):
        ms = jnp.dot((o * o).astype(BF16), m64_ref[...], preferred_element_type=F32)
        return o * lax.rsqrt(ms + EPS) * gain_ref[...] * gate

    gate = gate_ref[...].reshape(tm, 512)
    ohn = headnorm(oh_ref[...].reshape(tm, 256), hon_ref, gate[:, 0:256])
    ogn = headnorm(og_ref[...].reshape(tm, 256), gon_ref, gate[:, 256:512])
    mix_in = jnp.concatenate([omn, ohn, ogn], axis=-1).astype(BF16)
    mix = jnp.dot(mix_in, wo_ref[...], preferred_element_type=F32)
    x1 = x_ref[...] + g1_ref[...] * mix.reshape(bb, tt, d)
    h2 = (_rms(x1) * nf_ref[...] * (1.0 + sc2_ref[...]) + sh2_ref[...]).reshape(tm, d).astype(BF16)
    acc = jnp.zeros((tm, d), F32)
    for c in range(D_FF // ff_chunk):
        a = jnp.dot(h2, wfi_ref[:, ff_chunk * c:ff_chunk * (c + 1)], preferred_element_type=F32)
        u = jnp.dot(h2, wfi_ref[:, D_FF + ff_chunk * c:D_FF + ff_chunk * (c + 1)], preferred_element_type=F32)
        act = (_silu(a) * u).astype(BF16)
        acc = acc + jnp.dot(act, wfo_ref[ff_chunk * c:ff_chunk * (c + 1), :], preferred_element_type=F32)
    y_ref[...] = x1 + g2_ref[...] * acc.reshape(bb, tt, d)


def _outffn_call(x, om, oh, og, gate, mods, lw, bb, tt):
    b, t, d = x.shape
    tok = lambda w: pl.BlockSpec((bb, tt, w), lambda i, j: (i, j, 0))
    modspec = pl.BlockSpec((bb, 1, d), lambda i, j: (i, 0, 0))
    consts = [lw["mon"], lw["hon"], lw["gon"], lw["nf"], lw["m64"], lw["wo"], lw["wfi"], lw["wfo"]]
    single = lambda a: pl.BlockSpec(a.shape, lambda *_: (0,) * a.ndim, pipeline_mode=pl.Buffered(1))
    return pl.pallas_call(
        functools.partial(_outffn_body, ff_chunk=256),
        grid=(b // bb, t // tt),
        in_specs=[tok(d), tok(MLA_WIDTH), tok(256), tok(256), tok(512)] + [modspec] * 4 + [single(a) for a in consts],
        out_specs=tok(d),
        out_shape=jax.ShapeDtypeStruct((b, t, d), F32),
        compiler_params=_params(("arbitrary", "arbitrary")),
        name="outffn",
    )(x, om, oh, og, gate, *mods, *consts)


def _layer_weights(l, w_in, norm_mix, norm_ffn, mla_cq_norm, mla_ckv_norm, mla_w_uq, mla_w_uk, mla_w_uv, mla_q_norm,
                   mla_k_norm, mla_out_norm, hg_lb_logits, hg_out_norm, gla_w_a2, gla_b_a, gla_out_norm, w_o,
                   w_ffn_in, w_ffn_out):
    wi = w_in[l]
    d = wi.shape[0]
    o = dict(cq=0, ckv=256, kr=512, hq=544, hf=800, hi=1056, hg=1312, gq=1568, gk=1696, gv=1824, gg=2080, ga=2336)
    cols = [wi[:, o["cq"]:o["cq"] + 256], wi[:, o["ckv"]:o["ckv"] + 256], wi[:, o["hq"]:o["hq"] + 256],
            wi[:, o["hf"]:o["hf"] + 256], wi[:, o["hi"]:o["hi"] + 256], wi[:, o["hg"]:o["hg"] + 256],
            wi[:, o["gq"]:o["gq"] + 128], wi[:, o["gk"]:o["gk"] + 128], wi[:, o["gv"]:o["gv"] + 256],
            wi[:, o["gg"]:o["gg"] + 256], wi[:, o["kr"]:o["kr"] + 32], wi[:, o["ga"]:o["ga"] + 16],
            jnp.zeros((d, LANES - MLA_ROPE - GLA_RANK), wi.dtype)]
    win = jnp.concatenate(cols, axis=1).astype(BF16)
    assert win.shape[1] == ZCOLS

    uq = mla_w_uq[l]
    r = uq.shape[0]
    half = MLA_ROPE // 2
    zpad = jnp.zeros((r, MLA_HEADS, HEAD_SLAB - MLA_NOPE - MLA_ROPE), uq.dtype)
    uq_a = jnp.concatenate([uq, zpad], axis=-1).reshape(r, -1)
    uq_b = jnp.concatenate([jnp.zeros((r, MLA_HEADS, MLA_NOPE), uq.dtype), uq[..., MLA_NOPE + half:],
                            uq[..., MLA_NOPE:MLA_NOPE + half], zpad], axis=-1).reshape(r, -1)
    wuq = jnp.concatenate([uq_a, uq_b], axis=1).astype(BF16)
    uk = mla_w_uk[l]
    wuk = jnp.concatenate([uk, jnp.zeros((r, MLA_HEADS, HEAD_SLAB - MLA_NOPE), uk.dtype)], axis=-1)
    wuk = wuk.reshape(r, -1).astype(BF16)
    wuv = mla_w_uv[l].reshape(r, -1).astype(BF16)
    wuvt = wuv.T

    qn, kn = mla_q_norm[l], mla_k_norm[l]
    z32 = jnp.zeros((HEAD_SLAB - MLA_NOPE - MLA_ROPE,), F32)
    qga = jnp.concatenate([qn, z32])[None]
    qgb = jnp.concatenate([jnp.zeros((MLA_NOPE,), F32), qn[MLA_NOPE + half:], qn[MLA_NOPE:MLA_NOPE + half], z32])[None]
    kgk = jnp.concatenate([kn[:MLA_NOPE], jnp.zeros((HEAD_SLAB - MLA_NOPE,), F32)])[None]
    gkr = kn[MLA_NOPE:][None]

    blk = np.zeros((HEAD_SLAB, HEAD_SLAB), np.float32)
    blk[:MLA_NOPE, :MLA_NOPE] = 1.0 / MLA_NOPE
    mk = np.kron(np.eye(2), blk)
    blk_q = blk.copy()
    blk_q[MLA_NOPE:MLA_NOPE + MLA_ROPE, MLA_NOPE:MLA_NOPE + MLA_ROPE] = 1.0 / MLA_ROPE
    mq = np.kron(np.eye(2), blk_q)
    m64 = np.kron(np.eye(4), np.full((64, 64), 1.0 / 64, np.float32))
    tile = np.zeros((MLA_ROPE, 2 * HEAD_SLAB), np.float32)
    for hd in range(2):
        tile[np.arange(MLA_ROPE), HEAD_SLAB * hd + MLA_NOPE + np.arange(MLA_ROPE)] = 1.0

    def norm_bound(g):
        return jnp.sqrt(MLA_NOPE * jnp.max(jnp.abs(g[:MLA_NOPE])) ** 2 + MLA_ROPE * jnp.max(jnp.abs(g[MLA_NOPE:])) ** 2)
    logit_bound = (MLA_SCALE * norm_bound(qn) * norm_bound(kn)).reshape(1).astype(F32)

    return dict(
        logit_bound=logit_bound, nm=norm_mix[l][None], win=win, cqn=mla_cq_norm[l][None], ckvn=mla_ckv_norm[l][None], wuq=wuq, wuk=wuk,
        wuv=wuv, wuvt=wuvt, qga=qga, qgb=qgb, mq=jnp.asarray(mq, BF16), mk=jnp.asarray(mk, BF16), kgk=kgk, gkr=gkr,
        tile=jnp.asarray(tile, BF16), lbl=hg_lb_logits, wa2=gla_w_a2[l].astype(BF16), ba=gla_b_a[l][None],
        mon=mla_out_norm[l][None], hon=jnp.tile(hg_out_norm[l], HG_HEADS)[None],
        gon=jnp.tile(gla_out_norm[l], GLA_HEADS)[None], nf=norm_ffn[l][None], m64=jnp.asarray(m64, BF16),
        wo=w_o[l].astype(BF16), wfi=w_ffn_in[l].astype(BF16), wfo=w_ffn_out[l].astype(BF16))


def _rope_tables(pos, reps):
    inv = (np.float32(ROPE_THETA) ** (-(np.arange(0, MLA_ROPE, 2, dtype=np.float32) / np.float32(MLA_ROPE)))).astype(np.float32)
    ang = pos.astype(np.float32)[:, None] * inv[None, :]
    cos, sin = np.cos(ang).astype(np.float32), np.sin(ang).astype(np.float32)
    n = pos.shape[0]
    pad = HEAD_SLAB - MLA_NOPE - MLA_ROPE
    cosq = np.concatenate([np.ones((n, MLA_NOPE), np.float32), cos, cos, np.zeros((n, pad), np.float32)], axis=1)
    sinq = np.concatenate([np.zeros((n, MLA_NOPE), np.float32), -sin, sin, np.zeros((n, pad), np.float32)], axis=1)
    cosk = np.concatenate([cos, cos], axis=1)
    sink = np.concatenate([-sin, sin], axis=1)
    tabs = dict(cosq=cosq, sinq=sinq, cosk=cosk, sink=sink)
    return {k: jnp.asarray(np.tile(v, (reps, 1))) for k, v in tabs.items()}


def _block_diag_state(state):
    b, hh, dk, dv = state.shape
    eye = jnp.eye(hh, dtype=state.dtype)
    return (state[:, :, :, None, :] * eye[None, :, None, :, None]).reshape(b, hh * dk, hh * dv)


def _head_states(s_bd, hh):
    dk, dv = s_bd.shape[1] // hh, s_bd.shape[2] // hh
    return jnp.stack([s_bd[:, hd * dk:(hd + 1) * dk, hd * dv:(hd + 1) * dv] for hd in range(hh)], axis=1)


def _group_layer(x, mods, lw, tabs, layer, bb, tt, rtt, s_hg, s_gla, attn_fn, transpose_v):
    sh1, sc1, g1, sh2, sc2, g2 = mods
    ckv, kr, q, k, v, hrec, grec, gate = _inproj_call(x, sh1, sc1, lw, tabs, bb, tt, layer, transpose_v)
    om = attn_fn(q, k, v)
    oh, shg = _recur_call(hrec, s_hg, HG_HEADS * HG_DK, rtt, "recur_hgrn")
    og, sgl = _recur_call(grec, s_gla, GLA_HEADS * GLA_DK, rtt, "recur_gla")
    y = _outffn_call(x, om, oh, og, gate, (g1, sh2, sc2, g2), lw, bb, tt)
    return y, ckv, kr, _head_states(shg, HG_HEADS), _head_states(sgl, GLA_HEADS)


def kernel(x_prompt, x_sample, c_prompt, c_sample, cache_mla_ckv, cache_mla_krope, state_hgrn, state_gla, w_mod, b_mod, norm_mix, norm_ffn, w_in, mla_cq_norm, mla_ckv_norm, mla_w_uq, mla_w_uk, mla_w_uv, mla_q_norm, mla_k_norm, mla_out_norm, hg_lb_logits, hg_out_norm, gla_w_a2, gla_b_a, gla_out_norm, w_o, w_ffn_in, w_ffn_out):
    bp, t_p, d = x_prompt.shape
    bs, t_s, _ = x_sample.shape
    depth = w_mod.shape[0]
    tm = 512
    tt_p = min(tm, t_p)
    bb_s = max(1, min(bs, tm // t_s))
    rtt_p = min(256, t_p)
    rtt_s = min(128, t_s)
    attn_tile = min(512, t_p)
    tkc = min(1024, cache_mla_ckv.shape[2])

    tabs_p = _rope_tables(np.arange(t_p), 1)
    tabs_s = _rope_tables(PAST_LEN + np.arange(t_s), bb_s)

    n_c = bp + bs
    c_rows = -(-n_c // 16) * 16
    c_pad = jnp.concatenate([c_prompt, c_sample, jnp.zeros((c_rows - n_c, d), c_prompt.dtype)], axis=0)

    cache_kr_t = jnp.swapaxes(cache_mla_krope, 2, 3)
    xp, xs = x_prompt, x_sample
    outs = [[] for _ in range(8)]
    for l in range(depth):
        lw = _layer_weights(l, w_in, norm_mix, norm_ffn, mla_cq_norm, mla_ckv_norm, mla_w_uq, mla_w_uk, mla_w_uv,
                            mla_q_norm, mla_k_norm, mla_out_norm, hg_lb_logits, hg_out_norm, gla_w_a2, gla_b_a,
                            gla_out_norm, w_o, w_ffn_in, w_ffn_out)
        mod = _mod_call(c_pad, w_mod, b_mod[:, None, :], l)
        mods_p = [mod[0:bp, i * d:(i + 1) * d][:, None, :] for i in range(6)]
        mods_s = [mod[bp:n_c, i * d:(i + 1) * d][:, None, :] for i in range(6)]

        zero_hg = jnp.zeros((bp, HG_HEADS * HG_DK, HG_HEADS * HG_DV), F32)
        zero_gla = jnp.zeros((bp, GLA_HEADS * GLA_DK, GLA_HEADS * GLA_DV), F32)
        xp, a, b_, c_, d_ = _group_layer(xp, mods_p, lw, tabs_p, l, 1, tt_p, rtt_p, zero_hg, zero_gla,
                                         lambda q, k, vt: _attn_prompt_call(lw["logit_bound"], q, k, vt, attn_tile), True)
        for lst, val in zip(outs[0:4], (a, b_, c_, d_)):
            lst.append(val)
        xs, a, b_, c_, d_ = _group_layer(xs, mods_s, lw, tabs_s, l, bb_s, t_s, rtt_s,
                                         _block_diag_state(state_hgrn[l]), _block_diag_state(state_gla[l]),
                                         lambda q, k, v: _attn_sample_call(lw["logit_bound"], q, k, v, cache_mla_ckv,
                                                                           cache_kr_t, lw, l, tkc), False)
        for lst, val in zip(outs[4:8], (a, b_, c_, d_)):
            lst.append(val)

    return (xp, xs) + tuple(jnp.stack(o) for o in outs)
```

```python
import functools

import numpy as np
import jax
import jax.numpy as jnp
from jax import lax
from jax.experimental import pallas as pl
from jax.experimental.pallas import tpu as pltpu

F32, BF16 = jnp.float32, jnp.bfloat16

D_MODEL = 1024
PAST_LEN = 4096
CHUNK = 64
EPS = 1e-6
NEG_INF = -1e30
ROPE_THETA = 10000.0
MLA_HEADS = 8
MLA_NOPE = 64
MLA_ROPE = 32
MLA_V = 64
MLA_Q_RANK = 256
MLA_KV_RANK = 256
MLA_WIDTH = MLA_HEADS * MLA_V
MLA_SCALE = (MLA_NOPE + MLA_ROPE) ** -0.5
LOG2E = 1.4426950408889634
HG_HEADS = 4
HG_DK = 64
HG_DV = 64
GLA_HEADS = 4
GLA_DK = 32
GLA_DV = 64
GLA_RANK = 16
GLA_TAU = 16.0
D_FF = 2816

LANES = 128
HEAD_SLAB = 128
SUB = 16
MAX_CHUNK_DECAY_LOG2 = 110.0
ZCOLS = 2432
VMEM_LIMIT = 56 * 1024 * 1024

_Z = dict(cq=0, ckv=256, hq=512, hf=768, hi=1024, hg=1280, gq=1536, gk=1664, gv=1792, gg=2048, misc=2304)


def _bdot(a, b):
    return jnp.dot(a.astype(BF16), b.astype(BF16), preferred_element_type=F32)


def _dot_nt(a, b):
    return lax.dot_general(a, b, (((1,), (1,)), ((), ())), preferred_element_type=F32)


def _dot_tn(a, b):
    return lax.dot_general(a, b, (((0,), (0,)), ((), ())), preferred_element_type=F32)


def _sigmoid(x):
    return 1.0 / (1.0 + jnp.exp(-x))


def _silu(x):
    return x * _sigmoid(x)


def _log_sigmoid(x):
    return jnp.minimum(x, 0.0) - jnp.log1p(jnp.exp(-jnp.abs(x)))


def _rms(x):
    return x * lax.rsqrt(jnp.mean(x * x, axis=-1, keepdims=True) + EPS)


def _params(sem):
    return pltpu.CompilerParams(dimension_semantics=sem, vmem_limit_bytes=VMEM_LIMIT)


def _const_spec(shape):
    nd = len(shape)
    return pl.BlockSpec(shape, lambda *_: (0,) * nd)


def _mod_body(c_ref, w_ref, b_ref, o_ref):
    o_ref[...] = _bdot(_silu(c_ref[...]), w_ref[0]) + b_ref[0]


def _mod_call(c_pad, w_mod, b_mod, layer):
    rows, d = c_pad.shape
    n = w_mod.shape[2]
    tn = 1024
    return pl.pallas_call(
        _mod_body,
        grid=(n // tn,),
        in_specs=[pl.BlockSpec((rows, d), lambda j: (0, 0)),
                  pl.BlockSpec((1, d, tn), lambda j: (layer, 0, j)),
                  pl.BlockSpec((1, 1, tn), lambda j: (layer, 0, j))],
        out_specs=pl.BlockSpec((rows, tn), lambda j: (0, j)),
        out_shape=jax.ShapeDtypeStruct((rows, n), F32),
        compiler_params=_params(("arbitrary",)),
        name="mod",
    )(c_pad, w_mod, b_mod)


def _mla_kv(ckvn_bf, kro_bf, w_uk_ref, w_uv_ref, mk_ref, kgk_ref, tile_ref, transpose_v=False, kro_transposed=False):
    place = _dot_tn if kro_transposed else functools.partial(jnp.dot, preferred_element_type=F32)
    krt = place(kro_bf, tile_ref[...])
    kgk2 = jnp.concatenate([kgk_ref[...], kgk_ref[...]], axis=-1)
    parts = []
    for hp in range(MLA_HEADS // 2):
        sl = slice(2 * HEAD_SLAB * hp, 2 * HEAD_SLAB * (hp + 1))
        sk = jnp.dot(ckvn_bf, w_uk_ref[:, sl], preferred_element_type=F32)
        msk = jnp.dot((sk * sk).astype(BF16), mk_ref[...], preferred_element_type=F32)
        parts.append((sk * lax.rsqrt(msk + EPS) * kgk2 + krt).astype(BF16))
    k = jnp.concatenate(parts, axis=-1)
    if transpose_v:
        v = _dot_nt(w_uv_ref[...], ckvn_bf).astype(BF16)
    else:
        v = jnp.dot(ckvn_bf, w_uv_ref[...], preferred_element_type=F32).astype(BF16)
    return k, v


def _inproj_body(x_ref, sh_ref, sc_ref, nm_ref, win_ref, cqn_ref, ckvn_ref, wuq_ref, wuk_ref, wuv_ref,
                 qga_ref, qgb_ref, cosq_ref, sinq_ref, mq_ref, mk_ref, kgk_ref, gkr_ref, cosk_ref, sink_ref,
                 tile_ref, lbl_ref, wa2_ref, ba_ref,
                 ckv_o, kr_o, q_o, k_o, v_o, hrec_o, grec_o, gate_o, *, layer, transpose_v):
    bb, tt, d = x_ref.shape
    tm = bb * tt
    x = x_ref[...]
    h = _rms(x) * nm_ref[...] * (1.0 + sc_ref[...]) + sh_ref[...]
    z = jnp.dot(h.reshape(tm, d).astype(BF16), win_ref[...], preferred_element_type=F32)

    def seg(name, w):
        return z[:, _Z[name]:_Z[name] + w]

    cqn = _rms(seg("cq", MLA_Q_RANK)) * cqn_ref[...]
    ckvn = _rms(seg("ckv", MLA_KV_RANK)) * ckvn_ref[...]
    ckv_o[...] = ckvn.reshape(bb, tt, MLA_KV_RANK)
    misc = seg("misc", LANES)

    qab = jnp.dot(cqn.astype(BF16), wuq_ref[...], preferred_element_type=F32)
    width = MLA_HEADS * HEAD_SLAB
    ca = cosq_ref[...] * qga_ref[...]
    sb = sinq_ref[...] * qgb_ref[...]
    ca2 = jnp.concatenate([ca, ca], axis=-1)
    sb2 = jnp.concatenate([sb, sb], axis=-1)
    for hp in range(MLA_HEADS // 2):
        sl = slice(2 * HEAD_SLAB * hp, 2 * HEAD_SLAB * (hp + 1))
        qa = qab[:, sl]
        qb = qab[:, width + sl.start:width + sl.stop]
        msq = jnp.dot((qa * qa).astype(BF16), mq_ref[...], preferred_element_type=F32)
        qo = lax.rsqrt(msq + EPS) * (qa * ca2 + qb * sb2) * (MLA_SCALE * LOG2E)
        q_o[:, :, sl] = qo.astype(BF16).reshape(bb, tt, 2 * HEAD_SLAB)

    kr = misc[:, 0:MLA_ROPE]
    krn = _rms(kr) * gkr_ref[...]
    half = MLA_ROPE // 2
    swapped = jnp.concatenate([krn[:, half:], krn[:, :half]], axis=-1)
    kro = krn * cosk_ref[...] + swapped * sink_ref[...]
    kr_o[...] = kro.reshape(bb, tt, MLA_ROPE)
    k, v = _mla_kv(ckvn.astype(BF16), kro.astype(BF16), wuk_ref, wuv_ref, mk_ref, kgk_ref, tile_ref, transpose_v)
    k_o[...] = k.reshape(bb, tt, width)
    if transpose_v:
        v_o[0] = v
    else:
        v_o[...] = v.reshape(bb, tt, MLA_WIDTH)

    lg = lbl_ref[...]
    e = jnp.exp(lg - jnp.max(lg, axis=0, keepdims=True))
    p = e / jnp.sum(e, axis=0, keepdims=True)
    lb = jnp.sum(p[0:layer + 1], axis=0, keepdims=True) - p[0:1]
    f = lb + (1.0 - lb) * _sigmoid(seg("hf", 256))
    hrec_o[:, :, 0:256] = seg("hq", 256).reshape(bb, tt, 256)
    hrec_o[:, :, 256:512] = (1.0 - f).reshape(bb, tt, 256)
    hrec_o[:, :, 512:768] = jnp.log(f).reshape(bb, tt, 256)
    hrec_o[:, :, 768:1024] = seg("hi", 256).reshape(bb, tt, 256)

    ga = misc[:, MLA_ROPE:MLA_ROPE + GLA_RANK]
    apre = jnp.dot(ga.astype(BF16), wa2_ref[...], preferred_element_type=F32) + ba_ref[...]
    grec_o[:, :, 0:128] = (seg("gq", 128) * (GLA_DK ** -0.5)).reshape(bb, tt, 128)
    grec_o[:, :, 128:256] = seg("gk", 128).reshape(bb, tt, 128)
    grec_o[:, :, 256:384] = (_log_sigmoid(apre) / GLA_TAU).reshape(bb, tt, 128)
    grec_o[:, :, 384:640] = seg("gv", 256).reshape(bb, tt, 256)

    gate_o[:, :, 0:256] = _silu(seg("hg", 256)).reshape(bb, tt, 256)
    gate_o[:, :, 256:512] = _silu(seg("gg", 256)).reshape(bb, tt, 256)


def _inproj_call(x, sh, sc, lw, tabs, bb, tt, layer, transpose_v):
    b, t, d = x.shape
    tm = bb * tt
    grid = (b // bb, t // tt)
    tok = lambda w: pl.BlockSpec((bb, tt, w), lambda i, j: (i, j, 0))
    modspec = pl.BlockSpec((bb, 1, d), lambda i, j: (i, 0, 0))
    tab = lambda w: pl.BlockSpec((tm, w), lambda i, j: (j, 0))
    assert bb == 1 or not transpose_v
    consts = [lw["nm"], lw["win"], lw["cqn"], lw["ckvn"], lw["wuq"], lw["wuk"], lw["wuvt" if transpose_v else "wuv"],
              lw["qga"], lw["qgb"]]
    consts2 = [lw["mq"], lw["mk"], lw["kgk"], lw["gkr"]]
    consts3 = [lw["tile"], lw["lbl"], lw["wa2"], lw["ba"]]
    in_specs = ([tok(d), modspec, modspec] + [_const_spec(a.shape) for a in consts]
                + [tab(LANES), tab(LANES)] + [_const_spec(a.shape) for a in consts2]
                + [tab(MLA_ROPE), tab(MLA_ROPE)] + [_const_spec(a.shape) for a in consts3])
    widths = [(MLA_KV_RANK, F32), (MLA_ROPE, F32), (MLA_HEADS * HEAD_SLAB, BF16), (MLA_HEADS * HEAD_SLAB, BF16),
              (MLA_WIDTH, BF16), (1024, F32), (640, F32), (512, F32)]
    out_specs = [tok(w) for w, _ in widths]
    out_shape = [jax.ShapeDtypeStruct((b, t, w), dt) for w, dt in widths]
    if transpose_v:
        out_specs[4] = pl.BlockSpec((1, MLA_WIDTH, tt), lambda i, j: (i, 0, j))
        out_shape[4] = jax.ShapeDtypeStruct((b, MLA_WIDTH, t), BF16)
    return pl.pallas_call(
        functools.partial(_inproj_body, layer=layer, transpose_v=transpose_v),
        grid=grid,
        in_specs=in_specs,
        out_specs=out_specs,
        out_shape=out_shape,
        compiler_params=_params(("arbitrary", "arbitrary")),
        name="inproj",
    )(x, sh, sc, *consts, tabs["cosq"], tabs["sinq"], *consts2, tabs["cosk"], tabs["sink"], *consts3)


MAX_UNSHIFTED_LOGIT = 60.0
SMALL_SCORE_TILE = 64 * 1024


def _attend_bounded(q, k, v, lf_ref, accf_ref, bias):
    tq, tk = q.shape[0], k.shape[0]

    def scores(hd):
        sl = slice(HEAD_SLAB * hd, HEAD_SLAB * (hd + 1))
        s = _dot_nt(q[:, sl], k[:, sl])
        return s if bias is None else s + bias

    def accumulate(hd, s):
        p = jnp.exp2(s)
        if tk % LANES == 0:
            part = p[:, 0:LANES]
            for c in range(1, tk // LANES):
                part = part + p[:, LANES * c:LANES * (c + 1)]
        else:
            lane0 = lax.broadcasted_iota(jnp.int32, (tq, LANES), 1) == 0
            part = jnp.where(lane0, jnp.sum(p, axis=-1, keepdims=True), 0.0)
        lf_ref[hd] += part
        hp = hd // 2
        accf_ref[hd] += jnp.dot(p.astype(BF16), v[:, LANES * hp:LANES * (hp + 1)], preferred_element_type=F32)

    if tq * tk <= SMALL_SCORE_TILE:
        ss = [scores(hd) for hd in range(MLA_HEADS)]
        for hd in range(MLA_HEADS):
            accumulate(hd, ss[hd])
    else:
        for hd in range(MLA_HEADS):
            accumulate(hd, scores(hd))


def _finish_bounded(lf_ref, accf_ref):
    tq = accf_ref.shape[1]
    low = lax.broadcasted_iota(jnp.int32, (tq, LANES), 1) < MLA_V
    outs = []
    for hp in range(MLA_HEADS // 2):
        la = jnp.sum(lf_ref[2 * hp], axis=-1, keepdims=True)
        lb = jnp.sum(lf_ref[2 * hp + 1], axis=-1, keepdims=True)
        outs.append(jnp.where(low, accf_ref[2 * hp] / la, accf_ref[2 * hp + 1] / lb))
    return jnp.concatenate(outs, axis=-1)


def _attend_online(q, k, v, m_ref, l_ref, acc_ref, bias):
    tq = q.shape[0]
    low = lax.broadcasted_iota(jnp.int32, (tq, LANES), 1) < MLA_V
    for hp in range(MLA_HEADS // 2):
        vp = v[:, LANES * hp:LANES * (hp + 1)]
        pvs, alphas = [], []
        for e in range(2):
            hd = 2 * hp + e
            sl = slice(HEAD_SLAB * hd, HEAD_SLAB * (hd + 1))
            s = _dot_nt(q[:, sl], k[:, sl])
            if bias is not None:
                s = s + bias
            m_prev = m_ref[hd]
            m_new = jnp.maximum(m_prev, jnp.max(s, axis=-1, keepdims=True))
            p = jnp.exp2(s - m_new)
            alpha = jnp.exp2(m_prev - m_new)
            l_ref[hd] = alpha * l_ref[hd] + jnp.sum(p, axis=-1, keepdims=True)
            m_ref[hd] = m_new
            pvs.append(jnp.dot(p.astype(BF16), vp, preferred_element_type=F32))
            alphas.append(alpha)
        sl2 = slice(LANES * hp, LANES * (hp + 1))
        acc_ref[:, sl2] = jnp.where(low, alphas[0], alphas[1]) * acc_ref[:, sl2] + jnp.where(low, pvs[0], pvs[1])


def _finish_online(l_ref, acc_ref):
    tq = acc_ref.shape[0]
    low = lax.broadcasted_iota(jnp.int32, (tq, LANES), 1) < MLA_V
    outs = []
    for hp in range(MLA_HEADS // 2):
        l2 = jnp.where(low, l_ref[2 * hp], l_ref[2 * hp + 1])
        outs.append(acc_ref[:, LANES * hp:LANES * (hp + 1)] / l2)
    return jnp.concatenate(outs, axis=-1)


def _softmax_scratch(tq):
    return [pltpu.VMEM((MLA_HEADS, tq, 1), F32), pltpu.VMEM((MLA_HEADS, tq, 1), F32), pltpu.VMEM((tq, MLA_WIDTH), F32),
            pltpu.VMEM((MLA_HEADS, tq, LANES), F32), pltpu.VMEM((MLA_HEADS, tq, LANES), F32)]


def _softmax_init(m_ref, l_ref, acc_ref, lf_ref, accf_ref):
    m_ref[...] = jnp.full(m_ref.shape, NEG_INF, F32)
    for ref in (l_ref, acc_ref, lf_ref, accf_ref):
        ref[...] = jnp.zeros(ref.shape, F32)


def _attend(bounded, cond, get_qkv, scratch, bias_ref):
    m_ref, l_ref, acc_ref, lf_ref, accf_ref = scratch

    @pl.when(jnp.logical_and(cond, bounded))
    def _():
        _attend_bounded(*get_qkv(), lf_ref, accf_ref, None if bias_ref is None else bias_ref[...])

    @pl.when(jnp.logical_and(cond, jnp.logical_not(bounded)))
    def _():
        _attend_online(*get_qkv(), m_ref, l_ref, acc_ref, None if bias_ref is None else bias_ref[...])


def _softmax_finish(bounded, cond, o_ref, scratch):
    m_ref, l_ref, acc_ref, lf_ref, accf_ref = scratch

    @pl.when(jnp.logical_and(cond, bounded))
    def _():
        o_ref[0] = _finish_bounded(lf_ref, accf_ref)

    @pl.when(jnp.logical_and(cond, jnp.logical_not(bounded)))
    def _():
        o_ref[0] = _finish_online(l_ref, acc_ref)


def _attend_t_bounded(q, k, vt, lf_ref, acc_ref, bias_t):
    tq, tk = q.shape[0], k.shape[0]

    def scores_t(hd):
        sl = slice(HEAD_SLAB * hd, HEAD_SLAB * (hd + 1))
        st = _dot_nt(k[:, sl], q[:, sl])
        return st if bias_t is None else st + bias_t

    st_next = scores_t(0)
    for hd in range(MLA_HEADS):
        st = st_next
        if hd + 1 < MLA_HEADS:
            st_next = scores_t(hd + 1)
        pt = jnp.exp2(st)
        lf_ref[hd] += jnp.sum(pt.reshape(tk // 8, 8, tq), axis=0)
        rows = slice(MLA_V * hd, MLA_V * (hd + 1))
        acc_ref[rows, :] += jnp.dot(vt[rows, :], pt.astype(BF16), preferred_element_type=F32)


def _attend_t_online(q, k, vt, m_ref, l_ref, acc_ref, bias_t):
    for hd in range(MLA_HEADS):
        sl = slice(HEAD_SLAB * hd, HEAD_SLAB * (hd + 1))
        st = _dot_nt(k[:, sl], q[:, sl])
        if bias_t is not None:
            st = st + bias_t
        m_prev = m_ref[hd]
        m_new = jnp.maximum(m_prev, jnp.max(st, axis=0, keepdims=True))
        pt = jnp.exp2(st - m_new)
        alpha = jnp.exp2(m_prev - m_new)
        l_ref[hd] = alpha * l_ref[hd] + jnp.sum(pt, axis=0, keepdims=True)
        m_ref[hd] = m_new
        rows = slice(MLA_V * hd, MLA_V * (hd + 1))
        acc_ref[rows, :] = alpha * acc_ref[rows, :] + jnp.dot(vt[rows, :], pt.astype(BF16), preferred_element_type=F32)


def _finish_t(l_of_head, acc_ref):
    outs = []
    for hd in range(MLA_HEADS):
        outs.append(acc_ref[MLA_V * hd:MLA_V * (hd + 1), :] / l_of_head(hd))
    return jnp.concatenate(outs, axis=0).T


def _attn_prompt_body(qi_ref, ki_ref, bound_ref, bias_ref, q_ref, k_ref, vt_ref, o_ref, m_ref, l_ref, lf_ref, acc_ref):
    step = pl.program_id(1)
    i = qi_ref[step]
    j = ki_ref[step]
    bounded = bound_ref[0] <= MAX_UNSHIFTED_LOGIT
    unbounded = jnp.logical_not(bounded)
    diag = j == i

    @pl.when(j == 0)
    def _():
        m_ref[...] = jnp.full(m_ref.shape, NEG_INF, F32)
        for ref in (l_ref, lf_ref, acc_ref):
            ref[...] = jnp.zeros(ref.shape, F32)

    for on_diag in (False, True):
        cond = diag if on_diag else j < i

        @pl.when(jnp.logical_and(cond, bounded))
        def _():
            _attend_t_bounded(q_ref[0], k_ref[0], vt_ref[0], lf_ref, acc_ref, bias_ref[...] if on_diag else None)

        @pl.when(jnp.logical_and(cond, unbounded))
        def _():
            _attend_t_online(q_ref[0], k_ref[0], vt_ref[0], m_ref, l_ref, acc_ref, bias_ref[...] if on_diag else None)

    @pl.when(jnp.logical_and(diag, bounded))
    def _():
        o_ref[0] = _finish_t(lambda hd: jnp.sum(lf_ref[hd], axis=0, keepdims=True), acc_ref)

    @pl.when(jnp.logical_and(diag, unbounded))
    def _():
        o_ref[0] = _finish_t(lambda hd: l_ref[hd], acc_ref)


def _attn_prompt_call(bound, q, k, vt, tile):
    b, t, _ = q.shape
    n = t // tile
    pairs = [(i, j) for i in range(n) for j in range(i + 1)]
    qi = jnp.asarray([p[0] for p in pairs], jnp.int32)
    ki = jnp.asarray([p[1] for p in pairs], jnp.int32)
    q_map = lambda bi, st, qi_ref, ki_ref: (bi, qi_ref[st], 0)
    k_map = lambda bi, st, qi_ref, ki_ref: (bi, ki_ref[st], 0)
    vt_map = lambda bi, st, qi_ref, ki_ref: (bi, 0, ki_ref[st])
    chunk_of = np.arange(tile) // CHUNK
    bias_t = jnp.asarray(np.where(chunk_of[:, None] <= chunk_of[None, :], 0.0, NEG_INF), F32)
    grid_spec = pltpu.PrefetchScalarGridSpec(
        num_scalar_prefetch=2,
        grid=(b, len(pairs)),
        in_specs=[pl.BlockSpec(memory_space=pltpu.SMEM), pl.BlockSpec((tile, tile), lambda *_: (0, 0)),
                  pl.BlockSpec((1, tile, q.shape[2]), q_map),
                  pl.BlockSpec((1, tile, k.shape[2]), k_map),
                  pl.BlockSpec((1, MLA_WIDTH, tile), vt_map)],
        out_specs=pl.BlockSpec((1, tile, MLA_WIDTH), q_map),
        scratch_shapes=[pltpu.VMEM((MLA_HEADS, 1, tile), F32), pltpu.VMEM((MLA_HEADS, 1, tile), F32),
                        pltpu.VMEM((MLA_HEADS, 8, tile), F32), pltpu.VMEM((MLA_WIDTH, tile), F32)])
    return pl.pallas_call(
        _attn_prompt_body,
        grid_spec=grid_spec,
        out_shape=jax.ShapeDtypeStruct((b, t, MLA_WIDTH), F32),
        compiler_params=_params(("arbitrary", "arbitrary")),
        name="attn_prompt",
    )(qi, ki, bound, bias_t, q, k, vt)


def _attn_sample_body(bound_ref, q_ref, kn_ref, vn_ref, ckv_ref, kr_ref, wuk_ref, wuv_ref, mk_ref, kgk_ref, tile_ref,
                      o_ref, *scratch):
    j = pl.program_id(1)
    bounded = bound_ref[0] <= MAX_UNSHIFTED_LOGIT
    always = j >= 0

    @pl.when(j == 0)
    def _():
        _softmax_init(*scratch)

    _attend(bounded, j == 0, lambda: (q_ref[0], kn_ref[0], vn_ref[0]), scratch, None)

    def past_qkv():
        k, v = _mla_kv(ckv_ref[0, 0].astype(BF16), kr_ref[0, 0].astype(BF16), wuk_ref, wuv_ref, mk_ref, kgk_ref,
                       tile_ref, kro_transposed=True)
        return q_ref[0], k, v

    _attend(bounded, always, past_qkv, scratch, None)
    _softmax_finish(bounded, j == pl.num_programs(1) - 1, o_ref, scratch)


def _attn_sample_call(bound, q, k_new, v_new, cache_ckv, cache_kr_t, lw, layer, tkc):
    b, t, _ = q.shape
    past = cache_ckv.shape[2]
    consts = [lw["wuk"], lw["wuv"], lw["mk"], lw["kgk"], lw["tile"]]
    new = lambda w: pl.BlockSpec((1, t, w), lambda bi, j: (bi, 0, 0))
    return pl.pallas_call(
        _attn_sample_body,
        grid=(b, past // tkc),
        in_specs=[pl.BlockSpec(memory_space=pltpu.SMEM), new(q.shape[2]), new(k_new.shape[2]), new(v_new.shape[2]),
                  pl.BlockSpec((1, 1, tkc, MLA_KV_RANK), lambda bi, j: (layer, bi, j, 0)),
                  pl.BlockSpec((1, 1, MLA_ROPE, tkc), lambda bi, j: (layer, bi, 0, j))]
                 + [_const_spec(a.shape) for a in consts],
        out_specs=new(MLA_WIDTH),
        out_shape=jax.ShapeDtypeStruct((b, t, MLA_WIDTH), F32),
        scratch_shapes=_softmax_scratch(t),
        compiler_params=_params(("arbitrary", "arbitrary")),
        name="attn_sample",
    )(bound, q, k_new, v_new, cache_ckv, cache_kr_t, *consts)


def _cumsum_matrix(tt, blk):
    r = np.arange(tt)
    return (((r[:, None] // blk) == (r[None, :] // blk)) & (r[None, :] <= r[:, None])).astype(np.float32)


def _block_cumsum(g, lmat_ref, blk):
    tt, c = g.shape
    lmat = lmat_ref[...]
    g1 = g.astype(BF16)
    r1 = g - g1.astype(F32)
    g2 = r1.astype(BF16)
    g3 = (r1 - g2.astype(F32)).astype(BF16)
    bb = jnp.dot(lmat, jnp.concatenate([g1, g2, g3], axis=-1), preferred_element_type=F32)
    b = (bb[:, 2 * c:3 * c] + bb[:, c:2 * c]) + bb[:, 0:c]
    last = [jnp.broadcast_to(b[blk * (n + 1) - 1:blk * (n + 1)], (blk, c)) for n in range(tt // blk)]
    return b, jnp.concatenate(last, axis=0)


def _state_chain(blk, qe, ke, vb, dec, o_in_block, bd_ref, o_ref, s_ref):
    tt = qe.shape[0]
    nb = tt // blk
    picked = [dec[blk * n:blk * n + 1] for n in range(nb)]
    if nb % 8:
        picked.append(jnp.zeros((8 - nb % 8, dec.shape[1]), F32))
    dsel = jnp.concatenate(picked, axis=0).T
    us = [_dot_tn(ke[blk * n:blk * (n + 1)], vb[blk * n:blk * (n + 1)]) * bd_ref[...] for n in range(nb)]
    s = s_ref[...]
    for n in range(nb):
        rows = slice(blk * n, blk * (n + 1))
        o_ref[0, rows, :] = jnp.dot(qe[rows], s.astype(BF16), preferred_element_type=F32) + o_in_block(n)
        s = dsel[:, n:n + 1] * s + us[n]
    s_ref[...] = s


def _recur_factorised(q, k, v, b2, bt2, chunk, bd_ref, o_ref, s_ref, hk):
    heads = 4
    dk = hk // heads
    tt = q.shape[0]
    qe = (q * jnp.exp2(b2)).astype(BF16)
    kh = (k * jnp.exp2(-b2)).astype(BF16)
    ke = (k * jnp.exp2(bt2 - b2)).astype(BF16)
    dec = jnp.exp2(bt2)
    vb = v.astype(BF16)
    rows_k = lax.broadcasted_iota(jnp.int32, (heads * chunk, hk), 0) // chunk
    same_k = rows_k == lax.broadcasted_iota(jnp.int32, (heads * chunk, hk), 1) // dk
    rows_v = lax.broadcasted_iota(jnp.int32, (heads * chunk, 256), 0) // chunk
    same_v = rows_v == lax.broadcasted_iota(jnp.int32, (heads * chunk, 256), 1) // 64
    ii = lax.broadcasted_iota(jnp.int32, (chunk, heads * chunk), 0)
    jj = lax.broadcasted_iota(jnp.int32, (chunk, heads * chunk), 1) % chunk
    causal = jj <= ii
    chunks = [slice(chunk * n, chunk * (n + 1)) for n in range(tt // chunk)]
    scs = []
    for rows in chunks:
        kbd = jnp.where(same_k, jnp.concatenate([kh[rows]] * heads, axis=0), jnp.zeros(same_k.shape, BF16))
        scs.append(_dot_nt(qe[rows], kbd))
    ods = []
    for rows, sc in zip(chunks, scs):
        vbd = jnp.where(same_v, jnp.concatenate([vb[rows]] * heads, axis=0), jnp.zeros(same_v.shape, BF16))
        ods.append(jnp.dot(jnp.where(causal, sc, 0.0).astype(BF16), vbd, preferred_element_type=F32))
    _state_chain(chunk, qe, ke, vb, dec, lambda n: ods[n], bd_ref, o_ref, s_ref)


def _recur_pairwise(q, k, g, v, lsub_ref, e2_ref, sel_ref, bd_ref, o_ref, s_ref):
    tt = q.shape[0]
    nb = tt // SUB
    b, btot = _block_cumsum(g, lsub_ref, SUB)
    b2 = b * LOG2E
    bt2 = btot * LOG2E
    qe = (q * jnp.exp2(b2)).astype(BF16)
    ke = (k * jnp.exp2(bt2 - b2)).astype(BF16)
    dec = jnp.exp2(bt2)
    vb = v.astype(BF16)
    xs = []
    for n in range(nb):
        r0 = SUB * n
        bj, kj = b2[r0:r0 + SUB], k[r0:r0 + SUB]
        for i0 in range(SUB):
            bi = b2[r0 + i0:r0 + i0 + 1]
            qi = q[r0 + i0:r0 + i0 + 1]
            xs.append((jnp.exp2(jnp.minimum(bi - bj, 0.0)) * (kj * qi)).astype(BF16))
    x = jnp.concatenate(xs, axis=0)
    r = jnp.dot(x, e2_ref[...], preferred_element_type=F32)
    y = (r.reshape(nb, SUB, SUB, 256) * v.reshape(nb, 1, SUB, 256)).reshape(nb * SUB * SUB, 256).astype(BF16)
    ods = [jnp.dot(sel_ref[...], y[SUB * SUB * n:SUB * SUB * (n + 1)], preferred_element_type=F32) for n in range(nb)]
    _state_chain(SUB, qe, ke, vb, dec, lambda n: ods[n], bd_ref, o_ref, s_ref)


def _recur_body(rec_ref, s0_ref, lchunk_ref, lsub_ref, e2_ref, sel_ref, bd_ref, o_ref, s_o, s_ref, *, hk):
    t = pl.program_id(1)

    @pl.when(t == 0)
    def _():
        s_ref[...] = s0_ref[0]

    rec = rec_ref[0]
    q = rec[:, 0:hk]
    k = rec[:, hk:2 * hk]
    g = rec[:, 2 * hk:3 * hk]
    v = rec[:, 3 * hk:3 * hk + 256]

    tt, c = g.shape
    chunk = min(CHUNK, tt)
    half = chunk // 2
    b, btot = _block_cumsum(g, lchunk_ref, chunk)
    b2 = b * LOG2E
    bt2 = btot * LOG2E
    whole_ok = jnp.min(b2) >= -MAX_CHUNK_DECAY_LOG2

    @pl.when(whole_ok)
    def _():
        _recur_factorised(q, k, v, b2, bt2, chunk, bd_ref, o_ref, s_ref, hk)

    @pl.when(jnp.logical_not(whole_ok))
    def _():
        bh, bth = [], []
        for n in range(tt // chunk):
            mid = jnp.broadcast_to(b2[chunk * n + half - 1:chunk * n + half], (half, c))
            end = jnp.broadcast_to(b2[chunk * (n + 1) - 1:chunk * (n + 1)], (half, c))
            bh += [b2[chunk * n:chunk * n + half], b2[chunk * n + half:chunk * (n + 1)] - mid]
            bth += [mid, end - mid]
        bh = jnp.concatenate(bh, axis=0)
        bth = jnp.concatenate(bth, axis=0)
        half_ok = jnp.min(bh) >= -MAX_CHUNK_DECAY_LOG2

        @pl.when(half_ok)
        def _():
            _recur_factorised(q, k, v, bh, bth, half, bd_ref, o_ref, s_ref, hk)

        @pl.when(jnp.logical_not(half_ok))
        def _():
            _recur_pairwise(q, k, g, v, lsub_ref, e2_ref, sel_ref, bd_ref, o_ref, s_ref)

    @pl.when(t == pl.num_programs(1) - 1)
    def _():
        s_o[0] = s_ref[...]


def _recur_call(rec, s0, hk, tt, name):
    b, t, _ = rec.shape
    w = 3 * hk + 256
    heads = 4
    dk = hk // heads
    e2 = np.kron(np.eye(heads), np.ones((dk, 64))).astype(np.float32)
    sel = np.zeros((SUB, SUB * SUB), np.float32)
    for i in range(SUB):
        sel[i, SUB * i:SUB * i + i + 1] = 1.0
    return pl.pallas_call(
        functools.partial(_recur_body, hk=hk),
        grid=(b, t // tt),
        in_specs=[pl.BlockSpec((1, tt, w), lambda bi, ti: (bi, ti, 0)),
                  pl.BlockSpec((1, hk, 256), lambda bi, ti: (bi, 0, 0)),
                  _const_spec((tt, tt)), _const_spec((tt, tt)),
                  _const_spec((hk, 256)), _const_spec((SUB, SUB * SUB)), _const_spec((hk, 256))],
        out_specs=[pl.BlockSpec((1, tt, 256), lambda bi, ti: (bi, ti, 0)),
                   pl.BlockSpec((1, hk, 256), lambda bi, ti: (bi, 0, 0))],
        out_shape=[jax.ShapeDtypeStruct((b, t, 256), F32), jax.ShapeDtypeStruct((b, hk, 256), F32)],
        scratch_shapes=[pltpu.VMEM((hk, 256), F32)],
        compiler_params=_params(("arbitrary", "arbitrary")),
        name=name,
    )(rec, s0, jnp.asarray(_cumsum_matrix(tt, min(CHUNK, tt)), BF16), jnp.asarray(_cumsum_matrix(tt, SUB), BF16),
      jnp.asarray(e2, BF16), jnp.asarray(sel, BF16), jnp.asarray(e2, F32))


def _outffn_body(x_ref, om_ref, oh_ref, og_ref, gate_ref, g1_ref, sh2_ref, sc2_ref, g2_ref,
                 mon_ref, hon_ref, gon_ref, nf_ref, m64_ref, wo_ref, wfi_ref, wfo_ref, y_ref, *, ff_chunk):
    bb, tt, d = x_ref.shape
    tm = bb * tt
    om = om_ref[...].reshape(tm, MLA_WIDTH)
    omn = _rms(om) * mon_ref[...]

    def headnorm(o, gain_ref, gate):
        ms = jnp.dot((o * o).astype(BF16), m64_ref[...], preferred_element_type=F32)
        return o * lax.rsqrt(ms + EPS) * gain_ref[...] * gate

    gate = gate_ref[...].reshape(tm, 512)
    ohn = headnorm(oh_ref[...].reshape(tm, 256), hon_ref, gate[:, 0:256])
    ogn = headnorm(og_ref[...].reshape(tm, 256), gon_ref, gate[:, 256:512])
    mix_in = jnp.concatenate([omn, ohn, ogn], axis=-1).astype(BF16)
    mix = jnp.dot(mix_in, wo_ref[...], preferred_element_type=F32)
    x1 = x_ref[...] + g1_ref[...] * mix.reshape(bb, tt, d)
    h2 = (_rms(x1) * nf_ref[...] * (1.0 + sc2_ref[...]) + sh2_ref[...]).reshape(tm, d).astype(BF16)
    acc = jnp.zeros((tm, d), F32)
    for c in range(D_FF // ff_chunk):
        a = jnp.dot(h2, wfi_ref[:, ff_chunk * c:ff_chunk * (c + 1)], preferred_element_type=F32)
        u = jnp.dot(h2, wfi_ref[:, D_FF + ff_chunk * c:D_FF + ff_chunk * (c + 1)], preferred_element_type=F32)
        act = (_silu(a) * u).astype(BF16)
        acc = acc + jnp.dot(act, wfo_ref[ff_chunk * c:ff_chunk * (c + 1), :], preferred_element_type=F32)
    y_ref[...] = x1 + g2_ref[...] * acc.reshape(bb, tt, d)


def _outffn_call(x, om, oh, og, gate, mods, lw, bb, tt):
    b, t, d = x.shape
    tok = lambda w: pl.BlockSpec((bb, tt, w), lambda i, j: (i, j, 0))
    modspec = pl.BlockSpec((bb, 1, d), lambda i, j: (i, 0, 0))
    consts = [lw["mon"], lw["hon"], lw["gon"], lw["nf"], lw["m64"], lw["wo"], lw["wfi"], lw["wfo"]]
    single = lambda a: pl.BlockSpec(a.shape, lambda *_: (0,) * a.ndim, pipeline_mode=pl.Buffered(1))
    return pl.pallas_call(
        functools.partial(_outffn_body, ff_chunk=256),
        grid=(b // bb, t // tt),
        in_specs=[tok(d), tok(MLA_WIDTH), tok(256), tok(256), tok(512)] + [modspec] * 4 + [single(a) for a in consts],
        out_specs=tok(d),
        out_shape=jax.ShapeDtypeStruct((b, t, d), F32),
        compiler_params=_params(("arbitrary", "arbitrary")),
        name="outffn",
    )(x, om, oh, og, gate, *mods, *consts)


def _layer_weights(l, w_in, norm_mix, norm_ffn, mla_cq_norm, mla_ckv_norm, mla_w_uq, mla_w_uk, mla_w_uv, mla_q_norm,
                   mla_k_norm, mla_out_norm, hg_lb_logits, hg_out_norm, gla_w_a2, gla_b_a, gla_out_norm, w_o,
                   w_ffn_in, w_ffn_out):
    wi = w_in[l]
    d = wi.shape[0]
    o = dict(cq=0, ckv=256, kr=512, hq=544, hf=800, hi=1056, hg=1312, gq=1568, gk=1696, gv=1824, gg=2080, ga=2336)
    cols = [wi[:, o["cq"]:o["cq"] + 256], wi[:, o["ckv"]:o["ckv"] + 256], wi[:, o["hq"]:o["hq"] + 256],
            wi[:, o["hf"]:o["hf"] + 256], wi[:, o["hi"]:o["hi"] + 256], wi[:, o["hg"]:o["hg"] + 256],
            wi[:, o["gq"]:o["gq"] + 128], wi[:, o["gk"]:o["gk"] + 128], wi[:, o["gv"]:o["gv"] + 256],
            wi[:, o["gg"]:o["gg"] + 256], wi[:, o["kr"]:o["kr"] + 32], wi[:, o["ga"]:o["ga"] + 16],
            jnp.zeros((d, LANES - MLA_ROPE - GLA_RANK), wi.dtype)]
    win = jnp.concatenate(cols, axis=1).astype(BF16)
    assert win.shape[1] == ZCOLS

    uq = mla_w_uq[l]
    r = uq.shape[0]
    half = MLA_ROPE // 2
    zpad = jnp.zeros((r, MLA_HEADS, HEAD_SLAB - MLA_NOPE - MLA_ROPE), uq.dtype)
    uq_a = jnp.concatenate([uq, zpad], axis=-1).reshape(r, -1)
    uq_b = jnp.concatenate([jnp.zeros((r, MLA_HEADS, MLA_NOPE), uq.dtype), uq[..., MLA_NOPE + half:],
                            uq[..., MLA_NOPE:MLA_NOPE + half], zpad], axis=-1).reshape(r, -1)
    wuq = jnp.concatenate([uq_a, uq_b], axis=1).astype(BF16)
    uk = mla_w_uk[l]
    wuk = jnp.concatenate([uk, jnp.zeros((r, MLA_HEADS, HEAD_SLAB - MLA_NOPE), uk.dtype)], axis=-1)
    wuk = wuk.reshape(r, -1).astype(BF16)
    wuv = mla_w_uv[l].reshape(r, -1).astype(BF16)
    wuvt = wuv.T

    qn, kn = mla_q_norm[l], mla_k_norm[l]
    z32 = jnp.zeros((HEAD_SLAB - MLA_NOPE - MLA_ROPE,), F32)
    qga = jnp.concatenate([qn, z32])[None]
    qgb = jnp.concatenate([jnp.zeros((MLA_NOPE,), F32), qn[MLA_NOPE + half:], qn[MLA_NOPE:MLA_NOPE + half], z32])[None]
    kgk = jnp.concatenate([kn[:MLA_NOPE], jnp.zeros((HEAD_SLAB - MLA_NOPE,), F32)])[None]
    gkr = kn[MLA_NOPE:][None]

    blk = np.zeros((HEAD_SLAB, HEAD_SLAB), np.float32)
    blk[:MLA_NOPE, :MLA_NOPE] = 1.0 / MLA_NOPE
    mk = np.kron(np.eye(2), blk)
    blk_q = blk.copy()
    blk_q[MLA_NOPE:MLA_NOPE + MLA_ROPE, MLA_NOPE:MLA_NOPE + MLA_ROPE] = 1.0 / MLA_ROPE
    mq = np.kron(np.eye(2), blk_q)
    m64 = np.kron(np.eye(4), np.full((64, 64), 1.0 / 64, np.float32))
    tile = np.zeros((MLA_ROPE, 2 * HEAD_SLAB), np.float32)
    for hd in range(2):
        tile[np.arange(MLA_ROPE), HEAD_SLAB * hd + MLA_NOPE + np.arange(MLA_ROPE)] = 1.0

    def norm_bound(g):
        return jnp.sqrt(MLA_NOPE * jnp.max(jnp.abs(g[:MLA_NOPE])) ** 2 + MLA_ROPE * jnp.max(jnp.abs(g[MLA_NOPE:])) ** 2)
    logit_bound = (MLA_SCALE * norm_bound(qn) * norm_bound(kn)).reshape(1).astype(F32)

    return dict(
        logit_bound=logit_bound, nm=norm_mix[l][None], win=win, cqn=mla_cq_norm[l][None], ckvn=mla_ckv_norm[l][None], wuq=wuq, wuk=wuk,
        wuv=wuv, wuvt=wuvt, qga=qga, qgb=qgb, mq=jnp.asarray(mq, BF16), mk=jnp.asarray(mk, BF16), kgk=kgk, gkr=gkr,
        tile=jnp.asarray(tile, BF16), lbl=hg_lb_logits, wa2=gla_w_a2[l].astype(BF16), ba=gla_b_a[l][None],
        mon=mla_out_norm[l][None], hon=jnp.tile(hg_out_norm[l], HG_HEADS)[None],
        gon=jnp.tile(gla_out_norm[l], GLA_HEADS)[None], nf=norm_ffn[l][None], m64=jnp.asarray(m64, BF16),
        wo=w_o[l].astype(BF16), wfi=w_ffn_in[l].astype(BF16), wfo=w_ffn_out[l].astype(BF16))


def _rope_tables(pos, reps):
    inv = (np.float32(ROPE_THETA) ** (-(np.arange(0, MLA_ROPE, 2, dtype=np.float32) / np.float32(MLA_ROPE)))).astype(np.float32)
    ang = pos.astype(np.float32)[:, None] * inv[None, :]
    cos, sin = np.cos(ang).astype(np.float32), np.sin(ang).astype(np.float32)
    n = pos.shape[0]
    pad = HEAD_SLAB - MLA_NOPE - MLA_ROPE
    cosq = np.concatenate([np.ones((n, MLA_NOPE), np.float32), cos, cos, np.zeros((n, pad), np.float32)], axis=1)
    sinq = np.concatenate([np.zeros((n, MLA_NOPE), np.float32), -sin, sin, np.zeros((n, pad), np.float32)], axis=1)
    cosk = np.concatenate([cos, cos], axis=1)
    sink = np.concatenate([-sin, sin], axis=1)
    tabs = dict(cosq=cosq, sinq=sinq, cosk=cosk, sink=sink)
    return {k: jnp.asarray(np.tile(v, (reps, 1))) for k, v in tabs.items()}


def _block_diag_state(state):
    b, hh, dk, dv = state.shape
    eye = jnp.eye(hh, dtype=state.dtype)
    return (state[:, :, :, None, :] * eye[None, :, None, :, None]).reshape(b, hh * dk, hh * dv)


def _head_states(s_bd, hh):
    dk, dv = s_bd.shape[1] // hh, s_bd.shape[2] // hh
    return jnp.stack([s_bd[:, hd * dk:(hd + 1) * dk, hd * dv:(hd + 1) * dv] for hd in range(hh)], axis=1)


def _group_layer(x, mods, lw, tabs, layer, bb, tt, rtt, s_hg, s_gla, attn_fn, transpose_v):
    sh1, sc1, g1, sh2, sc2, g2 = mods
    ckv, kr, q, k, v, hrec, grec, gate = _inproj_call(x, sh1, sc1, lw, tabs, bb, tt, layer, transpose_v)
    om = attn_fn(q, k, v)
    oh, shg = _recur_call(hrec, s_hg, HG_HEADS * HG_DK, rtt, "recur_hgrn")
    og, sgl = _recur_call(grec, s_gla, GLA_HEADS * GLA_DK, rtt, "recur_gla")
    y = _outffn_call(x, om, oh, og, gate, (g1, sh2, sc2, g2), lw, bb, tt)
    return y, ckv, kr, _head_states(shg, HG_HEADS), _head_states(sgl, GLA_HEADS)


def kernel(x_prompt, x_sample, c_prompt, c_sample, cache_mla_ckv, cache_mla_krope, state_hgrn, state_gla, w_mod, b_mod, norm_mix, norm_ffn, w_in, mla_cq_norm, mla_ckv_norm, mla_w_uq, mla_w_uk, mla_w_uv, mla_q_norm, mla_k_norm, mla_out_norm, hg_lb_logits, hg_out_norm, gla_w_a2, gla_b_a, gla_out_norm, w_o, w_ffn_in, w_ffn_out):
    bp, t_p, d = x_prompt.shape
    bs, t_s, _ = x_sample.shape
    depth = w_mod.shape[0]
    tm = 512
    tt_p = min(tm, t_p)
    bb_s = max(1, min(bs, tm // t_s))
    rtt_p = min(256, t_p)
    rtt_s = min(128, t_s)
    attn_tile = min(512, t_p)
    tkc = min(1024, cache_mla_ckv.shape[2])

    tabs_p = _rope_tables(np.arange(t_p), 1)
    tabs_s = _rope_tables(PAST_LEN + np.arange(t_s), bb_s)

    n_c = bp + bs
    c_rows = -(-n_c // 16) * 16
    c_pad = jnp.concatenate([c_prompt, c_sample, jnp.zeros((c_rows - n_c, d), c_prompt.dtype)], axis=0)

    cache_kr_t = jnp.swapaxes(cache_mla_krope, 2, 3)
    xp, xs = x_prompt, x_sample
    outs = [[] for _ in range(8)]
    for l in range(depth):
        lw = _layer_weights(l, w_in, norm_mix, norm_ffn, mla_cq_norm, mla_ckv_norm, mla_w_uq, mla_w_uk, mla_w_uv,
                            mla_q_norm, mla_k_norm, mla_out_norm, hg_lb_logits, hg_out_norm, gla_w_a2, gla_b_a,
                            gla_out_norm, w_o, w_ffn_in, w_ffn_out)
        mod = _mod_call(c_pad, w_mod, b_mod[:, None, :], l)
        mods_p = [mod[0:bp, i * d:(i + 1) * d][:, None, :] for i in range(6)]
        mods_s = [mod[bp:n_c, i * d:(i + 1) * d][:, None, :] for i in range(6)]

        zero_hg = jnp.zeros((bp, HG_HEADS * HG_DK, HG_HEADS * HG_DV), F32)
        zero_gla = jnp.zeros((bp, GLA_HEADS * GLA_DK, GLA_HEADS * GLA_DV), F32)
        xp, a, b_, c_, d_ = _group_layer(xp, mods_p, lw, tabs_p, l, 1, tt_p, rtt_p, zero_hg, zero_gla,
                                         lambda q, k, vt: _attn_prompt_call(lw["logit_bound"], q, k, vt, attn_tile), True)
        for lst, val in zip(outs[0:4], (a, b_, c_, d_)):
            lst.append(val)
        xs, a, b_, c_, d_ = _group_layer(xs, mods_s, lw, tabs_s, l, bb_s, t_s, rtt_s,
                                         _block_diag_state(state_hgrn[l]), _block_diag_state(state_gla[l]),
                                         lambda q, k, v: _attn_sample_call(lw["logit_bound"], q, k, v, cache_mla_ckv,
                                                                           cache_kr_t, lw, l, tkc), False)
        for lst, val in zip(outs[4:8], (a, b_, c_, d_)):
            lst.append(val)

    return (xp, xs) + tuple(jnp.stack(o) for o in outs)
```

```python
import functools

import numpy as np
import jax
import jax.numpy as jnp
from jax import lax
from jax.experimental import pallas as pl
from jax.experimental.pallas import tpu as pltpu

F32, BF16 = jnp.float32, jnp.bfloat16

D_MODEL = 1024
PAST_LEN = 4096
CHUNK = 64
EPS = 1e-6
NEG_INF = -1e30
ROPE_THETA = 10000.0
MLA_HEADS = 8
MLA_NOPE = 64
MLA_ROPE = 32
MLA_V = 64
MLA_Q_RANK = 256
MLA_KV_RANK = 256
MLA_WIDTH = MLA_HEADS * MLA_V
MLA_SCALE = (MLA_NOPE + MLA_ROPE) ** -0.5
LOG2E = 1.4426950408889634
HG_HEADS = 4
HG_DK = 64
HG_DV = 64
GLA_HEADS = 4
GLA_DK = 32
GLA_DV = 64
GLA_RANK = 16
GLA_TAU = 16.0
D_FF = 2816

LANES = 128
HEAD_SLAB = 128
SUB = 16
MAX_CENTRED_EXPONENT = 110.0
ZCOLS = 2432
VMEM_LIMIT = 56 * 1024 * 1024

_Z = dict(cq=0, ckv=256, hq=512, hf=768, hi=1024, hg=1280, gq=1536, gk=1664, gv=1792, gg=2048, misc=2304)


def _bdot(a, b):
    return jnp.dot(a.astype(BF16), b.astype(BF16), preferred_element_type=F32)


def _dot_nt(a, b):
    return lax.dot_general(a, b, (((1,), (1,)), ((), ())), preferred_element_type=F32)


def _dot_tn(a, b):
    return lax.dot_general(a, b, (((0,), (0,)), ((), ())), preferred_element_type=F32)


def _sigmoid(x):
    return 1.0 / (1.0 + jnp.exp(-x))


def _silu(x):
    return x * _sigmoid(x)


def _log_sigmoid(x):
    return jnp.minimum(x, 0.0) - jnp.log1p(jnp.exp(-jnp.abs(x)))


def _rms(x):
    return x * lax.rsqrt(jnp.mean(x * x, axis=-1, keepdims=True) + EPS)


def _params(sem):
    return pltpu.CompilerParams(dimension_semantics=sem, vmem_limit_bytes=VMEM_LIMIT)


def _const_spec(shape):
    nd = len(shape)
    return pl.BlockSpec(shape, lambda *_: (0,) * nd)


def _mod_body(c_ref, w_ref, b_ref, o_ref):
    o_ref[...] = _bdot(_silu(c_ref[...]), w_ref[0]) + b_ref[0]


def _mod_call(c_pad, w_mod, b_mod, layer):
    rows, d = c_pad.shape
    n = w_mod.shape[2]
    tn = 1024
    return pl.pallas_call(
        _mod_body,
        grid=(n // tn,),
        in_specs=[pl.BlockSpec((rows, d), lambda j: (0, 0)),
                  pl.BlockSpec((1, d, tn), lambda j: (layer, 0, j)),
                  pl.BlockSpec((1, 1, tn), lambda j: (layer, 0, j))],
        out_specs=pl.BlockSpec((rows, tn), lambda j: (0, j)),
        out_shape=jax.ShapeDtypeStruct((rows, n), F32),
        compiler_params=_params(("arbitrary",)),
        name="mod",
    )(c_pad, w_mod, b_mod)


def _mla_kv(ckvn_bf, kro_bf, w_uk_ref, w_uv_ref, mk_ref, kgk_ref, tile_ref, transpose_v=False, kro_transposed=False):
    place = _dot_tn if kro_transposed else functools.partial(jnp.dot, preferred_element_type=F32)
    krt = place(kro_bf, tile_ref[...])
    kgk2 = jnp.concatenate([kgk_ref[...], kgk_ref[...]], axis=-1)
    parts = []
    for hp in range(MLA_HEADS // 2):
        sl = slice(2 * HEAD_SLAB * hp, 2 * HEAD_SLAB * (hp + 1))
        sk = jnp.dot(ckvn_bf, w_uk_ref[:, sl], preferred_element_type=F32)
        msk = jnp.dot((sk * sk).astype(BF16), mk_ref[...], preferred_element_type=F32)
        parts.append((sk * lax.rsqrt(msk + EPS) * kgk2 + krt).astype(BF16))
    k = jnp.concatenate(parts, axis=-1)
    if transpose_v:
        v = _dot_nt(w_uv_ref[...], ckvn_bf).astype(BF16)
    else:
        v = jnp.dot(ckvn_bf, w_uv_ref[...], preferred_element_type=F32).astype(BF16)
    return k, v


def _inproj_body(x_ref, sh_ref, sc_ref, nm_ref, win_ref, cqn_ref, ckvn_ref, wuq_ref, wuk_ref, wuv_ref,
                 qga_ref, qgb_ref, cosq_ref, sinq_ref, mq_ref, mk_ref, kgk_ref, gkr_ref, cosk_ref, sink_ref,
                 tile_ref, lbl_ref, wa2_ref, ba_ref,
                 ckv_o, kr_o, q_o, k_o, v_o, hrec_o, grec_o, gate_o, *, layer, transpose_v):
    bb, tt, d = x_ref.shape
    tm = bb * tt
    x = x_ref[...]
    h = _rms(x) * nm_ref[...] * (1.0 + sc_ref[...]) + sh_ref[...]
    z = jnp.dot(h.reshape(tm, d).astype(BF16), win_ref[...], preferred_element_type=F32)

    def seg(name, w):
        return z[:, _Z[name]:_Z[name] + w]

    cqn = _rms(seg("cq", MLA_Q_RANK)) * cqn_ref[...]
    ckvn = _rms(seg("ckv", MLA_KV_RANK)) * ckvn_ref[...]
    ckv_o[...] = ckvn.reshape(bb, tt, MLA_KV_RANK)
    misc = seg("misc", LANES)

    qab = jnp.dot(cqn.astype(BF16), wuq_ref[...], preferred_element_type=F32)
    width = MLA_HEADS * HEAD_SLAB
    ca = cosq_ref[...] * qga_ref[...]
    sb = sinq_ref[...] * qgb_ref[...]
    ca2 = jnp.concatenate([ca, ca], axis=-1)
    sb2 = jnp.concatenate([sb, sb], axis=-1)
    for hp in range(MLA_HEADS // 2):
        sl = slice(2 * HEAD_SLAB * hp, 2 * HEAD_SLAB * (hp + 1))
        qa = qab[:, sl]
        qb = qab[:, width + sl.start:width + sl.stop]
        msq = jnp.dot((qa * qa).astype(BF16), mq_ref[...], preferred_element_type=F32)
        qo = lax.rsqrt(msq + EPS) * (qa * ca2 + qb * sb2) * (MLA_SCALE * LOG2E)
        q_o[:, :, sl] = qo.astype(BF16).reshape(bb, tt, 2 * HEAD_SLAB)

    kr = misc[:, 0:MLA_ROPE]
    krn = _rms(kr) * gkr_ref[...]
    half = MLA_ROPE // 2
    swapped = jnp.concatenate([krn[:, half:], krn[:, :half]], axis=-1)
    kro = krn * cosk_ref[...] + swapped * sink_ref[...]
    kr_o[...] = kro.reshape(bb, tt, MLA_ROPE)
    k, v = _mla_kv(ckvn.astype(BF16), kro.astype(BF16), wuk_ref, wuv_ref, mk_ref, kgk_ref, tile_ref, transpose_v)
    k_o[...] = k.reshape(bb, tt, width)
    if transpose_v:
        v_o[0] = v
    else:
        v_o[...] = v.reshape(bb, tt, MLA_WIDTH)

    lg = lbl_ref[...]
    e = jnp.exp(lg - jnp.max(lg, axis=0, keepdims=True))
    p = e / jnp.sum(e, axis=0, keepdims=True)
    lb = jnp.sum(p[0:layer + 1], axis=0, keepdims=True) - p[0:1]
    f = lb + (1.0 - lb) * _sigmoid(seg("hf", 256))
    hrec_o[:, :, 0:256] = seg("hq", 256).reshape(bb, tt, 256)
    hrec_o[:, :, 256:512] = (1.0 - f).reshape(bb, tt, 256)
    hrec_o[:, :, 512:768] = jnp.log(f).reshape(bb, tt, 256)
    hrec_o[:, :, 768:1024] = seg("hi", 256).reshape(bb, tt, 256)

    ga = misc[:, MLA_ROPE:MLA_ROPE + GLA_RANK]
    apre = jnp.dot(ga.astype(BF16), wa2_ref[...], preferred_element_type=F32) + ba_ref[...]
    grec_o[:, :, 0:128] = (seg("gq", 128) * (GLA_DK ** -0.5)).reshape(bb, tt, 128)
    grec_o[:, :, 128:256] = seg("gk", 128).reshape(bb, tt, 128)
    grec_o[:, :, 256:384] = (_log_sigmoid(apre) / GLA_TAU).reshape(bb, tt, 128)
    grec_o[:, :, 384:640] = seg("gv", 256).reshape(bb, tt, 256)

    gate_o[:, :, 0:256] = _silu(seg("hg", 256)).reshape(bb, tt, 256)
    gate_o[:, :, 256:512] = _silu(seg("gg", 256)).reshape(bb, tt, 256)


def _inproj_call(x, sh, sc, lw, tabs, bb, tt, layer, transpose_v):
    b, t, d = x.shape
    tm = bb * tt
    grid = (b // bb, t // tt)
    tok = lambda w: pl.BlockSpec((bb, tt, w), lambda i, j: (i, j, 0))
    modspec = pl.BlockSpec((bb, 1, d), lambda i, j: (i, 0, 0))
    tab = lambda w: pl.BlockSpec((tm, w), lambda i, j: (j, 0))
    assert bb == 1 or not transpose_v
    consts = [lw["nm"], lw["win"], lw["cqn"], lw["ckvn"], lw["wuq"], lw["wuk"], lw["wuvt" if transpose_v else "wuv"],
              lw["qga"], lw["qgb"]]
    consts2 = [lw["mq"], lw["mk"], lw["kgk"], lw["gkr"]]
    consts3 = [lw["tile"], lw["lbl"], lw["wa2"], lw["ba"]]
    in_specs = ([tok(d), modspec, modspec] + [_const_spec(a.shape) for a in consts]
                + [tab(LANES), tab(LANES)] + [_const_spec(a.shape) for a in consts2]
                + [tab(MLA_ROPE), tab(MLA_ROPE)] + [_const_spec(a.shape) for a in consts3])
    widths = [(MLA_KV_RANK, F32), (MLA_ROPE, F32), (MLA_HEADS * HEAD_SLAB, BF16), (MLA_HEADS * HEAD_SLAB, BF16),
              (MLA_WIDTH, BF16), (1024, F32), (640, F32), (512, F32)]
    out_specs = [tok(w) for w, _ in widths]
    out_shape = [jax.ShapeDtypeStruct((b, t, w), dt) for w, dt in widths]
    if transpose_v:
        out_specs[4] = pl.BlockSpec((1, MLA_WIDTH, tt), lambda i, j: (i, 0, j))
        out_shape[4] = jax.ShapeDtypeStruct((b, MLA_WIDTH, t), BF16)
    return pl.pallas_call(
        functools.partial(_inproj_body, layer=layer, transpose_v=transpose_v),
        grid=grid,
        in_specs=in_specs,
        out_specs=out_specs,
        out_shape=out_shape,
        compiler_params=_params(("arbitrary", "arbitrary")),
        name="inproj",
    )(x, sh, sc, *consts, tabs["cosq"], tabs["sinq"], *consts2, tabs["cosk"], tabs["sink"], *consts3)


MAX_UNSHIFTED_LOGIT = 60.0
SMALL_SCORE_TILE = 64 * 1024


def _attend_bounded(q, k, v, lf_ref, accf_ref, bias):
    tq, tk = q.shape[0], k.shape[0]

    def scores(hd):
        sl = slice(HEAD_SLAB * hd, HEAD_SLAB * (hd + 1))
        s = _dot_nt(q[:, sl], k[:, sl])
        return s if bias is None else s + bias

    def accumulate(hd, s):
        p = jnp.exp2(s)
        if tk % LANES == 0:
            part = p[:, 0:LANES]
            for c in range(1, tk // LANES):
                part = part + p[:, LANES * c:LANES * (c + 1)]
        else:
            lane0 = lax.broadcasted_iota(jnp.int32, (tq, LANES), 1) == 0
            part = jnp.where(lane0, jnp.sum(p, axis=-1, keepdims=True), 0.0)
        lf_ref[hd] += part
        hp = hd // 2
        accf_ref[hd] += jnp.dot(p.astype(BF16), v[:, LANES * hp:LANES * (hp + 1)], preferred_element_type=F32)

    if tq * tk <= SMALL_SCORE_TILE:
        ss = [scores(hd) for hd in range(MLA_HEADS)]
        for hd in range(MLA_HEADS):
            accumulate(hd, ss[hd])
    else:
        for hd in range(MLA_HEADS):
            accumulate(hd, scores(hd))


def _finish_bounded(lf_ref, accf_ref):
    tq = accf_ref.shape[1]
    low = lax.broadcasted_iota(jnp.int32, (tq, LANES), 1) < MLA_V
    outs = []
    for hp in range(MLA_HEADS // 2):
        la = jnp.sum(lf_ref[2 * hp], axis=-1, keepdims=True)
        lb = jnp.sum(lf_ref[2 * hp + 1], axis=-1, keepdims=True)
        outs.append(jnp.where(low, accf_ref[2 * hp] / la, accf_ref[2 * hp + 1] / lb))
    return jnp.concatenate(outs, axis=-1)


def _attend_online(q, k, v, m_ref, l_ref, acc_ref, bias):
    tq = q.shape[0]
    low = lax.broadcasted_iota(jnp.int32, (tq, LANES), 1) < MLA_V
    for hp in range(MLA_HEADS // 2):
        vp = v[:, LANES * hp:LANES * (hp + 1)]
        pvs, alphas = [], []
        for e in range(2):
            hd = 2 * hp + e
            sl = slice(HEAD_SLAB * hd, HEAD_SLAB * (hd + 1))
            s = _dot_nt(q[:, sl], k[:, sl])
            if bias is not None:
                s = s + bias
            m_prev = m_ref[hd]
            m_new = jnp.maximum(m_prev, jnp.max(s, axis=-1, keepdims=True))
            p = jnp.exp2(s - m_new)
            alpha = jnp.exp2(m_prev - m_new)
            l_ref[hd] = alpha * l_ref[hd] + jnp.sum(p, axis=-1, keepdims=True)
            m_ref[hd] = m_new
            pvs.append(jnp.dot(p.astype(BF16), vp, preferred_element_type=F32))
            alphas.append(alpha)
        sl2 = slice(LANES * hp, LANES * (hp + 1))
        acc_ref[:, sl2] = jnp.where(low, alphas[0], alphas[1]) * acc_ref[:, sl2] + jnp.where(low, pvs[0], pvs[1])


def _finish_online(l_ref, acc_ref):
    tq = acc_ref.shape[0]
    low = lax.broadcasted_iota(jnp.int32, (tq, LANES), 1) < MLA_V
    outs = []
    for hp in range(MLA_HEADS // 2):
        l2 = jnp.where(low, l_ref[2 * hp], l_ref[2 * hp + 1])
        outs.append(acc_ref[:, LANES * hp:LANES * (hp + 1)] / l2)
    return jnp.concatenate(outs, axis=-1)


def _softmax_scratch(tq):
    return [pltpu.VMEM((MLA_HEADS, tq, 1), F32), pltpu.VMEM((MLA_HEADS, tq, 1), F32), pltpu.VMEM((tq, MLA_WIDTH), F32),
            pltpu.VMEM((MLA_HEADS, tq, LANES), F32), pltpu.VMEM((MLA_HEADS, tq, LANES), F32)]


def _softmax_init(m_ref, l_ref, acc_ref, lf_ref, accf_ref):
    m_ref[...] = jnp.full(m_ref.shape, NEG_INF, F32)
    for ref in (l_ref, acc_ref, lf_ref, accf_ref):
        ref[...] = jnp.zeros(ref.shape, F32)


def _attend(bounded, cond, get_qkv, scratch, bias_ref):
    m_ref, l_ref, acc_ref, lf_ref, accf_ref = scratch

    @pl.when(jnp.logical_and(cond, bounded))
    def _():
        _attend_bounded(*get_qkv(), lf_ref, accf_ref, None if bias_ref is None else bias_ref[...])

    @pl.when(jnp.logical_and(cond, jnp.logical_not(bounded)))
    def _():
        _attend_online(*get_qkv(), m_ref, l_ref, acc_ref, None if bias_ref is None else bias_ref[...])


def _softmax_finish(bounded, cond, o_ref, scratch):
    m_ref, l_ref, acc_ref, lf_ref, accf_ref = scratch

    @pl.when(jnp.logical_and(cond, bounded))
    def _():
        o_ref[0] = _finish_bounded(lf_ref, accf_ref)

    @pl.when(jnp.logical_and(cond, jnp.logical_not(bounded)))
    def _():
        o_ref[0] = _finish_online(l_ref, acc_ref)


def _attend_t_bounded(q, k, vt, lf_ref, acc_ref, bias_t):
    tq, tk = q.shape[0], k.shape[0]

    def scores_t(hd):
        sl = slice(HEAD_SLAB * hd, HEAD_SLAB * (hd + 1))
        st = _dot_nt(k[:, sl], q[:, sl])
        return st if bias_t is None else st + bias_t

    st_next = scores_t(0)
    for hd in range(MLA_HEADS):
        st = st_next
        if hd + 1 < MLA_HEADS:
            st_next = scores_t(hd + 1)
        pt = jnp.exp2(st)
        lf_ref[hd] += jnp.sum(pt.reshape(tk // 8, 8, tq), axis=0)
        rows = slice(MLA_V * hd, MLA_V * (hd + 1))
        acc_ref[rows, :] += jnp.dot(vt[rows, :], pt.astype(BF16), preferred_element_type=F32)


def _attend_t_online(q, k, vt, m_ref, l_ref, acc_ref, bias_t):
    for hd in range(MLA_HEADS):
        sl = slice(HEAD_SLAB * hd, HEAD_SLAB * (hd + 1))
        st = _dot_nt(k[:, sl], q[:, sl])
        if bias_t is not None:
            st = st + bias_t
        m_prev = m_ref[hd]
        m_new = jnp.maximum(m_prev, jnp.max(st, axis=0, keepdims=True))
        pt = jnp.exp2(st - m_new)
        alpha = jnp.exp2(m_prev - m_new)
        l_ref[hd] = alpha * l_ref[hd] + jnp.sum(pt, axis=0, keepdims=True)
        m_ref[hd] = m_new
        rows = slice(MLA_V * hd, MLA_V * (hd + 1))
        acc_ref[rows, :] = alpha * acc_ref[rows, :] + jnp.dot(vt[rows, :], pt.astype(BF16), preferred_element_type=F32)


def _finish_t(l_of_head, acc_ref):
    outs = []
    for hd in range(MLA_HEADS):
        outs.append(acc_ref[MLA_V * hd:MLA_V * (hd + 1), :] / l_of_head(hd))
    return jnp.concatenate(outs, axis=0).T


def _attn_prompt_body(qi_ref, ki_ref, bound_ref, bias_ref, q_ref, k_ref, vt_ref, o_ref, m_ref, l_ref, lf_ref, acc_ref):
    step = pl.program_id(1)
    i = qi_ref[step]
    j = ki_ref[step]
    bounded = bound_ref[0] <= MAX_UNSHIFTED_LOGIT
    unbounded = jnp.logical_not(bounded)
    diag = j == i

    @pl.when(j == 0)
    def _():
        m_ref[...] = jnp.full(m_ref.shape, NEG_INF, F32)
        for ref in (l_ref, lf_ref, acc_ref):
            ref[...] = jnp.zeros(ref.shape, F32)

    for on_diag in (False, True):
        cond = diag if on_diag else j < i

        @pl.when(jnp.logical_and(cond, bounded))
        def _():
            _attend_t_bounded(q_ref[0], k_ref[0], vt_ref[0], lf_ref, acc_ref, bias_ref[...] if on_diag else None)

        @pl.when(jnp.logical_and(cond, unbounded))
        def _():
            _attend_t_online(q_ref[0], k_ref[0], vt_ref[0], m_ref, l_ref, acc_ref, bias_ref[...] if on_diag else None)

    @pl.when(jnp.logical_and(diag, bounded))
    def _():
        o_ref[0] = _finish_t(lambda hd: jnp.sum(lf_ref[hd], axis=0, keepdims=True), acc_ref)

    @pl.when(jnp.logical_and(diag, unbounded))
    def _():
        o_ref[0] = _finish_t(lambda hd: l_ref[hd], acc_ref)


def _attn_prompt_call(bound, q, k, vt, tile):
    b, t, _ = q.shape
    n = t // tile
    pairs = [(i, j) for i in range(n) for j in range(i + 1)]
    qi = jnp.asarray([p[0] for p in pairs], jnp.int32)
    ki = jnp.asarray([p[1] for p in pairs], jnp.int32)
    q_map = lambda bi, st, qi_ref, ki_ref: (bi, qi_ref[st], 0)
    k_map = lambda bi, st, qi_ref, ki_ref: (bi, ki_ref[st], 0)
    vt_map = lambda bi, st, qi_ref, ki_ref: (bi, 0, ki_ref[st])
    chunk_of = np.arange(tile) // CHUNK
    bias_t = jnp.asarray(np.where(chunk_of[:, None] <= chunk_of[None, :], 0.0, NEG_INF), F32)
    grid_spec = pltpu.PrefetchScalarGridSpec(
        num_scalar_prefetch=2,
        grid=(b, len(pairs)),
        in_specs=[pl.BlockSpec(memory_space=pltpu.SMEM), pl.BlockSpec((tile, tile), lambda *_: (0, 0)),
                  pl.BlockSpec((1, tile, q.shape[2]), q_map),
                  pl.BlockSpec((1, tile, k.shape[2]), k_map),
                  pl.BlockSpec((1, MLA_WIDTH, tile), vt_map)],
        out_specs=pl.BlockSpec((1, tile, MLA_WIDTH), q_map),
        scratch_shapes=[pltpu.VMEM((MLA_HEADS, 1, tile), F32), pltpu.VMEM((MLA_HEADS, 1, tile), F32),
                        pltpu.VMEM((MLA_HEADS, 8, tile), F32), pltpu.VMEM((MLA_WIDTH, tile), F32)])
    return pl.pallas_call(
        _attn_prompt_body,
        grid_spec=grid_spec,
        out_shape=jax.ShapeDtypeStruct((b, t, MLA_WIDTH), F32),
        compiler_params=_params(("arbitrary", "arbitrary")),
        name="attn_prompt",
    )(qi, ki, bound, bias_t, q, k, vt)


def _attn_sample_body(bound_ref, q_ref, kn_ref, vn_ref, ckv_ref, kr_ref, wuk_ref, wuv_ref, mk_ref, kgk_ref, tile_ref,
                      o_ref, *scratch):
    j = pl.program_id(1)
    bounded = bound_ref[0] <= MAX_UNSHIFTED_LOGIT
    always = j >= 0

    @pl.when(j == 0)
    def _():
        _softmax_init(*scratch)

    _attend(bounded, j == 0, lambda: (q_ref[0], kn_ref[0], vn_ref[0]), scratch, None)

    def past_qkv():
        k, v = _mla_kv(ckv_ref[0, 0].astype(BF16), kr_ref[0, 0].astype(BF16), wuk_ref, wuv_ref, mk_ref, kgk_ref,
                       tile_ref, kro_transposed=True)
        return q_ref[0], k, v

    _attend(bounded, always, past_qkv, scratch, None)
    _softmax_finish(bounded, j == pl.num_programs(1) - 1, o_ref, scratch)


def _attn_sample_call(bound, q, k_new, v_new, cache_ckv, cache_kr_t, lw, layer, tkc):
    b, t, _ = q.shape
    past = cache_ckv.shape[2]
    consts = [lw["wuk"], lw["wuv"], lw["mk"], lw["kgk"], lw["tile"]]
    new = lambda w: pl.BlockSpec((1, t, w), lambda bi, j: (bi, 0, 0))
    return pl.pallas_call(
        _attn_sample_body,
        grid=(b, past // tkc),
        in_specs=[pl.BlockSpec(memory_space=pltpu.SMEM), new(q.shape[2]), new(k_new.shape[2]), new(v_new.shape[2]),
                  pl.BlockSpec((1, 1, tkc, MLA_KV_RANK), lambda bi, j: (layer, bi, j, 0)),
                  pl.BlockSpec((1, 1, MLA_ROPE, tkc), lambda bi, j: (layer, bi, 0, j))]
                 + [_const_spec(a.shape) for a in consts],
        out_specs=new(MLA_WIDTH),
        out_shape=jax.ShapeDtypeStruct((b, t, MLA_WIDTH), F32),
        scratch_shapes=_softmax_scratch(t),
        compiler_params=_params(("arbitrary", "arbitrary")),
        name="attn_sample",
    )(bound, q, k_new, v_new, cache_ckv, cache_kr_t, *consts)


def _cumsum_matrix(tt, blk):
    r = np.arange(tt)
    return (((r[:, None] // blk) == (r[None, :] // blk)) & (r[None, :] <= r[:, None])).astype(np.float32)


def _block_cumsum(g, lmat_ref, blk):
    tt, c = g.shape
    lmat = lmat_ref[...]
    g1 = g.astype(BF16)
    r1 = g - g1.astype(F32)
    g2 = r1.astype(BF16)
    g3 = (r1 - g2.astype(F32)).astype(BF16)
    bb = jnp.dot(lmat, jnp.concatenate([g1, g2, g3], axis=-1), preferred_element_type=F32)
    b = (bb[:, 2 * c:3 * c] + bb[:, c:2 * c]) + bb[:, 0:c]
    last = [jnp.broadcast_to(b[blk * (n + 1) - 1:blk * (n + 1)], (blk, c)) for n in range(tt // blk)]
    return b, jnp.concatenate(last, axis=0)


def _state_chain(blk, qe, ke, vb, dec, o_in_block, bd_ref, o_ref, s_ref):
    tt = qe.shape[0]
    nb = tt // blk
    picked = [dec[blk * n:blk * n + 1] for n in range(nb)]
    if nb % 8:
        picked.append(jnp.zeros((8 - nb % 8, dec.shape[1]), F32))
    dsel = jnp.concatenate(picked, axis=0).T
    us = [_dot_tn(ke[blk * n:blk * (n + 1)], vb[blk * n:blk * (n + 1)]) * bd_ref[...] for n in range(nb)]
    s = s_ref[...]
    for n in range(nb):
        rows = slice(blk * n, blk * (n + 1))
        o_ref[0, rows, :] = jnp.dot(qe[rows], s.astype(BF16), preferred_element_type=F32) + o_in_block(n)
        s = dsel[:, n:n + 1] * s + us[n]
    s_ref[...] = s


def _recur_factorised(q, k, v, b2, bt2, chunk, bd_ref, o_ref, s_ref, hk):
    heads = 4
    dk = hk // heads
    tt = q.shape[0]
    ref = 0.5 * bt2
    qe = (q * jnp.exp2(b2)).astype(BF16)
    qh = (q * jnp.exp2(b2 - ref)).astype(BF16)
    kh = (k * jnp.exp2(ref - b2)).astype(BF16)
    ke = (k * jnp.exp2(bt2 - b2)).astype(BF16)
    dec = jnp.exp2(bt2)
    vb = v.astype(BF16)
    rows_k = lax.broadcasted_iota(jnp.int32, (heads * chunk, hk), 0) // chunk
    same_k = rows_k == lax.broadcasted_iota(jnp.int32, (heads * chunk, hk), 1) // dk
    rows_v = lax.broadcasted_iota(jnp.int32, (heads * chunk, 256), 0) // chunk
    same_v = rows_v == lax.broadcasted_iota(jnp.int32, (heads * chunk, 256), 1) // 64
    ii = lax.broadcasted_iota(jnp.int32, (chunk, heads * chunk), 0)
    jj = lax.broadcasted_iota(jnp.int32, (chunk, heads * chunk), 1) % chunk
    causal = jj <= ii
    chunks = [slice(chunk * n, chunk * (n + 1)) for n in range(tt // chunk)]
    scs = []
    for rows in chunks:
        kbd = jnp.where(same_k, jnp.concatenate([kh[rows]] * heads, axis=0), jnp.zeros(same_k.shape, BF16))
        scs.append(_dot_nt(qh[rows], kbd))
    ods = []
    for rows, sc in zip(chunks, scs):
        vbd = jnp.where(same_v, jnp.concatenate([vb[rows]] * heads, axis=0), jnp.zeros(same_v.shape, BF16))
        ods.append(jnp.dot(jnp.where(causal, sc, 0.0).astype(BF16), vbd, preferred_element_type=F32))
    _state_chain(chunk, qe, ke, vb, dec, lambda n: ods[n], bd_ref, o_ref, s_ref)


def _recur_pairwise(q, k, g, v, lsub_ref, e2_ref, sel_ref, bd_ref, o_ref, s_ref):
    tt = q.shape[0]
    nb = tt // SUB
    b, btot = _block_cumsum(g, lsub_ref, SUB)
    b2 = b * LOG2E
    bt2 = btot * LOG2E
    qe = (q * jnp.exp2(b2)).astype(BF16)
    ke = (k * jnp.exp2(bt2 - b2)).astype(BF16)
    dec = jnp.exp2(bt2)
    vb = v.astype(BF16)
    xs = []
    for n in range(nb):
        r0 = SUB * n
        bj, kj = b2[r0:r0 + SUB], k[r0:r0 + SUB]
        for i0 in range(SUB):
            bi = b2[r0 + i0:r0 + i0 + 1]
            qi = q[r0 + i0:r0 + i0 + 1]
            xs.append((jnp.exp2(jnp.minimum(bi - bj, 0.0)) * (kj * qi)).astype(BF16))
    x = jnp.concatenate(xs, axis=0)
    r = jnp.dot(x, e2_ref[...], preferred_element_type=F32)
    y = (r.reshape(nb, SUB, SUB, 256) * v.reshape(nb, 1, SUB, 256)).reshape(nb * SUB * SUB, 256).astype(BF16)
    ods = [jnp.dot(sel_ref[...], y[SUB * SUB * n:SUB * SUB * (n + 1)], preferred_element_type=F32) for n in range(nb)]
    _state_chain(SUB, qe, ke, vb, dec, lambda n: ods[n], bd_ref, o_ref, s_ref)


def _recur_body(rec_ref, s0_ref, lchunk_ref, lsub_ref, e2_ref, sel_ref, bd_ref, o_ref, s_o, s_ref, *, hk):
    t = pl.program_id(1)

    @pl.when(t == 0)
    def _():
        s_ref[...] = s0_ref[0]

    rec = rec_ref[0]
    q = rec[:, 0:hk]
    k = rec[:, hk:2 * hk]
    g = rec[:, 2 * hk:3 * hk]
    v = rec[:, 3 * hk:3 * hk + 256]

    tt, c = g.shape
    chunk = min(CHUNK, tt)
    half = chunk // 2
    b, btot = _block_cumsum(g, lchunk_ref, chunk)
    b2 = b * LOG2E
    bt2 = btot * LOG2E
    whole_ok = jnp.min(b2) >= -2.0 * MAX_CENTRED_EXPONENT

    @pl.when(whole_ok)
    def _():
        _recur_factorised(q, k, v, b2, bt2, chunk, bd_ref, o_ref, s_ref, hk)

    @pl.when(jnp.logical_not(whole_ok))
    def _():
        bh, bth = [], []
        for n in range(tt // chunk):
            mid = jnp.broadcast_to(b2[chunk * n + half - 1:chunk * n + half], (half, c))
            end = jnp.broadcast_to(b2[chunk * (n + 1) - 1:chunk * (n + 1)], (half, c))
            bh += [b2[chunk * n:chunk * n + half], b2[chunk * n + half:chunk * (n + 1)] - mid]
            bth += [mid, end - mid]
        bh = jnp.concatenate(bh, axis=0)
        bth = jnp.concatenate(bth, axis=0)
        half_ok = jnp.min(bh) >= -2.0 * MAX_CENTRED_EXPONENT

        @pl.when(half_ok)
        def _():
            _recur_factorised(q, k, v, bh, bth, half, bd_ref, o_ref, s_ref, hk)

        @pl.when(jnp.logical_not(half_ok))
        def _():
            _recur_pairwise(q, k, g, v, lsub_ref, e2_ref, sel_ref, bd_ref, o_ref, s_ref)

    @pl.when(t == pl.num_programs(1) - 1)
    def _():
        s_o[0] = s_ref[...]


def _recur_call(rec, s0, hk, tt, name):
    b, t, _ = rec.shape
    w = 3 * hk + 256
    heads = 4
    dk = hk // heads
    e2 = np.kron(np.eye(heads), np.ones((dk, 64))).astype(np.float32)
    sel = np.zeros((SUB, SUB * SUB), np.float32)
    for i in range(SUB):
        sel[i, SUB * i:SUB * i + i + 1] = 1.0
    return pl.pallas_call(
        functools.partial(_recur_body, hk=hk),
        grid=(b, t // tt),
        in_specs=[pl.BlockSpec((1, tt, w), lambda bi, ti: (bi, ti, 0)),
                  pl.BlockSpec((1, hk, 256), lambda bi, ti: (bi, 0, 0)),
                  _const_spec((tt, tt)), _const_spec((tt, tt)),
                  _const_spec((hk, 256)), _const_spec((SUB, SUB * SUB)), _const_spec((hk, 256))],
        out_specs=[pl.BlockSpec((1, tt, 256), lambda bi, ti: (bi, ti, 0)),
                   pl.BlockSpec((1, hk, 256), lambda bi, ti: (bi, 0, 0))],
        out_shape=[jax.ShapeDtypeStruct((b, t, 256), F32), jax.ShapeDtypeStruct((b, hk, 256), F32)],
        scratch_shapes=[pltpu.VMEM((hk, 256), F32)],
        compiler_params=_params(("arbitrary", "arbitrary")),
        name=name,
    )(rec, s0, jnp.asarray(_cumsum_matrix(tt, min(CHUNK, tt)), BF16), jnp.asarray(_cumsum_matrix(tt, SUB), BF16),
      jnp.asarray(e2, BF16), jnp.asarray(sel, BF16), jnp.asarray(e2, F32))


def _outffn_body(x_ref, om_ref, oh_ref, og_ref, gate_ref, g1_ref, sh2_ref, sc2_ref, g2_ref,
                 mon_ref, hon_ref, gon_ref, nf_ref, m64_ref, wo_ref, wfi_ref, wfo_ref, y_ref, *, ff_chunk):
    bb, tt, d = x_ref.shape
    tm = bb * tt
    om = om_ref[...].reshape(tm, MLA_WIDTH)
    omn = _rms(om) * mon_ref[...]

    def headnorm(o, gain_ref, gate):
        ms = jnp.dot((o * o).astype(BF16), m64_ref[...], preferred_element_type=F32)
        return o * lax.rsqrt(ms + EPS) * gain_ref[...] * gate

    gate = gate_ref[...].reshape(tm, 512)
    ohn = headnorm(oh_ref[...].reshape(tm, 256), hon_ref, gate[:, 0:256])
    ogn = headnorm(og_ref[...].reshape(tm, 256), gon_ref, gate[:, 256:512])
    mix_in = jnp.concatenate([omn, ohn, ogn], axis=-1).astype(BF16)
    mix = jnp.dot(mix_in, wo_ref[...], preferred_element_type=F32)
    x1 = x_ref[...] + g1_ref[...] * mix.reshape(bb, tt, d)
    h2 = (_rms(x1) * nf_ref[...] * (1.0 + sc2_ref[...]) + sh2_ref[...]).reshape(tm, d).astype(BF16)
    acc = jnp.zeros((tm, d), F32)
    for c in range(D_FF // ff_chunk):
        a = jnp.dot(h2, wfi_ref[:, ff_chunk * c:ff_chunk * (c + 1)], preferred_element_type=F32)
        u = jnp.dot(h2, wfi_ref[:, D_FF + ff_chunk * c:D_FF + ff_chunk * (c + 1)], preferred_element_type=F32)
        act = (_silu(a) * u).astype(BF16)
        acc = acc + jnp.dot(act, wfo_ref[ff_chunk * c:ff_chunk * (c + 1), :], preferred_element_type=F32)
    y_ref[...] = x1 + g2_ref[...] * acc.reshape(bb, tt, d)


def _outffn_call(x, om, oh, og, gate, mods, lw, bb, tt):
    b, t, d = x.shape
    tok = lambda w: pl.BlockSpec((bb, tt, w), lambda i, j: (i, j, 0))
    modspec = pl.BlockSpec((bb, 1, d), lambda i, j: (i, 0, 0))
    consts = [lw["mon"], lw["hon"], lw["gon"], lw["nf"], lw["m64"], lw["wo"], lw["wfi"], lw["wfo"]]
    single = lambda a: pl.BlockSpec(a.shape, lambda *_: (0,) * a.ndim, pipeline_mode=pl.Buffered(1))
    return pl.pallas_call(
        functools.partial(_outffn_body, ff_chunk=256),
        grid=(b // bb, t // tt),
        in_specs=[tok(d), tok(MLA_WIDTH), tok(256), tok(256), tok(512)] + [modspec] * 4 + [single(a) for a in consts],
        out_specs=tok(d),
        out_shape=jax.ShapeDtypeStruct((b, t, d), F32),
        compiler_params=_params(("arbitrary", "arbitrary")),
        name="outffn",
    )(x, om, oh, og, gate, *mods, *consts)


def _layer_weights(l, w_in, norm_mix, norm_ffn, mla_cq_norm, mla_ckv_norm, mla_w_uq, mla_w_uk, mla_w_uv, mla_q_norm,
                   mla_k_norm, mla_out_norm, hg_lb_logits, hg_out_norm, gla_w_a2, gla_b_a, gla_out_norm, w_o,
                   w_ffn_in, w_ffn_out):
    wi = w_in[l]
    d = wi.shape[0]
    o = dict(cq=0, ckv=256, kr=512, hq=544, hf=800, hi=1056, hg=1312, gq=1568, gk=1696, gv=1824, gg=2080, ga=2336)
    cols = [wi[:, o["cq"]:o["cq"] + 256], wi[:, o["ckv"]:o["ckv"] + 256], wi[:, o["hq"]:o["hq"] + 256],
            wi[:, o["hf"]:o["hf"] + 256], wi[:, o["hi"]:o["hi"] + 256], wi[:, o["hg"]:o["hg"] + 256],
            wi[:, o["gq"]:o["gq"] + 128], wi[:, o["gk"]:o["gk"] + 128], wi[:, o["gv"]:o["gv"] + 256],
            wi[:, o["gg"]:o["gg"] + 256], wi[:, o["kr"]:o["kr"] + 32], wi[:, o["ga"]:o["ga"] + 16],
            jnp.zeros((d, LANES - MLA_ROPE - GLA_RANK), wi.dtype)]
    win = jnp.concatenate(cols, axis=1).astype(BF16)
    assert win.shape[1] == ZCOLS

    uq = mla_w_uq[l]
    r = uq.shape[0]
    half = MLA_ROPE // 2
    zpad = jnp.zeros((r, MLA_HEADS, HEAD_SLAB - MLA_NOPE - MLA_ROPE), uq.dtype)
    uq_a = jnp.concatenate([uq, zpad], axis=-1).reshape(r, -1)
    uq_b = jnp.concatenate([jnp.zeros((r, MLA_HEADS, MLA_NOPE), uq.dtype), uq[..., MLA_NOPE + half:],
                            uq[..., MLA_NOPE:MLA_NOPE + half], zpad], axis=-1).reshape(r, -1)
    wuq = jnp.concatenate([uq_a, uq_b], axis=1).astype(BF16)
    uk = mla_w_uk[l]
    wuk = jnp.concatenate([uk, jnp.zeros((r, MLA_HEADS, HEAD_SLAB - MLA_NOPE), uk.dtype)], axis=-1)
    wuk = wuk.reshape(r, -1).astype(BF16)
    wuv = mla_w_uv[l].reshape(r, -1).astype(BF16)
    wuvt = wuv.T

    qn, kn = mla_q_norm[l], mla_k_norm[l]
    z32 = jnp.zeros((HEAD_SLAB - MLA_NOPE - MLA_ROPE,), F32)
    qga = jnp.concatenate([qn, z32])[None]
    qgb = jnp.concatenate([jnp.zeros((MLA_NOPE,), F32), qn[MLA_NOPE + half:], qn[MLA_NOPE:MLA_NOPE + half], z32])[None]
    kgk = jnp.concatenate([kn[:MLA_NOPE], jnp.zeros((HEAD_SLAB - MLA_NOPE,), F32)])[None]
    gkr = kn[MLA_NOPE:][None]

    blk = np.zeros((HEAD_SLAB, HEAD_SLAB), np.float32)
    blk[:MLA_NOPE, :MLA_NOPE] = 1.0 / MLA_NOPE
    mk = np.kron(np.eye(2), blk)
    blk_q = blk.copy()
    blk_q[MLA_NOPE:MLA_NOPE + MLA_ROPE, MLA_NOPE:MLA_NOPE + MLA_ROPE] = 1.0 / MLA_ROPE
    mq = np.kron(np.eye(2), blk_q)
    m64 = np.kron(np.eye(4), np.full((64, 64), 1.0 / 64, np.float32))
    tile = np.zeros((MLA_ROPE, 2 * HEAD_SLAB), np.float32)
    for hd in range(2):
        tile[np.arange(MLA_ROPE), HEAD_SLAB * hd + MLA_NOPE + np.arange(MLA_ROPE)] = 1.0

    def norm_bound(g):
        return jnp.sqrt(MLA_NOPE * jnp.max(jnp.abs(g[:MLA_NOPE])) ** 2 + MLA_ROPE * jnp.max(jnp.abs(g[MLA_NOPE:])) ** 2)
    logit_bound = (MLA_SCALE * norm_bound(qn) * norm_bound(kn)).reshape(1).astype(F32)

    return dict(
        logit_bound=logit_bound, nm=norm_mix[l][None], win=win, cqn=mla_cq_norm[l][None], ckvn=mla_ckv_norm[l][None], wuq=wuq, wuk=wuk,
        wuv=wuv, wuvt=wuvt, qga=qga, qgb=qgb, mq=jnp.asarray(mq, BF16), mk=jnp.asarray(mk, BF16), kgk=kgk, gkr=gkr,
        tile=jnp.asarray(tile, BF16), lbl=hg_lb_logits, wa2=gla_w_a2[l].astype(BF16), ba=gla_b_a[l][None],
        mon=mla_out_norm[l][None], hon=jnp.tile(hg_out_norm[l], HG_HEADS)[None],
        gon=jnp.tile(gla_out_norm[l], GLA_HEADS)[None], nf=norm_ffn[l][None], m64=jnp.asarray(m64, BF16),
        wo=w_o[l].astype(BF16), wfi=w_ffn_in[l].astype(BF16), wfo=w_ffn_out[l].astype(BF16))


def _rope_tables(pos, reps):
    inv = (np.float32(ROPE_THETA) ** (-(np.arange(0, MLA_ROPE, 2, dtype=np.float32) / np.float32(MLA_ROPE)))).astype(np.float32)
    ang = pos.astype(np.float32)[:, None] * inv[None, :]
    cos, sin = np.cos(ang).astype(np.float32), np.sin(ang).astype(np.float32)
    n = pos.shape[0]
    pad = HEAD_SLAB - MLA_NOPE - MLA_ROPE
    cosq = np.concatenate([np.ones((n, MLA_NOPE), np.float32), cos, cos, np.zeros((n, pad), np.float32)], axis=1)
    sinq = np.concatenate([np.zeros((n, MLA_NOPE), np.float32), -sin, sin, np.zeros((n, pad), np.float32)], axis=1)
    cosk = np.concatenate([cos, cos], axis=1)
    sink = np.concatenate([-sin, sin], axis=1)
    tabs = dict(cosq=cosq, sinq=sinq, cosk=cosk, sink=sink)
    return {k: jnp.asarray(np.tile(v, (reps, 1))) for k, v in tabs.items()}


def _block_diag_state(state):
    b, hh, dk, dv = state.shape
    eye = jnp.eye(hh, dtype=state.dtype)
    return (state[:, :, :, None, :] * eye[None, :, None, :, None]).reshape(b, hh * dk, hh * dv)


def _head_states(s_bd, hh):
    dk, dv = s_bd.shape[1] // hh, s_bd.shape[2] // hh
    return jnp.stack([s_bd[:, hd * dk:(hd + 1) * dk, hd * dv:(hd + 1) * dv] for hd in range(hh)], axis=1)


def _group_layer(x, mods, lw, tabs, layer, bb, tt, rtt, s_hg, s_gla, attn_fn, transpose_v):
    sh1, sc1, g1, sh2, sc2, g2 = mods
    ckv, kr, q, k, v, hrec, grec, gate = _inproj_call(x, sh1, sc1, lw, tabs, bb, tt, layer, transpose_v)
    om = attn_fn(q, k, v)
    oh, shg = _recur_call(hrec, s_hg, HG_HEADS * HG_DK, rtt, "recur_hgrn")
    og, sgl = _recur_call(grec, s_gla, GLA_HEADS * GLA_DK, rtt, "recur_gla")
    y = _outffn_call(x, om, oh, og, gate, (g1, sh2, sc2, g2), lw, bb, tt)
    return y, ckv, kr, _head_states(shg, HG_HEADS), _head_states(sgl, GLA_HEADS)


def kernel(x_prompt, x_sample, c_prompt, c_sample, cache_mla_ckv, cache_mla_krope, state_hgrn, state_gla, w_mod, b_mod, norm_mix, norm_ffn, w_in, mla_cq_norm, mla_ckv_norm, mla_w_uq, mla_w_uk, mla_w_uv, mla_q_norm, mla_k_norm, mla_out_norm, hg_lb_logits, hg_out_norm, gla_w_a2, gla_b_a, gla_out_norm, w_o, w_ffn_in, w_ffn_out):
    bp, t_p, d = x_prompt.shape
    bs, t_s, _ = x_sample.shape
    depth = w_mod.shape[0]
    tm = 512
    tt_p = min(tm, t_p)
    bb_s = max(1, min(bs, tm // t_s))
    rtt_p = min(256, t_p)
    rtt_s = min(128, t_s)
    attn_tile = min(512, t_p)
    tkc = min(1024, cache_mla_ckv.shape[2])

    tabs_p = _rope_tables(np.arange(t_p), 1)
    tabs_s = _rope_tables(PAST_LEN + np.arange(t_s), bb_s)

    n_c = bp + bs
    c_rows = -(-n_c // 16) * 16
    c_pad = jnp.concatenate([c_prompt, c_sample, jnp.zeros((c_rows - n_c, d), c_prompt.dtype)], axis=0)

    cache_kr_t = jnp.swapaxes(cache_mla_krope, 2, 3)
    xp, xs = x_prompt, x_sample
    outs = [[] for _ in range(8)]
    for l in range(depth):
        lw = _layer_weights(l, w_in, norm_mix, norm_ffn, mla_cq_norm, mla_ckv_norm, mla_w_uq, mla_w_uk, mla_w_uv,
                            mla_q_norm, mla_k_norm, mla_out_norm, hg_lb_logits, hg_out_norm, gla_w_a2, gla_b_a,
                            gla_out_norm, w_o, w_ffn_in, w_ffn_out)
        mod = _mod_call(c_pad, w_mod, b_mod[:, None, :], l)
        mods_p = [mod[0:bp, i * d:(i + 1) * d][:, None, :] for i in range(6)]
        mods_s = [mod[bp:n_c, i * d:(i + 1) * d][:, None, :] for i in range(6)]

        zero_hg = jnp.zeros((bp, HG_HEADS * HG_DK, HG_HEADS * HG_DV), F32)
        zero_gla = jnp.zeros((bp, GLA_HEADS * GLA_DK, GLA_HEADS * GLA_DV), F32)
        xp, a, b_, c_, d_ = _group_layer(xp, mods_p, lw, tabs_p, l, 1, tt_p, rtt_p, zero_hg, zero_gla,
                                         lambda q, k, vt: _attn_prompt_call(lw["logit_bound"], q, k, vt, attn_tile), True)
        for lst, val in zip(outs[0:4], (a, b_, c_, d_)):
            lst.append(val)
        xs, a, b_, c_, d_ = _group_layer(xs, mods_s, lw, tabs_s, l, bb_s, t_s, rtt_s,
                                         _block_diag_state(state_hgrn[l]), _block_diag_state(state_gla[l]),
                                         lambda q, k, v: _attn_sample_call(lw["logit_bound"], q, k, v, cache_mla_ckv,
                                                                           cache_kr_t, lw, l, tkc), False)
        for lst, val in zip(outs[4:8], (a, b_, c_, d_)):
            lst.append(val)

    return (xp, xs) + tuple(jnp.stack(o) for o in outs)
```

```python
import functools

import numpy as np
import jax
import jax.numpy as jnp
from jax import lax
from jax.experimental import pallas as pl
from jax.experimental.pallas import tpu as pltpu

F32, BF16 = jnp.float32, jnp.bfloat16

D_MODEL = 1024
PAST_LEN = 4096
CHUNK = 64
EPS = 1e-6
NEG_INF = -1e30
ROPE_THETA = 10000.0
MLA_HEADS = 8
MLA_NOPE = 64
MLA_ROPE = 32
MLA_V = 64
MLA_Q_RANK = 256
MLA_KV_RANK = 256
MLA_WIDTH = MLA_HEADS * MLA_V
MLA_SCALE = (MLA_NOPE + MLA_ROPE) ** -0.5
LOG2E = 1.4426950408889634
HG_HEADS = 4
HG_DK = 64
HG_DV = 64
GLA_HEADS = 4
GLA_DK = 32
GLA_DV = 64
GLA_RANK = 16
GLA_TAU = 16.0
D_FF = 2816

LANES = 128
HEAD_SLAB = 128
SUB = 16
MAX_CENTRED_EXPONENT = 110.0
ZCOLS = 2432
VMEM_LIMIT = 56 * 1024 * 1024

_Z = dict(cq=0, ckv=256, hq=512, hf=768, hi=1024, hg=1280, gq=1536, gk=1664, gv=1792, gg=2048, misc=2304)


def _bdot(a, b):
    return jnp.dot(a.astype(BF16), b.astype(BF16), preferred_element_type=F32)


def _dot_nt(a, b):
    return lax.dot_general(a, b, (((1,), (1,)), ((), ())), preferred_element_type=F32)


def _dot_tn(a, b):
    return lax.dot_general(a, b, (((0,), (0,)), ((), ())), preferred_element_type=F32)


def _sigmoid(x):
    return 1.0 / (1.0 + jnp.exp(-x))


def _silu(x):
    return x * _sigmoid(x)


def _log_sigmoid(x):
    return jnp.minimum(x, 0.0) - jnp.log1p(jnp.exp(-jnp.abs(x)))


def _rms(x):
    return x * lax.rsqrt(jnp.mean(x * x, axis=-1, keepdims=True) + EPS)


def _params(sem):
    return pltpu.CompilerParams(dimension_semantics=sem, vmem_limit_bytes=VMEM_LIMIT)


def _const_spec(shape):
    nd = len(shape)
    return pl.BlockSpec(shape, lambda *_: (0,) * nd)


def _mod_body(c_ref, w_ref, b_ref, o_ref):
    o_ref[...] = _bdot(_silu(c_ref[...]), w_ref[0]) + b_ref[0]


def _mod_call(c_pad, w_mod, b_mod, layer):
    rows, d = c_pad.shape
    n = w_mod.shape[2]
    tn = 1024
    return pl.pallas_call(
        _mod_body,
        grid=(n // tn,),
        in_specs=[pl.BlockSpec((rows, d), lambda j: (0, 0)),
                  pl.BlockSpec((1, d, tn), lambda j: (layer, 0, j)),
                  pl.BlockSpec((1, 1, tn), lambda j: (layer, 0, j))],
        out_specs=pl.BlockSpec((rows, tn), lambda j: (0, j)),
        out_shape=jax.ShapeDtypeStruct((rows, n), F32),
        compiler_params=_params(("arbitrary",)),
        name="mod",
    )(c_pad, w_mod, b_mod)


def _mla_kv(ckvn_bf, kro_bf, w_uk_ref, w_uv_ref, mk_ref, kgk_ref, tile_ref, transpose_v=False, kro_transposed=False):
    place = _dot_tn if kro_transposed else functools.partial(jnp.dot, preferred_element_type=F32)
    krt = place(kro_bf, tile_ref[...])
    kgk2 = jnp.concatenate([kgk_ref[...], kgk_ref[...]], axis=-1)
    parts = []
    for hp in range(MLA_HEADS // 2):
        sl = slice(2 * HEAD_SLAB * hp, 2 * HEAD_SLAB * (hp + 1))
        sk = jnp.dot(ckvn_bf, w_uk_ref[:, sl], preferred_element_type=F32)
        msk = jnp.dot((sk * sk).astype(BF16), mk_ref[...], preferred_element_type=F32)
        parts.append((sk * lax.rsqrt(msk + EPS) * kgk2 + krt).astype(BF16))
    k = jnp.concatenate(parts, axis=-1)
    if transpose_v:
        v = _dot_nt(w_uv_ref[...], ckvn_bf).astype(BF16)
    else:
        v = jnp.dot(ckvn_bf, w_uv_ref[...], preferred_element_type=F32).astype(BF16)
    return k, v


def _inproj_body(x_ref, sh_ref, sc_ref, nm_ref, win_ref, cqn_ref, ckvn_ref, wuq_ref, wuk_ref, wuv_ref,
                 qga_ref, qgb_ref, cosq_ref, sinq_ref, mq_ref, mk_ref, kgk_ref, gkr_ref, cosk_ref, sink_ref,
                 tile_ref, lbl_ref, wa2_ref, ba_ref,
                 ckv_o, kr_o, q_o, k_o, v_o, hrec_o, grec_o, gate_o, *, layer, transpose_v):
    bb, tt, d = x_ref.shape
    tm = bb * tt
    x = x_ref[...]
    h = _rms(x) * nm_ref[...] * (1.0 + sc_ref[...]) + sh_ref[...]
    z = jnp.dot(h.reshape(tm, d).astype(BF16), win_ref[...], preferred_element_type=F32)

    def seg(name, w):
        return z[:, _Z[name]:_Z[name] + w]

    cqn = _rms(seg("cq", MLA_Q_RANK)) * cqn_ref[...]
    ckvn = _rms(seg("ckv", MLA_KV_RANK)) * ckvn_ref[...]
    ckv_o[...] = ckvn.reshape(bb, tt, MLA_KV_RANK)
    misc = seg("misc", LANES)

    qab = jnp.dot(cqn.astype(BF16), wuq_ref[...], preferred_element_type=F32)
    width = MLA_HEADS * HEAD_SLAB
    ca = cosq_ref[...] * qga_ref[...]
    sb = sinq_ref[...] * qgb_ref[...]
    ca2 = jnp.concatenate([ca, ca], axis=-1)
    sb2 = jnp.concatenate([sb, sb], axis=-1)
    for hp in range(MLA_HEADS // 2):
        sl = slice(2 * HEAD_SLAB * hp, 2 * HEAD_SLAB * (hp + 1))
        qa = qab[:, sl]
        qb = qab[:, width + sl.start:width + sl.stop]
        msq = jnp.dot((qa * qa).astype(BF16), mq_ref[...], preferred_element_type=F32)
        qo = lax.rsqrt(msq + EPS) * (qa * ca2 + qb * sb2) * (MLA_SCALE * LOG2E)
        q_o[:, :, sl] = qo.astype(BF16).reshape(bb, tt, 2 * HEAD_SLAB)

    kr = misc[:, 0:MLA_ROPE]
    krn = _rms(kr) * gkr_ref[...]
    half = MLA_ROPE // 2
    swapped = jnp.concatenate([krn[:, half:], krn[:, :half]], axis=-1)
    kro = krn * cosk_ref[...] + swapped * sink_ref[...]
    kr_o[...] = kro.reshape(bb, tt, MLA_ROPE)
    k, v = _mla_kv(ckvn.astype(BF16), kro.astype(BF16), wuk_ref, wuv_ref, mk_ref, kgk_ref, tile_ref, transpose_v)
    k_o[...] = k.reshape(bb, tt, width)
    if transpose_v:
        v_o[0, 0] = v
    else:
        v_o[...] = v.reshape(bb, tt, MLA_WIDTH)

    lg = lbl_ref[...]
    e = jnp.exp(lg - jnp.max(lg, axis=0, keepdims=True))
    p = e / jnp.sum(e, axis=0, keepdims=True)
    lb = jnp.sum(p[0:layer + 1], axis=0, keepdims=True) - p[0:1]
    f = lb + (1.0 - lb) * _sigmoid(seg("hf", 256))
    hrec_o[:, :, 0:256] = seg("hq", 256).reshape(bb, tt, 256)
    hrec_o[:, :, 256:512] = (1.0 - f).reshape(bb, tt, 256)
    hrec_o[:, :, 512:768] = jnp.log(f).reshape(bb, tt, 256)
    hrec_o[:, :, 768:1024] = seg("hi", 256).reshape(bb, tt, 256)

    ga = misc[:, MLA_ROPE:MLA_ROPE + GLA_RANK]
    apre = jnp.dot(ga.astype(BF16), wa2_ref[...], preferred_element_type=F32) + ba_ref[...]
    grec_o[:, :, 0:128] = (seg("gq", 128) * (GLA_DK ** -0.5)).reshape(bb, tt, 128)
    grec_o[:, :, 128:256] = seg("gk", 128).reshape(bb, tt, 128)
    grec_o[:, :, 256:384] = (_log_sigmoid(apre) / GLA_TAU).reshape(bb, tt, 128)
    grec_o[:, :, 384:640] = seg("gv", 256).reshape(bb, tt, 256)

    gate_o[:, :, 0:256] = _silu(seg("hg", 256)).reshape(bb, tt, 256)
    gate_o[:, :, 256:512] = _silu(seg("gg", 256)).reshape(bb, tt, 256)


def _inproj_call(x, sh, sc, lw, tabs, bb, tt, layer, transpose_v):
    b, t, d = x.shape
    tm = bb * tt
    grid = (b // bb, t // tt)
    tok = lambda w: pl.BlockSpec((bb, tt, w), lambda i, j: (i, j, 0))
    modspec = pl.BlockSpec((bb, 1, d), lambda i, j: (i, 0, 0))
    tab = lambda w: pl.BlockSpec((tm, w), lambda i, j: (j, 0))
    assert bb == 1 or not transpose_v
    consts = [lw["nm"], lw["win"], lw["cqn"], lw["ckvn"], lw["wuq"], lw["wuk"], lw["wuvt" if transpose_v else "wuv"],
              lw["qga"], lw["qgb"]]
    consts2 = [lw["mq"], lw["mk"], lw["kgk"], lw["gkr"]]
    consts3 = [lw["tile"], lw["lbl"], lw["wa2"], lw["ba"]]
    in_specs = ([tok(d), modspec, modspec] + [_const_spec(a.shape) for a in consts]
                + [tab(LANES), tab(LANES)] + [_const_spec(a.shape) for a in consts2]
                + [tab(MLA_ROPE), tab(MLA_ROPE)] + [_const_spec(a.shape) for a in consts3])
    widths = [(MLA_KV_RANK, F32), (MLA_ROPE, F32), (MLA_HEADS * HEAD_SLAB, BF16), (MLA_HEADS * HEAD_SLAB, BF16),
              (MLA_WIDTH, BF16), (1024, F32), (640, F32), (512, F32)]
    out_specs = [tok(w) for w, _ in widths]
    out_shape = [jax.ShapeDtypeStruct((b, t, w), dt) for w, dt in widths]
    if transpose_v:
        out_specs[4] = pl.BlockSpec((1, 1, MLA_WIDTH, tt), lambda i, j: (i, j, 0, 0))
        out_shape[4] = jax.ShapeDtypeStruct((b, t // tt, MLA_WIDTH, tt), BF16)
    return pl.pallas_call(
        functools.partial(_inproj_body, layer=layer, transpose_v=transpose_v),
        grid=grid,
        in_specs=in_specs,
        out_specs=out_specs,
        out_shape=out_shape,
        compiler_params=_params(("arbitrary", "arbitrary")),
        name="inproj",
    )(x, sh, sc, *consts, tabs["cosq"], tabs["sinq"], *consts2, tabs["cosk"], tabs["sink"], *consts3)


MAX_UNSHIFTED_LOGIT = 60.0
SMALL_SCORE_TILE = 64 * 1024


def _attend_bounded(q, k, v, lf_ref, accf_ref, bias):
    tq, tk = q.shape[0], k.shape[0]

    def scores(hd):
        sl = slice(HEAD_SLAB * hd, HEAD_SLAB * (hd + 1))
        s = _dot_nt(q[:, sl], k[:, sl])
        return s if bias is None else s + bias

    def accumulate(hd, s):
        p = jnp.exp2(s)
        if tk % LANES == 0:
            part = p[:, 0:LANES]
            for c in range(1, tk // LANES):
                part = part + p[:, LANES * c:LANES * (c + 1)]
        else:
            lane0 = lax.broadcasted_iota(jnp.int32, (tq, LANES), 1) == 0
            part = jnp.where(lane0, jnp.sum(p, axis=-1, keepdims=True), 0.0)
        lf_ref[hd] += part
        hp = hd // 2
        accf_ref[hd] += jnp.dot(p.astype(BF16), v[:, LANES * hp:LANES * (hp + 1)], preferred_element_type=F32)

    if tq * tk <= SMALL_SCORE_TILE:
        ss = [scores(hd) for hd in range(MLA_HEADS)]
        for hd in range(MLA_HEADS):
            accumulate(hd, ss[hd])
    else:
        for hd in range(MLA_HEADS):
            accumulate(hd, scores(hd))


def _finish_bounded(lf_ref, accf_ref):
    tq = accf_ref.shape[1]
    low = lax.broadcasted_iota(jnp.int32, (tq, LANES), 1) < MLA_V
    outs = []
    for hp in range(MLA_HEADS // 2):
        la = jnp.sum(lf_ref[2 * hp], axis=-1, keepdims=True)
        lb = jnp.sum(lf_ref[2 * hp + 1], axis=-1, keepdims=True)
        outs.append(jnp.where(low, accf_ref[2 * hp] / la, accf_ref[2 * hp + 1] / lb))
    return jnp.concatenate(outs, axis=-1)


def _attend_online(q, k, v, m_ref, l_ref, acc_ref, bias):
    tq = q.shape[0]
    low = lax.broadcasted_iota(jnp.int32, (tq, LANES), 1) < MLA_V
    for hp in range(MLA_HEADS // 2):
        vp = v[:, LANES * hp:LANES * (hp + 1)]
        pvs, alphas = [], []
        for e in range(2):
            hd = 2 * hp + e
            sl = slice(HEAD_SLAB * hd, HEAD_SLAB * (hd + 1))
            s = _dot_nt(q[:, sl], k[:, sl])
            if bias is not None:
                s = s + bias
            m_prev = m_ref[hd]
            m_new = jnp.maximum(m_prev, jnp.max(s, axis=-1, keepdims=True))
            p = jnp.exp2(s - m_new)
            alpha = jnp.exp2(m_prev - m_new)
            l_ref[hd] = alpha * l_ref[hd] + jnp.sum(p, axis=-1, keepdims=True)
            m_ref[hd] = m_new
            pvs.append(jnp.dot(p.astype(BF16), vp, preferred_element_type=F32))
            alphas.append(alpha)
        sl2 = slice(LANES * hp, LANES * (hp + 1))
        acc_ref[:, sl2] = jnp.where(low, alphas[0], alphas[1]) * acc_ref[:, sl2] + jnp.where(low, pvs[0], pvs[1])


def _finish_online(l_ref, acc_ref):
    tq = acc_ref.shape[0]
    low = lax.broadcasted_iota(jnp.int32, (tq, LANES), 1) < MLA_V
    outs = []
    for hp in range(MLA_HEADS // 2):
        l2 = jnp.where(low, l_ref[2 * hp], l_ref[2 * hp + 1])
        outs.append(acc_ref[:, LANES * hp:LANES * (hp + 1)] / l2)
    return jnp.concatenate(outs, axis=-1)


def _softmax_scratch(tq):
    return [pltpu.VMEM((MLA_HEADS, tq, 1), F32), pltpu.VMEM((MLA_HEADS, tq, 1), F32), pltpu.VMEM((tq, MLA_WIDTH), F32),
            pltpu.VMEM((MLA_HEADS, tq, LANES), F32), pltpu.VMEM((MLA_HEADS, tq, LANES), F32)]


def _softmax_init(m_ref, l_ref, acc_ref, lf_ref, accf_ref):
    m_ref[...] = jnp.full(m_ref.shape, NEG_INF, F32)
    for ref in (l_ref, acc_ref, lf_ref, accf_ref):
        ref[...] = jnp.zeros(ref.shape, F32)


def _attend(bounded, cond, get_qkv, scratch, bias_ref):
    m_ref, l_ref, acc_ref, lf_ref, accf_ref = scratch

    @pl.when(jnp.logical_and(cond, bounded))
    def _():
        _attend_bounded(*get_qkv(), lf_ref, accf_ref, None if bias_ref is None else bias_ref[...])

    @pl.when(jnp.logical_and(cond, jnp.logical_not(bounded)))
    def _():
        _attend_online(*get_qkv(), m_ref, l_ref, acc_ref, None if bias_ref is None else bias_ref[...])


def _softmax_finish(bounded, cond, o_ref, scratch):
    m_ref, l_ref, acc_ref, lf_ref, accf_ref = scratch

    @pl.when(jnp.logical_and(cond, bounded))
    def _():
        o_ref[0] = _finish_bounded(lf_ref, accf_ref)

    @pl.when(jnp.logical_and(cond, jnp.logical_not(bounded)))
    def _():
        o_ref[0] = _finish_online(l_ref, acc_ref)


def _attend_t_bounded(q, k, vt, lf_ref, acc_ref, bias_t):
    tq, tk = q.shape[0], k.shape[0]

    def scores_t(hd):
        sl = slice(HEAD_SLAB * hd, HEAD_SLAB * (hd + 1))
        st = _dot_nt(k[:, sl], q[:, sl])
        return st if bias_t is None else st + bias_t

    st_next = scores_t(0)
    for hd in range(MLA_HEADS):
        st = st_next
        if hd + 1 < MLA_HEADS:
            st_next = scores_t(hd + 1)
        pt = jnp.exp2(st)
        lf_ref[hd] += jnp.sum(pt.reshape(tk // 8, 8, tq), axis=0)
        rows = slice(MLA_V * hd, MLA_V * (hd + 1))
        acc_ref[rows, :] += jnp.dot(vt[rows, :], pt.astype(BF16), preferred_element_type=F32)


def _attend_t_online(q, k, vt, m_ref, l_ref, acc_ref, bias_t):
    for hd in range(MLA_HEADS):
        sl = slice(HEAD_SLAB * hd, HEAD_SLAB * (hd + 1))
        st = _dot_nt(k[:, sl], q[:, sl])
        if bias_t is not None:
            st = st + bias_t
        m_prev = m_ref[hd]
        m_new = jnp.maximum(m_prev, jnp.max(st, axis=0, keepdims=True))
        pt = jnp.exp2(st - m_new)
        alpha = jnp.exp2(m_prev - m_new)
        l_ref[hd] = alpha * l_ref[hd] + jnp.sum(pt, axis=0, keepdims=True)
        m_ref[hd] = m_new
        rows = slice(MLA_V * hd, MLA_V * (hd + 1))
        acc_ref[rows, :] = alpha * acc_ref[rows, :] + jnp.dot(vt[rows, :], pt.astype(BF16), preferred_element_type=F32)


def _finish_t(l_of_head, acc_ref):
    outs = []
    for hd in range(MLA_HEADS):
        outs.append(acc_ref[MLA_V * hd:MLA_V * (hd + 1), :] / l_of_head(hd))
    return jnp.concatenate(outs, axis=0).T


def _attn_prompt_body(qi_ref, ki_ref, bound_ref, bias_ref, q_ref, k_ref, vt_ref, o_ref, m_ref, l_ref, lf_ref, acc_ref):
    step = pl.program_id(1)
    i = qi_ref[step]
    j = ki_ref[step]
    bounded = bound_ref[0] <= MAX_UNSHIFTED_LOGIT
    unbounded = jnp.logical_not(bounded)
    diag = j == i

    @pl.when(j == 0)
    def _():
        m_ref[...] = jnp.full(m_ref.shape, NEG_INF, F32)
        for ref in (l_ref, lf_ref, acc_ref):
            ref[...] = jnp.zeros(ref.shape, F32)

    for on_diag in (False, True):
        cond = diag if on_diag else j < i

        @pl.when(jnp.logical_and(cond, bounded))
        def _():
            _attend_t_bounded(q_ref[0], k_ref[0], vt_ref[0, 0], lf_ref, acc_ref, bias_ref[...] if on_diag else None)

        @pl.when(jnp.logical_and(cond, unbounded))
        def _():
            _attend_t_online(q_ref[0], k_ref[0], vt_ref[0, 0], m_ref, l_ref, acc_ref, bias_ref[...] if on_diag else None)

    @pl.when(jnp.logical_and(diag, bounded))
    def _():
        o_ref[0] = _finish_t(lambda hd: jnp.sum(lf_ref[hd], axis=0, keepdims=True), acc_ref)

    @pl.when(jnp.logical_and(diag, unbounded))
    def _():
        o_ref[0] = _finish_t(lambda hd: l_ref[hd], acc_ref)


def _attn_prompt_call(bound, q, k, vt, tile):
    b, t, _ = q.shape
    n = t // tile
    assert vt.shape == (b, n, MLA_WIDTH, tile)
    pairs = [(i, j) for i in range(n) for j in range(i + 1)]
    qi = jnp.asarray([p[0] for p in pairs], jnp.int32)
    ki = jnp.asarray([p[1] for p in pairs], jnp.int32)
    q_map = lambda bi, st, qi_ref, ki_ref: (bi, qi_ref[st], 0)
    k_map = lambda bi, st, qi_ref, ki_ref: (bi, ki_ref[st], 0)
    vt_map = lambda bi, st, qi_ref, ki_ref: (bi, ki_ref[st], 0, 0)
    chunk_of = np.arange(tile) // CHUNK
    bias_t = jnp.asarray(np.where(chunk_of[:, None] <= chunk_of[None, :], 0.0, NEG_INF), F32)
    grid_spec = pltpu.PrefetchScalarGridSpec(
        num_scalar_prefetch=2,
        grid=(b, len(pairs)),
        in_specs=[pl.BlockSpec(memory_space=pltpu.SMEM), pl.BlockSpec((tile, tile), lambda *_: (0, 0)),
                  pl.BlockSpec((1, tile, q.shape[2]), q_map),
                  pl.BlockSpec((1, tile, k.shape[2]), k_map),
                  pl.BlockSpec((1, 1, MLA_WIDTH, tile), vt_map)],
        out_specs=pl.BlockSpec((1, tile, MLA_WIDTH), q_map),
        scratch_shapes=[pltpu.VMEM((MLA_HEADS, 1, tile), F32), pltpu.VMEM((MLA_HEADS, 1, tile), F32),
                        pltpu.VMEM((MLA_HEADS, 8, tile), F32), pltpu.VMEM((MLA_WIDTH, tile), F32)])
    return pl.pallas_call(
        _attn_prompt_body,
        grid_spec=grid_spec,
        out_shape=jax.ShapeDtypeStruct((b, t, MLA_WIDTH), F32),
        compiler_params=_params(("arbitrary", "arbitrary")),
        name="attn_prompt",
    )(qi, ki, bound, bias_t, q, k, vt)


def _attn_sample_body(bound_ref, q_ref, kn_ref, vn_ref, ckv_ref, kr_ref, wuk_ref, wuv_ref, mk_ref, kgk_ref, tile_ref,
                      o_ref, *scratch):
    j = pl.program_id(1)
    bounded = bound_ref[0] <= MAX_UNSHIFTED_LOGIT
    always = j >= 0

    @pl.when(j == 0)
    def _():
        _softmax_init(*scratch)

    _attend(bounded, j == 0, lambda: (q_ref[0], kn_ref[0], vn_ref[0]), scratch, None)

    def past_qkv():
        k, v = _mla_kv(ckv_ref[0, 0].astype(BF16), kr_ref[0, 0].astype(BF16), wuk_ref, wuv_ref, mk_ref, kgk_ref,
                       tile_ref, kro_transposed=True)
        return q_ref[0], k, v

    _attend(bounded, always, past_qkv, scratch, None)
    _softmax_finish(bounded, j == pl.num_programs(1) - 1, o_ref, scratch)


def _attn_sample_call(bound, q, k_new, v_new, cache_ckv, cache_kr_t, lw, layer, tkc):
    b, t, _ = q.shape
    past = cache_ckv.shape[2]
    consts = [lw["wuk"], lw["wuv"], lw["mk"], lw["kgk"], lw["tile"]]
    new = lambda w: pl.BlockSpec((1, t, w), lambda bi, j: (bi, 0, 0))
    return pl.pallas_call(
        _attn_sample_body,
        grid=(b, past // tkc),
        in_specs=[pl.BlockSpec(memory_space=pltpu.SMEM), new(q.shape[2]), new(k_new.shape[2]), new(v_new.shape[2]),
                  pl.BlockSpec((1, 1, tkc, MLA_KV_RANK), lambda bi, j: (layer, bi, j, 0)),
                  pl.BlockSpec((1, 1, MLA_ROPE, tkc), lambda bi, j: (layer, bi, 0, j))]
                 + [_const_spec(a.shape) for a in consts],
        out_specs=new(MLA_WIDTH),
        out_shape=jax.ShapeDtypeStruct((b, t, MLA_WIDTH), F32),
        scratch_shapes=_softmax_scratch(t),
        compiler_params=_params(("arbitrary", "arbitrary")),
        name="attn_sample",
    )(bound, q, k_new, v_new, cache_ckv, cache_kr_t, *consts)


def _cumsum_matrix(tt, blk):
    r = np.arange(tt)
    return (((r[:, None] // blk) == (r[None, :] // blk)) & (r[None, :] <= r[:, None])).astype(np.float32)


def _block_cumsum(g, lmat_ref, blk):
    tt, c = g.shape
    lmat = lmat_ref[...]
    g1 = g.astype(BF16)
    r1 = g - g1.astype(F32)
    g2 = r1.astype(BF16)
    g3 = (r1 - g2.astype(F32)).astype(BF16)
    bb = jnp.dot(lmat, jnp.concatenate([g1, g2, g3], axis=-1), preferred_element_type=F32)
    b = (bb[:, 2 * c:3 * c] + bb[:, c:2 * c]) + bb[:, 0:c]
    last = [jnp.broadcast_to(b[blk * (n + 1) - 1:blk * (n + 1)], (blk, c)) for n in range(tt // blk)]
    return b, jnp.concatenate(last, axis=0)


def _state_chain(blk, qe, ke, vb, dec, o_in_block, bd_ref, o_ref, s_ref):
    tt = qe.shape[0]
    nb = tt // blk
    picked = [dec[blk * n:blk * n + 1] for n in range(nb)]
    if nb % 8:
        picked.append(jnp.zeros((8 - nb % 8, dec.shape[1]), F32))
    dsel = jnp.concatenate(picked, axis=0).T
    us = [_dot_tn(ke[blk * n:blk * (n + 1)], vb[blk * n:blk * (n + 1)]) * bd_ref[...] for n in range(nb)]
    s = s_ref[...]
    for n in range(nb):
        rows = slice(blk * n, blk * (n + 1))
        o_ref[0, rows, :] = jnp.dot(qe[rows], s.astype(BF16), preferred_element_type=F32) + o_in_block(n)
        s = dsel[:, n:n + 1] * s + us[n]
    s_ref[...] = s


def _recur_factorised(q, k, v, b2, bt2, chunk, bd_ref, o_ref, s_ref, hk):
    heads = 4
    dk = hk // heads
    tt = q.shape[0]
    ref = 0.5 * bt2
    qe = (q * jnp.exp2(b2)).astype(BF16)
    qh = (q * jnp.exp2(b2 - ref)).astype(BF16)
    kh = (k * jnp.exp2(ref - b2)).astype(BF16)
    ke = (k * jnp.exp2(bt2 - b2)).astype(BF16)
    dec = jnp.exp2(bt2)
    vb = v.astype(BF16)
    rows_k = lax.broadcasted_iota(jnp.int32, (heads * chunk, hk), 0) // chunk
    same_k = rows_k == lax.broadcasted_iota(jnp.int32, (heads * chunk, hk), 1) // dk
    rows_v = lax.broadcasted_iota(jnp.int32, (heads * chunk, 256), 0) // chunk
    same_v = rows_v == lax.broadcasted_iota(jnp.int32, (heads * chunk, 256), 1) // 64
    ii = lax.broadcasted_iota(jnp.int32, (chunk, heads * chunk), 0)
    jj = lax.broadcasted_iota(jnp.int32, (chunk, heads * chunk), 1) % chunk
    causal = jj <= ii
    chunks = [slice(chunk * n, chunk * (n + 1)) for n in range(tt // chunk)]
    scs = []
    for rows in chunks:
        kbd = jnp.where(same_k, jnp.concatenate([kh[rows]] * heads, axis=0), jnp.zeros(same_k.shape, BF16))
        scs.append(_dot_nt(qh[rows], kbd))
    ods = []
    for rows, sc in zip(chunks, scs):
        vbd = jnp.where(same_v, jnp.concatenate([vb[rows]] * heads, axis=0), jnp.zeros(same_v.shape, BF16))
        ods.append(jnp.dot(jnp.where(causal, sc, 0.0).astype(BF16), vbd, preferred_element_type=F32))
    _state_chain(chunk, qe, ke, vb, dec, lambda n: ods[n], bd_ref, o_ref, s_ref)


def _recur_pairwise(q, k, g, v, lsub_ref, e2_ref, sel_ref, bd_ref, o_ref, s_ref):
    tt = q.shape[0]
    nb = tt // SUB
    b, btot = _block_cumsum(g, lsub_ref, SUB)
    b2 = b * LOG2E
    bt2 = btot * LOG2E
    qe = (q * jnp.exp2(b2)).astype(BF16)
    ke = (k * jnp.exp2(bt2 - b2)).astype(BF16)
    dec = jnp.exp2(bt2)
    vb = v.astype(BF16)
    xs = []
    for n in range(nb):
        r0 = SUB * n
        bj, kj = b2[r0:r0 + SUB], k[r0:r0 + SUB]
        for i0 in range(SUB):
            bi = b2[r0 + i0:r0 + i0 + 1]
            qi = q[r0 + i0:r0 + i0 + 1]
            xs.append((jnp.exp2(jnp.minimum(bi - bj, 0.0)) * (kj * qi)).astype(BF16))
    x = jnp.concatenate(xs, axis=0)
    r = jnp.dot(x, e2_ref[...], preferred_element_type=F32)
    y = (r.reshape(nb, SUB, SUB, 256) * v.reshape(nb, 1, SUB, 256)).reshape(nb * SUB * SUB, 256).astype(BF16)
    ods = [jnp.dot(sel_ref[...], y[SUB * SUB * n:SUB * SUB * (n + 1)], preferred_element_type=F32) for n in range(nb)]
    _state_chain(SUB, qe, ke, vb, dec, lambda n: ods[n], bd_ref, o_ref, s_ref)


def _recur_body(rec_ref, s0_ref, lchunk_ref, lsub_ref, e2_ref, sel_ref, bd_ref, o_ref, s_o, s_ref, *, hk):
    t = pl.program_id(1)

    @pl.when(t == 0)
    def _():
        s_ref[...] = s0_ref[0]

    rec = rec_ref[0]
    q = rec[:, 0:hk]
    k = rec[:, hk:2 * hk]
    g = rec[:, 2 * hk:3 * hk]
    v = rec[:, 3 * hk:3 * hk + 256]

    tt, c = g.shape
    chunk = min(CHUNK, tt)
    half = chunk // 2
    b, btot = _block_cumsum(g, lchunk_ref, chunk)
    b2 = b * LOG2E
    bt2 = btot * LOG2E
    whole_ok = jnp.min(b2) >= -2.0 * MAX_CENTRED_EXPONENT

    @pl.when(whole_ok)
    def _():
        _recur_factorised(q, k, v, b2, bt2, chunk, bd_ref, o_ref, s_ref, hk)

    @pl.when(jnp.logical_not(whole_ok))
    def _():
        bh, bth = [], []
        for n in range(tt // chunk):
            mid = jnp.broadcast_to(b2[chunk * n + half - 1:chunk * n + half], (half, c))
            end = jnp.broadcast_to(b2[chunk * (n + 1) - 1:chunk * (n + 1)], (half, c))
            bh += [b2[chunk * n:chunk * n + half], b2[chunk * n + half:chunk * (n + 1)] - mid]
            bth += [mid, end - mid]
        bh = jnp.concatenate(bh, axis=0)
        bth = jnp.concatenate(bth, axis=0)
        half_ok = jnp.min(bh) >= -2.0 * MAX_CENTRED_EXPONENT

        @pl.when(half_ok)
        def _():
            _recur_factorised(q, k, v, bh, bth, half, bd_ref, o_ref, s_ref, hk)

        @pl.when(jnp.logical_not(half_ok))
        def _():
            _recur_pairwise(q, k, g, v, lsub_ref, e2_ref, sel_ref, bd_ref, o_ref, s_ref)

    @pl.when(t == pl.num_programs(1) - 1)
    def _():
        s_o[0] = s_ref[...]


def _recur_call(rec, s0, hk, tt, name):
    b, t, _ = rec.shape
    w = 3 * hk + 256
    heads = 4
    dk = hk // heads
    e2 = np.kron(np.eye(heads), np.ones((dk, 64))).astype(np.float32)
    sel = np.zeros((SUB, SUB * SUB), np.float32)
    for i in range(SUB):
        sel[i, SUB * i:SUB * i + i + 1] = 1.0
    return pl.pallas_call(
        functools.partial(_recur_body, hk=hk),
        grid=(b, t // tt),
        in_specs=[pl.BlockSpec((1, tt, w), lambda bi, ti: (bi, ti, 0)),
                  pl.BlockSpec((1, hk, 256), lambda bi, ti: (bi, 0, 0)),
                  _const_spec((tt, tt)), _const_spec((tt, tt)),
                  _const_spec((hk, 256)), _const_spec((SUB, SUB * SUB)), _const_spec((hk, 256))],
        out_specs=[pl.BlockSpec((1, tt, 256), lambda bi, ti: (bi, ti, 0)),
                   pl.BlockSpec((1, hk, 256), lambda bi, ti: (bi, 0, 0))],
        out_shape=[jax.ShapeDtypeStruct((b, t, 256), F32), jax.ShapeDtypeStruct((b, hk, 256), F32)],
        scratch_shapes=[pltpu.VMEM((hk, 256), F32)],
        compiler_params=_params(("arbitrary", "arbitrary")),
        name=name,
    )(rec, s0, jnp.asarray(_cumsum_matrix(tt, min(CHUNK, tt)), BF16), jnp.asarray(_cumsum_matrix(tt, SUB), BF16),
      jnp.asarray(e2, BF16), jnp.asarray(sel, BF16), jnp.asarray(e2, F32))


def _outffn_body(x_ref, om_ref, oh_ref, og_ref, gate_ref, g1_ref, sh2_ref, sc2_ref, g2_ref,
                 mon_ref, hon_ref, gon_ref, nf_ref, m64_ref, wo_ref, wfi_ref, wfo_ref, y_ref, *, ff_chunk):
    bb, tt, d = x_ref.shape
    tm = bb * tt
    om = om_ref[...].reshape(tm, MLA_WIDTH)
    omn = _rms(om) * mon_ref[...]

    def headnorm(o, gain_ref, gate):
        ms = jnp.dot((o * o).astype(BF16), m64_ref[...], preferred_element_type=F32)
        return o * lax.rsqrt(ms + EPS) * gain_ref[...] * gate

    gate = gate_ref[...].reshape(tm, 512)
    ohn = headnorm(oh_ref[...].reshape(tm, 256), hon_ref, gate[:, 0:256])
    ogn = headnorm(og_ref[...].reshape(tm, 256), gon_ref, gate[:, 256:512])
    mix_in = jnp.concatenate([omn, ohn, ogn], axis=-1).astype(BF16)
    mix = jnp.dot(mix_in, wo_ref[...], preferred_element_type=F32)
    x1 = x_ref[...] + g1_ref[...] * mix.reshape(bb, tt, d)
    h2 = (_rms(x1) * nf_ref[...] * (1.0 + sc2_ref[...]) + sh2_ref[...]).reshape(tm, d).astype(BF16)
    acc = jnp.zeros((tm, d), F32)
    for c in range(D_FF // ff_chunk):
        a = jnp.dot(h2, wfi_ref[:, ff_chunk * c:ff_chunk * (c + 1)], preferred_element_type=F32)
        u = jnp.dot(h2, wfi_ref[:, D_FF + ff_chunk * c:D_FF + ff_chunk * (c + 1)], preferred_element_type=F32)
        act = (_silu(a) * u).astype(BF16)
        acc = acc + jnp.dot(act, wfo_ref[ff_chunk * c:ff_chunk * (c + 1), :], preferred_element_type=F32)
    y_ref[...] = x1 + g2_ref[...] * acc.reshape(bb, tt, d)


def _outffn_call(x, om, oh, og, gate, mods, lw, bb, tt):
    b, t, d = x.shape
    tok = lambda w: pl.BlockSpec((bb, tt, w), lambda i, j: (i, j, 0))
    modspec = pl.BlockSpec((bb, 1, d), lambda i, j: (i, 0, 0))
    consts = [lw["mon"], lw["hon"], lw["gon"], lw["nf"], lw["m64"], lw["wo"], lw["wfi"], lw["wfo"]]
    single = lambda a: pl.BlockSpec(a.shape, lambda *_: (0,) * a.ndim, pipeline_mode=pl.Buffered(1))
    return pl.pallas_call(
        functools.partial(_outffn_body, ff_chunk=256),
        grid=(b // bb, t // tt),
        in_specs=[tok(d), tok(MLA_WIDTH), tok(256), tok(256), tok(512)] + [modspec] * 4 + [single(a) for a in consts],
        out_specs=tok(d),
        out_shape=jax.ShapeDtypeStruct((b, t, d), F32),
        compiler_params=_params(("arbitrary", "arbitrary")),
        name="outffn",
    )(x, om, oh, og, gate, *mods, *consts)


def _layer_weights(l, w_in, norm_mix, norm_ffn, mla_cq_norm, mla_ckv_norm, mla_w_uq, mla_w_uk, mla_w_uv, mla_q_norm,
                   mla_k_norm, mla_out_norm, hg_lb_logits, hg_out_norm, gla_w_a2, gla_b_a, gla_out_norm, w_o,
                   w_ffn_in, w_ffn_out):
    wi = w_in[l]
    d = wi.shape[0]
    o = dict(cq=0, ckv=256, kr=512, hq=544, hf=800, hi=1056, hg=1312, gq=1568, gk=1696, gv=1824, gg=2080, ga=2336)
    cols = [wi[:, o["cq"]:o["cq"] + 256], wi[:, o["ckv"]:o["ckv"] + 256], wi[:, o["hq"]:o["hq"] + 256],
            wi[:, o["hf"]:o["hf"] + 256], wi[:, o["hi"]:o["hi"] + 256], wi[:, o["hg"]:o["hg"] + 256],
            wi[:, o["gq"]:o["gq"] + 128], wi[:, o["gk"]:o["gk"] + 128], wi[:, o["gv"]:o["gv"] + 256],
            wi[:, o["gg"]:o["gg"] + 256], wi[:, o["kr"]:o["kr"] + 32], wi[:, o["ga"]:o["ga"] + 16],
            jnp.zeros((d, LANES - MLA_ROPE - GLA_RANK), wi.dtype)]
    win = jnp.concatenate(cols, axis=1).astype(BF16)
    assert win.shape[1] == ZCOLS

    uq = mla_w_uq[l]
    r = uq.shape[0]
    half = MLA_ROPE // 2
    zpad = jnp.zeros((r, MLA_HEADS, HEAD_SLAB - MLA_NOPE - MLA_ROPE), uq.dtype)
    uq_a = jnp.concatenate([uq, zpad], axis=-1).reshape(r, -1)
    uq_b = jnp.concatenate([jnp.zeros((r, MLA_HEADS, MLA_NOPE), uq.dtype), uq[..., MLA_NOPE + half:],
                            uq[..., MLA_NOPE:MLA_NOPE + half], zpad], axis=-1).reshape(r, -1)
    wuq = jnp.concatenate([uq_a, uq_b], axis=1).astype(BF16)
    uk = mla_w_uk[l]
    wuk = jnp.concatenate([uk, jnp.zeros((r, MLA_HEADS, HEAD_SLAB - MLA_NOPE), uk.dtype)], axis=-1)
    wuk = wuk.reshape(r, -1).astype(BF16)
    wuv = mla_w_uv[l].reshape(r, -1).astype(BF16)
    wuvt = wuv.T

    qn, kn = mla_q_norm[l], mla_k_norm[l]
    z32 = jnp.zeros((HEAD_SLAB - MLA_NOPE - MLA_ROPE,), F32)
    qga = jnp.concatenate([qn, z32])[None]
    qgb = jnp.concatenate([jnp.zeros((MLA_NOPE,), F32), qn[MLA_NOPE + half:], qn[MLA_NOPE:MLA_NOPE + half], z32])[None]
    kgk = jnp.concatenate([kn[:MLA_NOPE], jnp.zeros((HEAD_SLAB - MLA_NOPE,), F32)])[None]
    gkr = kn[MLA_NOPE:][None]

    blk = np.zeros((HEAD_SLAB, HEAD_SLAB), np.float32)
    blk[:MLA_NOPE, :MLA_NOPE] = 1.0 / MLA_NOPE
    mk = np.kron(np.eye(2), blk)
    blk_q = blk.copy()
    blk_q[MLA_NOPE:MLA_NOPE + MLA_ROPE, MLA_NOPE:MLA_NOPE + MLA_ROPE] = 1.0 / MLA_ROPE
    mq = np.kron(np.eye(2), blk_q)
    m64 = np.kron(np.eye(4), np.full((64, 64), 1.0 / 64, np.float32))
    tile = np.zeros((MLA_ROPE, 2 * HEAD_SLAB), np.float32)
    for hd in range(2):
        tile[np.arange(MLA_ROPE), HEAD_SLAB * hd + MLA_NOPE + np.arange(MLA_ROPE)] = 1.0

    def norm_bound(g):
        return jnp.sqrt(MLA_NOPE * jnp.max(jnp.abs(g[:MLA_NOPE])) ** 2 + MLA_ROPE * jnp.max(jnp.abs(g[MLA_NOPE:])) ** 2)
    logit_bound = (MLA_SCALE * norm_bound(qn) * norm_bound(kn)).reshape(1).astype(F32)

    return dict(
        logit_bound=logit_bound, nm=norm_mix[l][None], win=win, cqn=mla_cq_norm[l][None], ckvn=mla_ckv_norm[l][None], wuq=wuq, wuk=wuk,
        wuv=wuv, wuvt=wuvt, qga=qga, qgb=qgb, mq=jnp.asarray(mq, BF16), mk=jnp.asarray(mk, BF16), kgk=kgk, gkr=gkr,
        tile=jnp.asarray(tile, BF16), lbl=hg_lb_logits, wa2=gla_w_a2[l].astype(BF16), ba=gla_b_a[l][None],
        mon=mla_out_norm[l][None], hon=jnp.tile(hg_out_norm[l], HG_HEADS)[None],
        gon=jnp.tile(gla_out_norm[l], GLA_HEADS)[None], nf=norm_ffn[l][None], m64=jnp.asarray(m64, BF16),
        wo=w_o[l].astype(BF16), wfi=w_ffn_in[l].astype(BF16), wfo=w_ffn_out[l].astype(BF16))


def _rope_tables(pos, reps):
    inv = (np.float32(ROPE_THETA) ** (-(np.arange(0, MLA_ROPE, 2, dtype=np.float32) / np.float32(MLA_ROPE)))).astype(np.float32)
    ang = pos.astype(np.float32)[:, None] * inv[None, :]
    cos, sin = np.cos(ang).astype(np.float32), np.sin(ang).astype(np.float32)
    n = pos.shape[0]
    pad = HEAD_SLAB - MLA_NOPE - MLA_ROPE
    cosq = np.concatenate([np.ones((n, MLA_NOPE), np.float32), cos, cos, np.zeros((n, pad), np.float32)], axis=1)
    sinq = np.concatenate([np.zeros((n, MLA_NOPE), np.float32), -sin, sin, np.zeros((n, pad), np.float32)], axis=1)
    cosk = np.concatenate([cos, cos], axis=1)
    sink = np.concatenate([-sin, sin], axis=1)
    tabs = dict(cosq=cosq, sinq=sinq, cosk=cosk, sink=sink)
    return {k: jnp.asarray(np.tile(v, (reps, 1))) for k, v in tabs.items()}


def _block_diag_state(state):
    b, hh, dk, dv = state.shape
    eye = jnp.eye(hh, dtype=state.dtype)
    return (state[:, :, :, None, :] * eye[None, :, None, :, None]).reshape(b, hh * dk, hh * dv)


def _head_states(s_bd, hh):
    dk, dv = s_bd.shape[1] // hh, s_bd.shape[2] // hh
    return jnp.stack([s_bd[:, hd * dk:(hd + 1) * dk, hd * dv:(hd + 1) * dv] for hd in range(hh)], axis=1)


def _group_layer(x, mods, lw, tabs, layer, bb, tt, rtt, s_hg, s_gla, attn_fn, transpose_v):
    sh1, sc1, g1, sh2, sc2, g2 = mods
    ckv, kr, q, k, v, hrec, grec, gate = _inproj_call(x, sh1, sc1, lw, tabs, bb, tt, layer, transpose_v)
    om = attn_fn(q, k, v)
    oh, shg = _recur_call(hrec, s_hg, HG_HEADS * HG_DK, rtt, "recur_hgrn")
    og, sgl = _recur_call(grec, s_gla, GLA_HEADS * GLA_DK, rtt, "recur_gla")
    y = _outffn_call(x, om, oh, og, gate, (g1, sh2, sc2, g2), lw, bb, tt)
    return y, ckv, kr, _head_states(shg, HG_HEADS), _head_states(sgl, GLA_HEADS)


def kernel(x_prompt, x_sample, c_prompt, c_sample, cache_mla_ckv, cache_mla_krope, state_hgrn, state_gla, w_mod, b_mod, norm_mix, norm_ffn, w_in, mla_cq_norm, mla_ckv_norm, mla_w_uq, mla_w_uk, mla_w_uv, mla_q_norm, mla_k_norm, mla_out_norm, hg_lb_logits, hg_out_norm, gla_w_a2, gla_b_a, gla_out_norm, w_o, w_ffn_in, w_ffn_out):
    bp, t_p, d = x_prompt.shape
    bs, t_s, _ = x_sample.shape
    depth = w_mod.shape[0]
    tm = 512
    tt_p = min(tm, t_p)
    bb_s = max(1, min(bs, tm // t_s))
    rtt_p = min(256, t_p)
    rtt_s = min(128, t_s)
    attn_tile = tt_p
    tkc = min(1024, cache_mla_ckv.shape[2])

    tabs_p = _rope_tables(np.arange(t_p), 1)
    tabs_s = _rope_tables(PAST_LEN + np.arange(t_s), bb_s)

    n_c = bp + bs
    c_rows = -(-n_c // 16) * 16
    c_pad = jnp.concatenate([c_prompt, c_sample, jnp.zeros((c_rows - n_c, d), c_prompt.dtype)], axis=0)

    cache_kr_t = jnp.swapaxes(cache_mla_krope, 2, 3)
    xp, xs = x_prompt, x_sample
    outs = [[] for _ in range(8)]
    for l in range(depth):
        lw = _layer_weights(l, w_in, norm_mix, norm_ffn, mla_cq_norm, mla_ckv_norm, mla_w_uq, mla_w_uk, mla_w_uv,
                            mla_q_norm, mla_k_norm, mla_out_norm, hg_lb_logits, hg_out_norm, gla_w_a2, gla_b_a,
                            gla_out_norm, w_o, w_ffn_in, w_ffn_out)
        mod = _mod_call(c_pad, w_mod, b_mod[:, None, :], l)
        mods_p = [mod[0:bp, i * d:(i + 1) * d][:, None, :] for i in range(6)]
        mods_s = [mod[bp:n_c, i * d:(i + 1) * d][:, None, :] for i in range(6)]

        zero_hg = jnp.zeros((bp, HG_HEADS * HG_DK, HG_HEADS * HG_DV), F32)
        zero_gla = jnp.zeros((bp, GLA_HEADS * GLA_DK, GLA_HEADS * GLA_DV), F32)
        xp, a, b_, c_, d_ = _group_layer(xp, mods_p, lw, tabs_p, l, 1, tt_p, rtt_p, zero_hg, zero_gla,
                                         lambda q, k, vt: _attn_prompt_call(lw["logit_bound"], q, k, vt, attn_tile), True)
        for lst, val in zip(outs[0:4], (a, b_, c_, d_)):
            lst.append(val)
        xs, a, b_, c_, d_ = _group_layer(xs, mods_s, lw, tabs_s, l, bb_s, t_s, rtt_s,
                                         _block_diag_state(state_hgrn[l]), _block_diag_state(state_gla[l]),
                                         lambda q, k, v: _attn_sample_call(lw["logit_bound"], q, k, v, cache_mla_ckv,
                                                                           cache_kr_t, lw, l, tkc), False)
        for lst, val in zip(outs[4:8], (a, b_, c_, d_)):
            lst.append(val)

    return (xp, xs) + tuple(jnp.stack(o) for o in outs)
```

```python
import functools

import numpy as np
import jax
import jax.numpy as jnp
from jax import lax
from jax.experimental import pallas as pl
from jax.experimental.pallas import tpu as pltpu

F32, BF16 = jnp.float32, jnp.bfloat16

D_MODEL = 1024
PAST_LEN = 4096
CHUNK = 64
EPS = 1e-6
NEG_INF = -1e30
ROPE_THETA = 10000.0
MLA_HEADS = 8
MLA_NOPE = 64
MLA_ROPE = 32
MLA_V = 64
MLA_Q_RANK = 256
MLA_KV_RANK = 256
MLA_WIDTH = MLA_HEADS * MLA_V
MLA_SCALE = (MLA_NOPE + MLA_ROPE) ** -0.5
LOG2E = 1.4426950408889634
HG_HEADS = 4
HG_DK = 64
HG_DV = 64
GLA_HEADS = 4
GLA_DK = 32
GLA_DV = 64
GLA_RANK = 16
GLA_TAU = 16.0
D_FF = 2816

LANES = 128
HEAD_SLAB = 128
SUB = 16
MAX_CENTRED_EXPONENT = 110.0
ZCOLS = 2432
VMEM_LIMIT = 56 * 1024 * 1024

_Z = dict(cq=0, ckv=256, hq=512, hf=768, hi=1024, hg=1280, gq=1536, gk=1664, gv=1792, gg=2048, misc=2304)


def _bdot(a, b):
    return jnp.dot(a.astype(BF16), b.astype(BF16), preferred_element_type=F32)


def _dot_nt(a, b):
    return lax.dot_general(a, b, (((1,), (1,)), ((), ())), preferred_element_type=F32)


def _dot_tn(a, b):
    return lax.dot_general(a, b, (((0,), (0,)), ((), ())), preferred_element_type=F32)


def _sigmoid(x):
    return 1.0 / (1.0 + jnp.exp(-x))


def _silu(x):
    return x * _sigmoid(x)


def _log_sigmoid(x):
    return jnp.minimum(x, 0.0) - jnp.log1p(jnp.exp(-jnp.abs(x)))


def _rms(x):
    return x * lax.rsqrt(jnp.mean(x * x, axis=-1, keepdims=True) + EPS)


def _params(sem):
    return pltpu.CompilerParams(dimension_semantics=sem, vmem_limit_bytes=VMEM_LIMIT)


def _const_spec(shape):
    nd = len(shape)
    return pl.BlockSpec(shape, lambda *_: (0,) * nd)


def _mod_body(c_ref, w_ref, b_ref, o_ref):
    o_ref[...] = _bdot(_silu(c_ref[...]), w_ref[0]) + b_ref[0]


def _mod_call(c_pad, w_mod, b_mod, layer):
    rows, d = c_pad.shape
    n = w_mod.shape[2]
    tn = 1024
    return pl.pallas_call(
        _mod_body,
        grid=(n // tn,),
        in_specs=[pl.BlockSpec((rows, d), lambda j: (0, 0)),
                  pl.BlockSpec((1, d, tn), lambda j: (layer, 0, j)),
                  pl.BlockSpec((1, 1, tn), lambda j: (layer, 0, j))],
        out_specs=pl.BlockSpec((rows, tn), lambda j: (0, j)),
        out_shape=jax.ShapeDtypeStruct((rows, n), F32),
        compiler_params=_params(("arbitrary",)),
        name="mod",
    )(c_pad, w_mod, b_mod)


def _mla_kv(ckvn_bf, kro_bf, w_uk_ref, w_uv_ref, mk_ref, kgk_ref, tile_ref, transpose_v=False, kro_transposed=False):
    place = _dot_tn if kro_transposed else functools.partial(jnp.dot, preferred_element_type=F32)
    krt = place(kro_bf, tile_ref[...])
    kgk2 = jnp.concatenate([kgk_ref[...], kgk_ref[...]], axis=-1)
    parts = []
    for hp in range(MLA_HEADS // 2):
        sl = slice(2 * HEAD_SLAB * hp, 2 * HEAD_SLAB * (hp + 1))
        sk = jnp.dot(ckvn_bf, w_uk_ref[:, sl], preferred_element_type=F32)
        msk = jnp.dot((sk * sk).astype(BF16), mk_ref[...], preferred_element_type=F32)
        parts.append((sk * lax.rsqrt(msk + EPS) * kgk2 + krt).astype(BF16))
    k = jnp.concatenate(parts, axis=-1)
    if transpose_v:
        v = _dot_nt(w_uv_ref[...], ckvn_bf).astype(BF16)
    else:
        v = jnp.dot(ckvn_bf, w_uv_ref[...], preferred_element_type=F32).astype(BF16)
    return k, v


def _inproj_body(x_ref, sh_ref, sc_ref, nm_ref, win_ref, cqn_ref, ckvn_ref, wuq_ref, wuk_ref, wuv_ref,
                 qga_ref, qgb_ref, cosq_ref, sinq_ref, mq_ref, mk_ref, kgk_ref, gkr_ref, cosk_ref, sink_ref,
                 tile_ref, lbl_ref, wa2_ref, ba_ref,
                 ckv_o, kr_o, q_o, k_o, v_o, hrec_o, grec_o, gate_o, *, layer, transpose_v):
    bb, tt, d = x_ref.shape
    tm = bb * tt
    x = x_ref[...]
    h = _rms(x) * nm_ref[...] * (1.0 + sc_ref[...]) + sh_ref[...]
    z = jnp.dot(h.reshape(tm, d).astype(BF16), win_ref[...], preferred_element_type=F32)

    def seg(name, w):
        return z[:, _Z[name]:_Z[name] + w]

    cqn = _rms(seg("cq", MLA_Q_RANK)) * cqn_ref[...]
    ckvn = _rms(seg("ckv", MLA_KV_RANK)) * ckvn_ref[...]
    ckv_o[...] = ckvn.reshape(bb, tt, MLA_KV_RANK)
    misc = seg("misc", LANES)

    qab = jnp.dot(cqn.astype(BF16), wuq_ref[...], preferred_element_type=F32)
    width = MLA_HEADS * HEAD_SLAB
    ca = cosq_ref[...] * qga_ref[...]
    sb = sinq_ref[...] * qgb_ref[...]
    ca2 = jnp.concatenate([ca, ca], axis=-1)
    sb2 = jnp.concatenate([sb, sb], axis=-1)
    for hp in range(MLA_HEADS // 2):
        sl = slice(2 * HEAD_SLAB * hp, 2 * HEAD_SLAB * (hp + 1))
        qa = qab[:, sl]
        qb = qab[:, width + sl.start:width + sl.stop]
        msq = jnp.dot((qa * qa).astype(BF16), mq_ref[...], preferred_element_type=F32)
        qo = lax.rsqrt(msq + EPS) * (qa * ca2 + qb * sb2) * (MLA_SCALE * LOG2E)
        q_o[:, :, sl] = qo.astype(BF16).reshape(bb, tt, 2 * HEAD_SLAB)

    kr = misc[:, 0:MLA_ROPE]
    krn = _rms(kr) * gkr_ref[...]
    half = MLA_ROPE // 2
    swapped = jnp.concatenate([krn[:, half:], krn[:, :half]], axis=-1)
    kro = krn * cosk_ref[...] + swapped * sink_ref[...]
    kr_o[...] = kro.reshape(bb, tt, MLA_ROPE)
    k, v = _mla_kv(ckvn.astype(BF16), kro.astype(BF16), wuk_ref, wuv_ref, mk_ref, kgk_ref, tile_ref, transpose_v)
    k_o[...] = k.reshape(bb, tt, width)
    if transpose_v:
        v_o[0, 0] = v
    else:
        v_o[...] = v.reshape(bb, tt, MLA_WIDTH)

    lg = lbl_ref[...]
    e = jnp.exp(lg - jnp.max(lg, axis=0, keepdims=True))
    p = e / jnp.sum(e, axis=0, keepdims=True)
    lb = jnp.sum(p[0:layer + 1], axis=0, keepdims=True) - p[0:1]
    f = lb + (1.0 - lb) * _sigmoid(seg("hf", 256))
    hrec_o[:, :, 0:256] = seg("hq", 256).reshape(bb, tt, 256)
    hrec_o[:, :, 256:512] = (1.0 - f).reshape(bb, tt, 256)
    hrec_o[:, :, 512:768] = jnp.log(f).reshape(bb, tt, 256)
    hrec_o[:, :, 768:1024] = seg("hi", 256).reshape(bb, tt, 256)

    ga = misc[:, MLA_ROPE:MLA_ROPE + GLA_RANK]
    apre = jnp.dot(ga.astype(BF16), wa2_ref[...], preferred_element_type=F32) + ba_ref[...]
    grec_o[:, :, 0:128] = (seg("gq", 128) * (GLA_DK ** -0.5)).reshape(bb, tt, 128)
    grec_o[:, :, 128:256] = seg("gk", 128).reshape(bb, tt, 128)
    grec_o[:, :, 256:384] = (_log_sigmoid(apre) / GLA_TAU).reshape(bb, tt, 128)
    grec_o[:, :, 384:640] = seg("gv", 256).reshape(bb, tt, 256)

    gate_o[:, :, 0:256] = _silu(seg("hg", 256)).reshape(bb, tt, 256)
    gate_o[:, :, 256:512] = _silu(seg("gg", 256)).reshape(bb, tt, 256)


def _inproj_call(x, sh, sc, lw, tabs, bb, tt, layer, transpose_v):
    b, t, d = x.shape
    tm = bb * tt
    grid = (b // bb, t // tt)
    tok = lambda w: pl.BlockSpec((bb, tt, w), lambda i, j: (i, j, 0))
    modspec = pl.BlockSpec((bb, 1, d), lambda i, j: (i, 0, 0))
    tab = lambda w: pl.BlockSpec((tm, w), lambda i, j: (j, 0))
    assert bb == 1 or not transpose_v
    consts = [lw["nm"], lw["win"], lw["cqn"], lw["ckvn"], lw["wuq"], lw["wuk"], lw["wuvt" if transpose_v else "wuv"],
              lw["qga"], lw["qgb"]]
    consts2 = [lw["mq"], lw["mk"], lw["kgk"], lw["gkr"]]
    consts3 = [lw["tile"], lw["lbl"], lw["wa2"], lw["ba"]]
    in_specs = ([tok(d), modspec, modspec] + [_const_spec(a.shape) for a in consts]
                + [tab(LANES), tab(LANES)] + [_const_spec(a.shape) for a in consts2]
                + [tab(MLA_ROPE), tab(MLA_ROPE)] + [_const_spec(a.shape) for a in consts3])
    widths = [(MLA_KV_RANK, F32), (MLA_ROPE, F32), (MLA_HEADS * HEAD_SLAB, BF16), (MLA_HEADS * HEAD_SLAB, BF16),
              (MLA_WIDTH, BF16), (1024, F32), (640, F32), (512, F32)]
    out_specs = [tok(w) for w, _ in widths]
    out_shape = [jax.ShapeDtypeStruct((b, t, w), dt) for w, dt in widths]
    if transpose_v:
        out_specs[4] = pl.BlockSpec((1, 1, MLA_WIDTH, tt), lambda i, j: (i, j, 0, 0))
        out_shape[4] = jax.ShapeDtypeStruct((b, t // tt, MLA_WIDTH, tt), BF16)
    return pl.pallas_call(
        functools.partial(_inproj_body, layer=layer, transpose_v=transpose_v),
        grid=grid,
        in_specs=in_specs,
        out_specs=out_specs,
        out_shape=out_shape,
        compiler_params=_params(("arbitrary", "arbitrary")),
        name="inproj",
    )(x, sh, sc, *consts, tabs["cosq"], tabs["sinq"], *consts2, tabs["cosk"], tabs["sink"], *consts3)


MAX_UNSHIFTED_LOGIT = 60.0
SMALL_SCORE_TILE = 64 * 1024


def _attend_bounded(q, k, v, lf_ref, accf_ref, bias):
    tq, tk = q.shape[0], k.shape[0]

    def scores(hd):
        sl = slice(HEAD_SLAB * hd, HEAD_SLAB * (hd + 1))
        s = _dot_nt(q[:, sl], k[:, sl])
        return s if bias is None else s + bias

    def accumulate(hd, s):
        p = jnp.exp2(s)
        if tk % LANES == 0:
            part = p[:, 0:LANES]
            for c in range(1, tk // LANES):
                part = part + p[:, LANES * c:LANES * (c + 1)]
        else:
            lane0 = lax.broadcasted_iota(jnp.int32, (tq, LANES), 1) == 0
            part = jnp.where(lane0, jnp.sum(p, axis=-1, keepdims=True), 0.0)
        lf_ref[hd] += part
        hp = hd // 2
        accf_ref[hd] += jnp.dot(p.astype(BF16), v[:, LANES * hp:LANES * (hp + 1)], preferred_element_type=F32)

    if tq * tk <= SMALL_SCORE_TILE:
        ss = [scores(hd) for hd in range(MLA_HEADS)]
        for hd in range(MLA_HEADS):
            accumulate(hd, ss[hd])
    else:
        for hd in range(MLA_HEADS):
            accumulate(hd, scores(hd))


def _finish_bounded(lf_ref, accf_ref):
    tq = accf_ref.shape[1]
    low = lax.broadcasted_iota(jnp.int32, (tq, LANES), 1) < MLA_V
    outs = []
    for hp in range(MLA_HEADS // 2):
        la = jnp.sum(lf_ref[2 * hp], axis=-1, keepdims=True)
        lb = jnp.sum(lf_ref[2 * hp + 1], axis=-1, keepdims=True)
        outs.append(jnp.where(low, accf_ref[2 * hp] / la, accf_ref[2 * hp + 1] / lb))
    return jnp.concatenate(outs, axis=-1)


def _attend_online(q, k, v, m_ref, l_ref, acc_ref, bias):
    tq = q.shape[0]
    low = lax.broadcasted_iota(jnp.int32, (tq, LANES), 1) < MLA_V
    for hp in range(MLA_HEADS // 2):
        vp = v[:, LANES * hp:LANES * (hp + 1)]
        pvs, alphas = [], []
        for e in range(2):
            hd = 2 * hp + e
            sl = slice(HEAD_SLAB * hd, HEAD_SLAB * (hd + 1))
            s = _dot_nt(q[:, sl], k[:, sl])
            if bias is not None:
                s = s + bias
            m_prev = m_ref[hd]
            m_new = jnp.maximum(m_prev, jnp.max(s, axis=-1, keepdims=True))
            p = jnp.exp2(s - m_new)
            alpha = jnp.exp2(m_prev - m_new)
            l_ref[hd] = alpha * l_ref[hd] + jnp.sum(p, axis=-1, keepdims=True)
            m_ref[hd] = m_new
            pvs.append(jnp.dot(p.astype(BF16), vp, preferred_element_type=F32))
            alphas.append(alpha)
        sl2 = slice(LANES * hp, LANES * (hp + 1))
        acc_ref[:, sl2] = jnp.where(low, alphas[0], alphas[1]) * acc_ref[:, sl2] + jnp.where(low, pvs[0], pvs[1])


def _finish_online(l_ref, acc_ref):
    tq = acc_ref.shape[0]
    low = lax.broadcasted_iota(jnp.int32, (tq, LANES), 1) < MLA_V
    outs = []
    for hp in range(MLA_HEADS // 2):
        l2 = jnp.where(low, l_ref[2 * hp], l_ref[2 * hp + 1])
        outs.append(acc_ref[:, LANES * hp:LANES * (hp + 1)] / l2)
    return jnp.concatenate(outs, axis=-1)


def _softmax_scratch(tq):
    return [pltpu.VMEM((MLA_HEADS, tq, 1), F32), pltpu.VMEM((MLA_HEADS, tq, 1), F32), pltpu.VMEM((tq, MLA_WIDTH), F32),
            pltpu.VMEM((MLA_HEADS, tq, LANES), F32), pltpu.VMEM((MLA_HEADS, tq, LANES), F32)]


def _softmax_init(m_ref, l_ref, acc_ref, lf_ref, accf_ref):
    m_ref[...] = jnp.full(m_ref.shape, NEG_INF, F32)
    for ref in (l_ref, acc_ref, lf_ref, accf_ref):
        ref[...] = jnp.zeros(ref.shape, F32)


def _attend(bounded, cond, get_qkv, scratch, bias_ref):
    m_ref, l_ref, acc_ref, lf_ref, accf_ref = scratch

    @pl.when(jnp.logical_and(cond, bounded))
    def _():
        _attend_bounded(*get_qkv(), lf_ref, accf_ref, None if bias_ref is None else bias_ref[...])

    @pl.when(jnp.logical_and(cond, jnp.logical_not(bounded)))
    def _():
        _attend_online(*get_qkv(), m_ref, l_ref, acc_ref, None if bias_ref is None else bias_ref[...])


def _softmax_finish(bounded, cond, o_ref, scratch):
    m_ref, l_ref, acc_ref, lf_ref, accf_ref = scratch

    @pl.when(jnp.logical_and(cond, bounded))
    def _():
        o_ref[0] = _finish_bounded(lf_ref, accf_ref)

    @pl.when(jnp.logical_and(cond, jnp.logical_not(bounded)))
    def _():
        o_ref[0] = _finish_online(l_ref, acc_ref)


def _attend_t_bounded(q, k, vt, lf_ref, acc_ref, bias_t):
    tq, tk = q.shape[0], k.shape[0]

    def scores_t(hd):
        sl = slice(HEAD_SLAB * hd, HEAD_SLAB * (hd + 1))
        st = _dot_nt(k[:, sl], q[:, sl])
        return st if bias_t is None else st + bias_t

    st_next = scores_t(0)
    for hd in range(MLA_HEADS):
        st = st_next
        if hd + 1 < MLA_HEADS:
            st_next = scores_t(hd + 1)
        pt = jnp.exp2(st)
        lf_ref[hd] += jnp.sum(pt.reshape(tk // 8, 8, tq), axis=0)
        rows = slice(MLA_V * hd, MLA_V * (hd + 1))
        acc_ref[rows, :] += jnp.dot(vt[rows, :], pt.astype(BF16), preferred_element_type=F32)


def _attend_t_online(q, k, vt, m_ref, l_ref, acc_ref, bias_t):
    for hd in range(MLA_HEADS):
        sl = slice(HEAD_SLAB * hd, HEAD_SLAB * (hd + 1))
        st = _dot_nt(k[:, sl], q[:, sl])
        if bias_t is not None:
            st = st + bias_t
        m_prev = m_ref[hd]
        m_new = jnp.maximum(m_prev, jnp.max(st, axis=0, keepdims=True))
        pt = jnp.exp2(st - m_new)
        alpha = jnp.exp2(m_prev - m_new)
        l_ref[hd] = alpha * l_ref[hd] + jnp.sum(pt, axis=0, keepdims=True)
        m_ref[hd] = m_new
        rows = slice(MLA_V * hd, MLA_V * (hd + 1))
        acc_ref[rows, :] = alpha * acc_ref[rows, :] + jnp.dot(vt[rows, :], pt.astype(BF16), preferred_element_type=F32)


def _finish_t(l_of_head, acc_ref):
    outs = []
    for hd in range(MLA_HEADS):
        outs.append(acc_ref[MLA_V * hd:MLA_V * (hd + 1), :] / l_of_head(hd))
    return jnp.concatenate(outs, axis=0).T


def _attn_prompt_body(qi_ref, ki_ref, bound_ref, bias_ref, q_ref, k_ref, vt_ref, o_ref, m_ref, l_ref, lf_ref, acc_ref):
    step = pl.program_id(1)
    i = qi_ref[step]
    j = ki_ref[step]
    bounded = bound_ref[0] <= MAX_UNSHIFTED_LOGIT
    unbounded = jnp.logical_not(bounded)
    diag = j == i

    @pl.when(j == 0)
    def _():
        m_ref[...] = jnp.full(m_ref.shape, NEG_INF, F32)
        for ref in (l_ref, lf_ref, acc_ref):
            ref[...] = jnp.zeros(ref.shape, F32)

    for on_diag in (False, True):
        cond = diag if on_diag else j < i

        @pl.when(jnp.logical_and(cond, bounded))
        def _():
            _attend_t_bounded(q_ref[0], k_ref[0], vt_ref[0, 0], lf_ref, acc_ref, bias_ref[...] if on_diag else None)

        @pl.when(jnp.logical_and(cond, unbounded))
        def _():
            _attend_t_online(q_ref[0], k_ref[0], vt_ref[0, 0], m_ref, l_ref, acc_ref, bias_ref[...] if on_diag else None)

    @pl.when(jnp.logical_and(diag, bounded))
    def _():
        o_ref[0] = _finish_t(lambda hd: jnp.sum(lf_ref[hd], axis=0, keepdims=True), acc_ref)

    @pl.when(jnp.logical_and(diag, unbounded))
    def _():
        o_ref[0] = _finish_t(lambda hd: l_ref[hd], acc_ref)


def _attn_prompt_call(bound, q, k, vt, tile):
    b, t, _ = q.shape
    n = t // tile
    assert vt.shape == (b, n, MLA_WIDTH, tile)
    pairs = [(i, j) for i in range(n) for j in range(i + 1)]
    qi = jnp.asarray([p[0] for p in pairs], jnp.int32)
    ki = jnp.asarray([p[1] for p in pairs], jnp.int32)
    q_map = lambda bi, st, qi_ref, ki_ref: (bi, qi_ref[st], 0)
    k_map = lambda bi, st, qi_ref, ki_ref: (bi, ki_ref[st], 0)
    vt_map = lambda bi, st, qi_ref, ki_ref: (bi, ki_ref[st], 0, 0)
    chunk_of = np.arange(tile) // CHUNK
    bias_t = jnp.asarray(np.where(chunk_of[:, None] <= chunk_of[None, :], 0.0, NEG_INF), F32)
    grid_spec = pltpu.PrefetchScalarGridSpec(
        num_scalar_prefetch=2,
        grid=(b, len(pairs)),
        in_specs=[pl.BlockSpec(memory_space=pltpu.SMEM), pl.BlockSpec((tile, tile), lambda *_: (0, 0)),
                  pl.BlockSpec((1, tile, q.shape[2]), q_map),
                  pl.BlockSpec((1, tile, k.shape[2]), k_map),
                  pl.BlockSpec((1, 1, MLA_WIDTH, tile), vt_map)],
        out_specs=pl.BlockSpec((1, tile, MLA_WIDTH), q_map),
        scratch_shapes=[pltpu.VMEM((MLA_HEADS, 1, tile), F32), pltpu.VMEM((MLA_HEADS, 1, tile), F32),
                        pltpu.VMEM((MLA_HEADS, 8, tile), F32), pltpu.VMEM((MLA_WIDTH, tile), F32)])
    return pl.pallas_call(
        _attn_prompt_body,
        grid_spec=grid_spec,
        out_shape=jax.ShapeDtypeStruct((b, t, MLA_WIDTH), F32),
        compiler_params=_params(("arbitrary", "arbitrary")),
        name="attn_prompt",
    )(qi, ki, bound, bias_t, q, k, vt)


def _attn_sample_body(bound_ref, q_ref, kn_ref, vn_ref, ckv_ref, kr_ref, wuk_ref, wuv_ref, mk_ref, kgk_ref, tile_ref,
                      o_ref, *scratch):
    j = pl.program_id(1)
    bounded = bound_ref[0] <= MAX_UNSHIFTED_LOGIT
    always = j >= 0

    @pl.when(j == 0)
    def _():
        _softmax_init(*scratch)

    _attend(bounded, j == 0, lambda: (q_ref[0], kn_ref[0], vn_ref[0]), scratch, None)

    def past_qkv():
        k, v = _mla_kv(ckv_ref[0, 0].astype(BF16), kr_ref[0, 0].astype(BF16), wuk_ref, wuv_ref, mk_ref, kgk_ref,
                       tile_ref, kro_transposed=True)
        return q_ref[0], k, v

    _attend(bounded, always, past_qkv, scratch, None)
    _softmax_finish(bounded, j == pl.num_programs(1) - 1, o_ref, scratch)


def _attn_sample_call(bound, q, k_new, v_new, cache_ckv, cache_kr_t, lw, layer, tkc):
    b, t, _ = q.shape
    past = cache_ckv.shape[2]
    consts = [lw["wuk"], lw["wuv"], lw["mk"], lw["kgk"], lw["tile"]]
    new = lambda w: pl.BlockSpec((1, t, w), lambda bi, j: (bi, 0, 0))
    return pl.pallas_call(
        _attn_sample_body,
        grid=(b, past // tkc),
        in_specs=[pl.BlockSpec(memory_space=pltpu.SMEM), new(q.shape[2]), new(k_new.shape[2]), new(v_new.shape[2]),
                  pl.BlockSpec((1, 1, tkc, MLA_KV_RANK), lambda bi, j: (layer, bi, j, 0)),
                  pl.BlockSpec((1, 1, MLA_ROPE, tkc), lambda bi, j: (layer, bi, 0, j))]
                 + [_const_spec(a.shape) for a in consts],
        out_specs=new(MLA_WIDTH),
        out_shape=jax.ShapeDtypeStruct((b, t, MLA_WIDTH), F32),
        scratch_shapes=_softmax_scratch(t),
        compiler_params=_params(("arbitrary", "arbitrary")),
        name="attn_sample",
    )(bound, q, k_new, v_new, cache_ckv, cache_kr_t, *consts)


def _cumsum_matrix(tt, blk):
    r = np.arange(tt)
    return (((r[:, None] // blk) == (r[None, :] // blk)) & (r[None, :] <= r[:, None])).astype(np.float32)


def _block_cumsum(g, lmat_ref, blk):
    tt, c = g.shape
    lmat = lmat_ref[...]
    g1 = g.astype(BF16)
    r1 = g - g1.astype(F32)
    g2 = r1.astype(BF16)
    g3 = (r1 - g2.astype(F32)).astype(BF16)
    bb = jnp.dot(lmat, jnp.concatenate([g1, g2, g3], axis=-1), preferred_element_type=F32)
    b = (bb[:, 2 * c:3 * c] + bb[:, c:2 * c]) + bb[:, 0:c]
    last = [jnp.broadcast_to(b[blk * (n + 1) - 1:blk * (n + 1)], (blk, c)) for n in range(tt // blk)]
    return b, jnp.concatenate(last, axis=0)


def _state_chain(blk, qe, ke, vb, dec, o_in_block, bd_ref, o_ref, s_ref):
    tt = qe.shape[0]
    nb = tt // blk
    picked = [dec[blk * n:blk * n + 1] for n in range(nb)]
    if nb % 8:
        picked.append(jnp.zeros((8 - nb % 8, dec.shape[1]), F32))
    dsel = jnp.concatenate(picked, axis=0).T
    us = [_dot_tn(ke[blk * n:blk * (n + 1)], vb[blk * n:blk * (n + 1)]) * bd_ref[...] for n in range(nb)]
    s = s_ref[...]
    for n in range(nb):
        rows = slice(blk * n, blk * (n + 1))
        o_ref[0, rows, :] = jnp.dot(qe[rows], s.astype(BF16), preferred_element_type=F32) + o_in_block(n)
        s = dsel[:, n:n + 1] * s + us[n]
    s_ref[...] = s


def _recur_factorised(q, k, v, b2, bt2, chunk, bd_ref, o_ref, s_ref, hk):
    heads = 4
    dk = hk // heads
    tt = q.shape[0]
    ref = 0.5 * bt2
    qe = (q * jnp.exp2(b2)).astype(BF16)
    qh = (q * jnp.exp2(b2 - ref)).astype(BF16)
    kh = (k * jnp.exp2(ref - b2)).astype(BF16)
    ke = (k * jnp.exp2(bt2 - b2)).astype(BF16)
    dec = jnp.exp2(bt2)
    vb = v.astype(BF16)
    rows_k = lax.broadcasted_iota(jnp.int32, (heads * chunk, hk), 0) // chunk
    same_k = rows_k == lax.broadcasted_iota(jnp.int32, (heads * chunk, hk), 1) // dk
    rows_v = lax.broadcasted_iota(jnp.int32, (heads * chunk, 256), 0) // chunk
    same_v = rows_v == lax.broadcasted_iota(jnp.int32, (heads * chunk, 256), 1) // 64
    ii = lax.broadcasted_iota(jnp.int32, (chunk, heads * chunk), 0)
    jj = lax.broadcasted_iota(jnp.int32, (chunk, heads * chunk), 1) % chunk
    causal = jj <= ii
    chunks = [slice(chunk * n, chunk * (n + 1)) for n in range(tt // chunk)]
    scs = []
    for rows in chunks:
        kbd = jnp.where(same_k, jnp.concatenate([kh[rows]] * heads, axis=0), jnp.zeros(same_k.shape, BF16))
        scs.append(_dot_nt(qh[rows], kbd))
    ods = []
    for rows, sc in zip(chunks, scs):
        vbd = jnp.where(same_v, jnp.concatenate([vb[rows]] * heads, axis=0), jnp.zeros(same_v.shape, BF16))
        ods.append(jnp.dot(jnp.where(causal, sc, 0.0).astype(BF16), vbd, preferred_element_type=F32))
    _state_chain(chunk, qe, ke, vb, dec, lambda n: ods[n], bd_ref, o_ref, s_ref)


def _recur_pairwise(q, k, g, v, lsub_ref, e2_ref, sel_ref, bd_ref, o_ref, s_ref):
    tt = q.shape[0]
    nb = tt // SUB
    b, btot = _block_cumsum(g, lsub_ref, SUB)
    b2 = b * LOG2E
    bt2 = btot * LOG2E
    qe = (q * jnp.exp2(b2)).astype(BF16)
    ke = (k * jnp.exp2(bt2 - b2)).astype(BF16)
    dec = jnp.exp2(bt2)
    vb = v.astype(BF16)
    xs = []
    for n in range(nb):
        r0 = SUB * n
        bj, kj = b2[r0:r0 + SUB], k[r0:r0 + SUB]
        for i0 in range(SUB):
            bi = b2[r0 + i0:r0 + i0 + 1]
            qi = q[r0 + i0:r0 + i0 + 1]
            xs.append((jnp.exp2(jnp.minimum(bi - bj, 0.0)) * (kj * qi)).astype(BF16))
    x = jnp.concatenate(xs, axis=0)
    r = jnp.dot(x, e2_ref[...], preferred_element_type=F32)
    y = (r.reshape(nb, SUB, SUB, 256) * v.reshape(nb, 1, SUB, 256)).reshape(nb * SUB * SUB, 256).astype(BF16)
    ods = [jnp.dot(sel_ref[...], y[SUB * SUB * n:SUB * SUB * (n + 1)], preferred_element_type=F32) for n in range(nb)]
    _state_chain(SUB, qe, ke, vb, dec, lambda n: ods[n], bd_ref, o_ref, s_ref)


def _recur_body(rec_ref, s0_ref, lchunk_ref, lsub_ref, e2_ref, sel_ref, bd_ref, o_ref, s_o, s_ref, *, hk):
    t = pl.program_id(1)

    @pl.when(t == 0)
    def _():
        s_ref[...] = s0_ref[0]

    rec = rec_ref[0]
    q = rec[:, 0:hk]
    k = rec[:, hk:2 * hk]
    g = rec[:, 2 * hk:3 * hk]
    v = rec[:, 3 * hk:3 * hk + 256]

    tt, c = g.shape
    chunk = min(CHUNK, tt)
    half = chunk // 2
    b, btot = _block_cumsum(g, lchunk_ref, chunk)
    b2 = b * LOG2E
    bt2 = btot * LOG2E
    whole_ok = jnp.min(b2) >= -2.0 * MAX_CENTRED_EXPONENT

    @pl.when(whole_ok)
    def _():
        _recur_factorised(q, k, v, b2, bt2, chunk, bd_ref, o_ref, s_ref, hk)

    @pl.when(jnp.logical_not(whole_ok))
    def _():
        bh, bth = [], []
        for n in range(tt // chunk):
            mid = jnp.broadcast_to(b2[chunk * n + half - 1:chunk * n + half], (half, c))
            end = jnp.broadcast_to(b2[chunk * (n + 1) - 1:chunk * (n + 1)], (half, c))
            bh += [b2[chunk * n:chunk * n + half], b2[chunk * n + half:chunk * (n + 1)] - mid]
            bth += [mid, end - mid]
        bh = jnp.concatenate(bh, axis=0)
        bth = jnp.concatenate(bth, axis=0)
        half_ok = jnp.min(bh) >= -2.0 * MAX_CENTRED_EXPONENT

        @pl.when(half_ok)
        def _():
            _recur_factorised(q, k, v, bh, bth, half, bd_ref, o_ref, s_ref, hk)

        @pl.when(jnp.logical_not(half_ok))
        def _():
            _recur_pairwise(q, k, g, v, lsub_ref, e2_ref, sel_ref, bd_ref, o_ref, s_ref)

    @pl.when(t == pl.num_programs(1) - 1)
    def _():
        s_o[0] = s_ref[...]


def _recur_call(rec, s0, hk, tt, name):
    b, t, _ = rec.shape
    w = 3 * hk + 256
    heads = 4
    dk = hk // heads
    e2 = np.kron(np.eye(heads), np.ones((dk, 64))).astype(np.float32)
    sel = np.zeros((SUB, SUB * SUB), np.float32)
    for i in range(SUB):
        sel[i, SUB * i:SUB * i + i + 1] = 1.0
    return pl.pallas_call(
        functools.partial(_recur_body, hk=hk),
        grid=(b, t // tt),
        in_specs=[pl.BlockSpec((1, tt, w), lambda bi, ti: (bi, ti, 0)),
                  pl.BlockSpec((1, hk, 256), lambda bi, ti: (bi, 0, 0)),
                  _const_spec((tt, tt)), _const_spec((tt, tt)),
                  _const_spec((hk, 256)), _const_spec((SUB, SUB * SUB)), _const_spec((hk, 256))],
        out_specs=[pl.BlockSpec((1, tt, 256), lambda bi, ti: (bi, ti, 0)),
                   pl.BlockSpec((1, hk, 256), lambda bi, ti: (bi, 0, 0))],
        out_shape=[jax.ShapeDtypeStruct((b, t, 256), F32), jax.ShapeDtypeStruct((b, hk, 256), F32)],
        scratch_shapes=[pltpu.VMEM((hk, 256), F32)],
        compiler_params=_params(("arbitrary", "arbitrary")),
        name=name,
    )(rec, s0, jnp.asarray(_cumsum_matrix(tt, min(CHUNK, tt)), BF16), jnp.asarray(_cumsum_matrix(tt, SUB), BF16),
      jnp.asarray(e2, BF16), jnp.asarray(sel, BF16), jnp.asarray(e2, F32))


def _outffn_body(x_ref, om_ref, oh_ref, og_ref, gate_ref, g1_ref, sh2_ref, sc2_ref, g2_ref,
                 mon_ref, hon_ref, gon_ref, nf_ref, m64_ref, wo_ref, wfi_ref, wfo_ref, y_ref, *, ff_chunk):
    bb, tt, d = x_ref.shape
    tm = bb * tt
    om = om_ref[...].reshape(tm, MLA_WIDTH)
    omn = _rms(om) * mon_ref[...]

    def headnorm(o, gain_ref, gate):
        ms = jnp.dot((o * o).astype(BF16), m64_ref[...], preferred_element_type=F32)
        return o * lax.rsqrt(ms + EPS) * gain_ref[...] * gate

    gate = gate_ref[...].reshape(tm, 512)
    ohn = headnorm(oh_ref[...].reshape(tm, 256), hon_ref, gate[:, 0:256])
    ogn = headnorm(og_ref[...].reshape(tm, 256), gon_ref, gate[:, 256:512])
    mix_in = jnp.concatenate([omn, ohn, ogn], axis=-1).astype(BF16)
    mix = jnp.dot(mix_in, wo_ref[...], preferred_element_type=F32)
    x1 = x_ref[...] + g1_ref[...] * mix.reshape(bb, tt, d)
    h2 = (_rms(x1) * nf_ref[...] * (1.0 + sc2_ref[...]) + sh2_ref[...]).reshape(tm, d).astype(BF16)
    acc = jnp.zeros((tm, d), F32)
    for c in range(D_FF // ff_chunk):
        a = jnp.dot(h2, wfi_ref[:, ff_chunk * c:ff_chunk * (c + 1)], preferred_element_type=F32)
        u = jnp.dot(h2, wfi_ref[:, D_FF + ff_chunk * c:D_FF + ff_chunk * (c + 1)], preferred_element_type=F32)
        act = (_silu(a) * u).astype(BF16)
        acc = acc + jnp.dot(act, wfo_ref[ff_chunk * c:ff_chunk * (c + 1), :], preferred_element_type=F32)
    y_ref[...] = x1 + g2_ref[...] * acc.reshape(bb, tt, d)


def _outffn_call(x, om, oh, og, gate, mods, lw, bb, tt):
    b, t, d = x.shape
    tok = lambda w: pl.BlockSpec((bb, tt, w), lambda i, j: (i, j, 0))
    modspec = pl.BlockSpec((bb, 1, d), lambda i, j: (i, 0, 0))
    consts = [lw["mon"], lw["hon"], lw["gon"], lw["nf"], lw["m64"], lw["wo"], lw["wfi"], lw["wfo"]]
    single = lambda a: pl.BlockSpec(a.shape, lambda *_: (0,) * a.ndim, pipeline_mode=pl.Buffered(1))
    return pl.pallas_call(
        functools.partial(_outffn_body, ff_chunk=256),
        grid=(b // bb, t // tt),
        in_specs=[tok(d), tok(MLA_WIDTH), tok(256), tok(256), tok(512)] + [modspec] * 4 + [single(a) for a in consts],
        out_specs=tok(d),
        out_shape=jax.ShapeDtypeStruct((b, t, d), F32),
        compiler_params=_params(("arbitrary", "arbitrary")),
        name="outffn",
    )(x, om, oh, og, gate, *mods, *consts)


def _layer_weights(l, w_in, norm_mix, norm_ffn, mla_cq_norm, mla_ckv_norm, mla_w_uq, mla_w_uk, mla_w_uv, mla_q_norm,
                   mla_k_norm, mla_out_norm, hg_lb_logits, hg_out_norm, gla_w_a2, gla_b_a, gla_out_norm, w_o,
                   w_ffn_in, w_ffn_out):
    wi = w_in[l]
    d = wi.shape[0]
    o = dict(cq=0, ckv=256, kr=512, hq=544, hf=800, hi=1056, hg=1312, gq=1568, gk=1696, gv=1824, gg=2080, ga=2336)
    cols = [wi[:, o["cq"]:o["cq"] + 256], wi[:, o["ckv"]:o["ckv"] + 256], wi[:, o["hq"]:o["hq"] + 256],
            wi[:, o["hf"]:o["hf"] + 256], wi[:, o["hi"]:o["hi"] + 256], wi[:, o["hg"]:o["hg"] + 256],
            wi[:, o["gq"]:o["gq"] + 128], wi[:, o["gk"]:o["gk"] + 128], wi[:, o["gv"]:o["gv"] + 256],
            wi[:, o["gg"]:o["gg"] + 256], wi[:, o["kr"]:o["kr"] + 32], wi[:, o["ga"]:o["ga"] + 16],
            jnp.zeros((d, LANES - MLA_ROPE - GLA_RANK), wi.dtype)]
    win = jnp.concatenate(cols, axis=1).astype(BF16)
    assert win.shape[1] == ZCOLS

    uq = mla_w_uq[l]
    r = uq.shape[0]
    half = MLA_ROPE // 2
    zpad = jnp.zeros((r, MLA_HEADS, HEAD_SLAB - MLA_NOPE - MLA_ROPE), uq.dtype)
    uq_a = jnp.concatenate([uq, zpad], axis=-1).reshape(r, -1)
    uq_b = jnp.concatenate([jnp.zeros((r, MLA_HEADS, MLA_NOPE), uq.dtype), uq[..., MLA_NOPE + half:],
                            uq[..., MLA_NOPE:MLA_NOPE + half], zpad], axis=-1).reshape(r, -1)
    wuq = jnp.concatenate([uq_a, uq_b], axis=1).astype(BF16)
    uk = mla_w_uk[l]
    wuk = jnp.concatenate([uk, jnp.zeros((r, MLA_HEADS, HEAD_SLAB - MLA_NOPE), uk.dtype)], axis=-1)
    wuk = wuk.reshape(r, -1).astype(BF16)
    wuv = mla_w_uv[l].reshape(r, -1).astype(BF16)
    wuvt = wuv.T

    qn, kn = mla_q_norm[l], mla_k_norm[l]
    z32 = jnp.zeros((HEAD_SLAB - MLA_NOPE - MLA_ROPE,), F32)
    qga = jnp.concatenate([qn, z32])[None]
    qgb = jnp.concatenate([jnp.zeros((MLA_NOPE,), F32), qn[MLA_NOPE + half:], qn[MLA_NOPE:MLA_NOPE + half], z32])[None]
    kgk = jnp.concatenate([kn[:MLA_NOPE], jnp.zeros((HEAD_SLAB - MLA_NOPE,), F32)])[None]
    gkr = kn[MLA_NOPE:][None]

    blk = np.zeros((HEAD_SLAB, HEAD_SLAB), np.float32)
    blk[:MLA_NOPE, :MLA_NOPE] = 1.0 / MLA_NOPE
    mk = np.kron(np.eye(2), blk)
    blk_q = blk.copy()
    blk_q[MLA_NOPE:MLA_NOPE + MLA_ROPE, MLA_NOPE:MLA_NOPE + MLA_ROPE] = 1.0 / MLA_ROPE
    mq = np.kron(np.eye(2), blk_q)
    m64 = np.kron(np.eye(4), np.full((64, 64), 1.0 / 64, np.float32))
    tile = np.zeros((MLA_ROPE, 2 * HEAD_SLAB), np.float32)
    for hd in range(2):
        tile[np.arange(MLA_ROPE), HEAD_SLAB * hd + MLA_NOPE + np.arange(MLA_ROPE)] = 1.0

    def norm_bound(g):
        return jnp.sqrt(MLA_NOPE * jnp.max(jnp.abs(g[:MLA_NOPE])) ** 2 + MLA_ROPE * jnp.max(jnp.abs(g[MLA_NOPE:])) ** 2)
    logit_bound = (MLA_SCALE * norm_bound(qn) * norm_bound(kn)).reshape(1).astype(F32)

    return dict(
        logit_bound=logit_bound, nm=norm_mix[l][None], win=win, cqn=mla_cq_norm[l][None], ckvn=mla_ckv_norm[l][None], wuq=wuq, wuk=wuk,
        wuv=wuv, wuvt=wuvt, qga=qga, qgb=qgb, mq=jnp.asarray(mq, BF16), mk=jnp.asarray(mk, BF16), kgk=kgk, gkr=gkr,
        tile=jnp.asarray(tile, BF16), lbl=hg_lb_logits, wa2=gla_w_a2[l].astype(BF16), ba=gla_b_a[l][None],
        mon=mla_out_norm[l][None], hon=jnp.tile(hg_out_norm[l], HG_HEADS)[None],
        gon=jnp.tile(gla_out_norm[l], GLA_HEADS)[None], nf=norm_ffn[l][None], m64=jnp.asarray(m64, BF16),
        wo=w_o[l].astype(BF16), wfi=w_ffn_in[l].astype(BF16), wfo=w_ffn_out[l].astype(BF16))


def _rope_tables(pos, reps):
    inv = (np.float32(ROPE_THETA) ** (-(np.arange(0, MLA_ROPE, 2, dtype=np.float32) / np.float32(MLA_ROPE)))).astype(np.float32)
    ang = pos.astype(np.float32)[:, None] * inv[None, :]
    cos, sin = np.cos(ang).astype(np.float32), np.sin(ang).astype(np.float32)
    n = pos.shape[0]
    pad = HEAD_SLAB - MLA_NOPE - MLA_ROPE
    cosq = np.concatenate([np.ones((n, MLA_NOPE), np.float32), cos, cos, np.zeros((n, pad), np.float32)], axis=1)
    sinq = np.concatenate([np.zeros((n, MLA_NOPE), np.float32), -sin, sin, np.zeros((n, pad), np.float32)], axis=1)
    cosk = np.concatenate([cos, cos], axis=1)
    sink = np.concatenate([-sin, sin], axis=1)
    tabs = dict(cosq=cosq, sinq=sinq, cosk=cosk, sink=sink)
    return {k: jnp.asarray(np.tile(v, (reps, 1))) for k, v in tabs.items()}


def _block_diag_state(state):
    b, hh, dk, dv = state.shape
    eye = jnp.eye(hh, dtype=state.dtype)
    return (state[:, :, :, None, :] * eye[None, :, None, :, None]).reshape(b, hh * dk, hh * dv)


def _head_states(s_bd, hh):
    dk, dv = s_bd.shape[1] // hh, s_bd.shape[2] // hh
    return jnp.stack([s_bd[:, hd * dk:(hd + 1) * dk, hd * dv:(hd + 1) * dv] for hd in range(hh)], axis=1)


def _group_layer(x, mods, lw, tabs, layer, bb, tt, rtt, s_hg, s_gla, attn_fn, transpose_v):
    sh1, sc1, g1, sh2, sc2, g2 = mods
    ckv, kr, q, k, v, hrec, grec, gate = _inproj_call(x, sh1, sc1, lw, tabs, bb, tt, layer, transpose_v)
    om = attn_fn(q, k, v)
    oh, shg = _recur_call(hrec, s_hg, HG_HEADS * HG_DK, rtt, "recur_hgrn")
    og, sgl = _recur_call(grec, s_gla, GLA_HEADS * GLA_DK, rtt, "recur_gla")
    y = _outffn_call(x, om, oh, og, gate, (g1, sh2, sc2, g2), lw, bb, tt)
    return y, ckv, kr, _head_states(shg, HG_HEADS), _head_states(sgl, GLA_HEADS)


def kernel(x_prompt, x_sample, c_prompt, c_sample, cache_mla_ckv, cache_mla_krope, state_hgrn, state_gla, w_mod, b_mod, norm_mix, norm_ffn, w_in, mla_cq_norm, mla_ckv_norm, mla_w_uq, mla_w_uk, mla_w_uv, mla_q_norm, mla_k_norm, mla_out_norm, hg_lb_logits, hg_out_norm, gla_w_a2, gla_b_a, gla_out_norm, w_o, w_ffn_in, w_ffn_out):
    bp, t_p, d = x_prompt.shape
    bs, t_s, _ = x_sample.shape
    depth = w_mod.shape[0]
    tm = 512
    tt_p = min(tm, t_p)
    bb_s = max(1, min(bs, tm // t_s))
    rtt_p = min(512, t_p)
    rtt_s = min(128, t_s)
    attn_tile = tt_p
    tkc = min(1024, cache_mla_ckv.shape[2])

    tabs_p = _rope_tables(np.arange(t_p), 1)
    tabs_s = _rope_tables(PAST_LEN + np.arange(t_s), bb_s)

    n_c = bp + bs
    c_rows = -(-n_c // 16) * 16
    c_pad = jnp.concatenate([c_prompt, c_sample, jnp.zeros((c_rows - n_c, d), c_prompt.dtype)], axis=0)

    cache_kr_t = jnp.swapaxes(cache_mla_krope, 2, 3)
    xp, xs = x_prompt, x_sample
    outs = [[] for _ in range(8)]
    for l in range(depth):
        lw = _layer_weights(l, w_in, norm_mix, norm_ffn, mla_cq_norm, mla_ckv_norm, mla_w_uq, mla_w_uk, mla_w_uv,
                            mla_q_norm, mla_k_norm, mla_out_norm, hg_lb_logits, hg_out_norm, gla_w_a2, gla_b_a,
                            gla_out_norm, w_o, w_ffn_in, w_ffn_out)
        mod = _mod_call(c_pad, w_mod, b_mod[:, None, :], l)
        mods_p = [mod[0:bp, i * d:(i + 1) * d][:, None, :] for i in range(6)]
        mods_s = [mod[bp:n_c, i * d:(i + 1) * d][:, None, :] for i in range(6)]

        zero_hg = jnp.zeros((bp, HG_HEADS * HG_DK, HG_HEADS * HG_DV), F32)
        zero_gla = jnp.zeros((bp, GLA_HEADS * GLA_DK, GLA_HEADS * GLA_DV), F32)
        xp, a, b_, c_, d_ = _group_layer(xp, mods_p, lw, tabs_p, l, 1, tt_p, rtt_p, zero_hg, zero_gla,
                                         lambda q, k, vt: _attn_prompt_call(lw["logit_bound"], q, k, vt, attn_tile), True)
        for lst, val in zip(outs[0:4], (a, b_, c_, d_)):
            lst.append(val)
        xs, a, b_, c_, d_ = _group_layer(xs, mods_s, lw, tabs_s, l, bb_s, t_s, rtt_s,
                                         _block_diag_state(state_hgrn[l]), _block_diag_state(state_gla[l]),
                                         lambda q, k, v: _attn_sample_call(lw["logit_bound"], q, k, v, cache_mla_ckv,
                                                                           cache_kr_t, lw, l, tkc), False)
        for lst, val in zip(outs[4:8], (a, b_, c_, d_)):
            lst.append(val)

    return (xp, xs) + tuple(jnp.stack(o) for o in outs)
```

```python
import functools

import numpy as np
import jax
import jax.numpy as jnp
from jax import lax
from jax.experimental import pallas as pl
from jax.experimental.pallas import tpu as pltpu

F32, BF16 = jnp.float32, jnp.bfloat16

D_MODEL = 1024
PAST_LEN = 4096
CHUNK = 64
EPS = 1e-6
NEG_INF = -1e30
ROPE_THETA = 10000.0
MLA_HEADS = 8
MLA_NOPE = 64
MLA_ROPE = 32
MLA_V = 64
MLA_Q_RANK = 256
MLA_KV_RANK = 256
MLA_WIDTH = MLA_HEADS * MLA_V
MLA_SCALE = (MLA_NOPE + MLA_ROPE) ** -0.5
LOG2E = 1.4426950408889634
HG_HEADS = 4
HG_DK = 64
HG_DV = 64
GLA_HEADS = 4
GLA_DK = 32
GLA_DV = 64
GLA_RANK = 16
GLA_TAU = 16.0
D_FF = 2816

LANES = 128
HEAD_SLAB = 128
SUB = 16
MAX_CENTRED_EXPONENT = 110.0
ZCOLS = 2432
VMEM_LIMIT = 56 * 1024 * 1024

_Z = dict(cq=0, ckv=256, hq=512, hf=768, hi=1024, hg=1280, gq=1536, gk=1664, gv=1792, gg=2048, misc=2304)


def _bdot(a, b):
    return jnp.dot(a.astype(BF16), b.astype(BF16), preferred_element_type=F32)


def _dot_nt(a, b):
    return lax.dot_general(a, b, (((1,), (1,)), ((), ())), preferred_element_type=F32)


def _dot_tn(a, b):
    return lax.dot_general(a, b, (((0,), (0,)), ((), ())), preferred_element_type=F32)


def _sigmoid(x):
    return 1.0 / (1.0 + jnp.exp(-x))


def _silu(x):
    return x * _sigmoid(x)


def _log_sigmoid(x):
    return jnp.minimum(x, 0.0) - jnp.log1p(jnp.exp(-jnp.abs(x)))


def _rms(x):
    return x * lax.rsqrt(jnp.mean(x * x, axis=-1, keepdims=True) + EPS)


def _params(sem):
    return pltpu.CompilerParams(dimension_semantics=sem, vmem_limit_bytes=VMEM_LIMIT)


def _const_spec(shape):
    nd = len(shape)
    return pl.BlockSpec(shape, lambda *_: (0,) * nd)


def _mod_body(c_ref, w_ref, b_ref, o_ref):
    o_ref[...] = _bdot(_silu(c_ref[...]), w_ref[0]) + b_ref[0]


def _mod_call(c_pad, w_mod, b_mod, layer):
    rows, d = c_pad.shape
    n = w_mod.shape[2]
    tn = 1024
    return pl.pallas_call(
        _mod_body,
        grid=(n // tn,),
        in_specs=[pl.BlockSpec((rows, d), lambda j: (0, 0)),
                  pl.BlockSpec((1, d, tn), lambda j: (layer, 0, j)),
                  pl.BlockSpec((1, 1, tn), lambda j: (layer, 0, j))],
        out_specs=pl.BlockSpec((rows, tn), lambda j: (0, j)),
        out_shape=jax.ShapeDtypeStruct((rows, n), F32),
        compiler_params=_params(("arbitrary",)),
        name="mod",
    )(c_pad, w_mod, b_mod)


def _mla_kv(ckvn_bf, kro_bf, w_uk_ref, w_uv_ref, mk_ref, kgk_ref, tile_ref, transpose_v=False, kro_transposed=False):
    place = _dot_tn if kro_transposed else functools.partial(jnp.dot, preferred_element_type=F32)
    krt = place(kro_bf, tile_ref[...])
    kgk2 = jnp.concatenate([kgk_ref[...], kgk_ref[...]], axis=-1)
    parts = []
    for hp in range(MLA_HEADS // 2):
        sl = slice(2 * HEAD_SLAB * hp, 2 * HEAD_SLAB * (hp + 1))
        sk = jnp.dot(ckvn_bf, w_uk_ref[:, sl], preferred_element_type=F32)
        msk = jnp.dot((sk * sk).astype(BF16), mk_ref[...], preferred_element_type=F32)
        parts.append((sk * lax.rsqrt(msk + EPS) * kgk2 + krt).astype(BF16))
    k = jnp.concatenate(parts, axis=-1)
    if transpose_v:
        v = _dot_nt(w_uv_ref[...], ckvn_bf).astype(BF16)
    else:
        v = jnp.dot(ckvn_bf, w_uv_ref[...], preferred_element_type=F32).astype(BF16)
    return k, v


def _inproj_body(x_ref, sh_ref, sc_ref, nm_ref, win_ref, cqn_ref, ckvn_ref, wuq_ref, wuk_ref, wuv_ref,
                 qga_ref, qgb_ref, cosq_ref, sinq_ref, mq_ref, mk_ref, kgk_ref, gkr_ref, cosk_ref, sink_ref,
                 tile_ref, lbl_ref, wa2_ref, ba_ref,
                 ckv_o, kr_o, q_o, k_o, v_o, hrec_o, grec_o, gate_o, *, layer, transpose_v):
    bb, tt, d = x_ref.shape
    tm = bb * tt
    x = x_ref[...]
    h = _rms(x) * nm_ref[...] * (1.0 + sc_ref[...]) + sh_ref[...]
    z = jnp.dot(h.reshape(tm, d).astype(BF16), win_ref[...], preferred_element_type=F32)

    def seg(name, w):
        return z[:, _Z[name]:_Z[name] + w]

    cqn = _rms(seg("cq", MLA_Q_RANK)) * cqn_ref[...]
    ckvn = _rms(seg("ckv", MLA_KV_RANK)) * ckvn_ref[...]
    ckv_o[...] = ckvn.reshape(bb, tt, MLA_KV_RANK)
    misc = seg("misc", LANES)

    qab = jnp.dot(cqn.astype(BF16), wuq_ref[...], preferred_element_type=F32)
    width = MLA_HEADS * HEAD_SLAB
    ca = cosq_ref[...] * qga_ref[...]
    sb = sinq_ref[...] * qgb_ref[...]
    ca2 = jnp.concatenate([ca, ca], axis=-1)
    sb2 = jnp.concatenate([sb, sb], axis=-1)
    for hp in range(MLA_HEADS // 2):
        sl = slice(2 * HEAD_SLAB * hp, 2 * HEAD_SLAB * (hp + 1))
        qa = qab[:, sl]
        qb = qab[:, width + sl.start:width + sl.stop]
        msq = jnp.dot((qa * qa).astype(BF16), mq_ref[...], preferred_element_type=F32)
        qo = lax.rsqrt(msq + EPS) * (qa * ca2 + qb * sb2) * (MLA_SCALE * LOG2E)
        q_o[:, :, sl] = qo.astype(BF16).reshape(bb, tt, 2 * HEAD_SLAB)

    kr = misc[:, 0:MLA_ROPE]
    krn = _rms(kr) * gkr_ref[...]
    half = MLA_ROPE // 2
    swapped = jnp.concatenate([krn[:, half:], krn[:, :half]], axis=-1)
    kro = krn * cosk_ref[...] + swapped * sink_ref[...]
    kr_o[...] = kro.reshape(bb, tt, MLA_ROPE)
    k, v = _mla_kv(ckvn.astype(BF16), kro.astype(BF16), wuk_ref, wuv_ref, mk_ref, kgk_ref, tile_ref, transpose_v)
    k_o[...] = k.reshape(bb, tt, width)
    if transpose_v:
        v_o[0, 0] = v
    else:
        v_o[...] = v.reshape(bb, tt, MLA_WIDTH)

    lg = lbl_ref[...]
    e = jnp.exp(lg - jnp.max(lg, axis=0, keepdims=True))
    p = e / jnp.sum(e, axis=0, keepdims=True)
    lb = jnp.sum(p[0:layer + 1], axis=0, keepdims=True) - p[0:1]
    f = lb + (1.0 - lb) * _sigmoid(seg("hf", 256))
    hrec_o[:, :, 0:256] = seg("hq", 256).reshape(bb, tt, 256)
    hrec_o[:, :, 256:512] = (1.0 - f).reshape(bb, tt, 256)
    hrec_o[:, :, 512:768] = jnp.log(f).reshape(bb, tt, 256)
    hrec_o[:, :, 768:1024] = seg("hi", 256).reshape(bb, tt, 256)

    ga = misc[:, MLA_ROPE:MLA_ROPE + GLA_RANK]
    apre = jnp.dot(ga.astype(BF16), wa2_ref[...], preferred_element_type=F32) + ba_ref[...]
    grec_o[:, :, 0:128] = (seg("gq", 128) * (GLA_DK ** -0.5)).reshape(bb, tt, 128)
    grec_o[:, :, 128:256] = seg("gk", 128).reshape(bb, tt, 128)
    grec_o[:, :, 256:384] = (_log_sigmoid(apre) / GLA_TAU).reshape(bb, tt, 128)
    grec_o[:, :, 384:640] = seg("gv", 256).reshape(bb, tt, 256)

    gate_o[:, :, 0:256] = _silu(seg("hg", 256)).reshape(bb, tt, 256)
    gate_o[:, :, 256:512] = _silu(seg("gg", 256)).reshape(bb, tt, 256)


def _inproj_call(x, sh, sc, lw, tabs, bb, tt, layer, transpose_v):
    b, t, d = x.shape
    tm = bb * tt
    grid = (b // bb, t // tt)
    tok = lambda w: pl.BlockSpec((bb, tt, w), lambda i, j: (i, j, 0))
    modspec = pl.BlockSpec((bb, 1, d), lambda i, j: (i, 0, 0))
    tab = lambda w: pl.BlockSpec((tm, w), lambda i, j: (j, 0))
    assert bb == 1 or not transpose_v
    consts = [lw["nm"], lw["win"], lw["cqn"], lw["ckvn"], lw["wuq"], lw["wuk"], lw["wuvt" if transpose_v else "wuv"],
              lw["qga"], lw["qgb"]]
    consts2 = [lw["mq"], lw["mk"], lw["kgk"], lw["gkr"]]
    consts3 = [lw["tile"], lw["lbl"], lw["wa2"], lw["ba"]]
    in_specs = ([tok(d), modspec, modspec] + [_const_spec(a.shape) for a in consts]
                + [tab(LANES), tab(LANES)] + [_const_spec(a.shape) for a in consts2]
                + [tab(MLA_ROPE), tab(MLA_ROPE)] + [_const_spec(a.shape) for a in consts3])
    widths = [(MLA_KV_RANK, F32), (MLA_ROPE, F32), (MLA_HEADS * HEAD_SLAB, BF16), (MLA_HEADS * HEAD_SLAB, BF16),
              (MLA_WIDTH, BF16), (1024, F32), (640, F32), (512, F32)]
    out_specs = [tok(w) for w, _ in widths]
    out_shape = [jax.ShapeDtypeStruct((b, t, w), dt) for w, dt in widths]
    if transpose_v:
        out_specs[4] = pl.BlockSpec((1, 1, MLA_WIDTH, tt), lambda i, j: (i, j, 0, 0))
        out_shape[4] = jax.ShapeDtypeStruct((b, t // tt, MLA_WIDTH, tt), BF16)
    return pl.pallas_call(
        functools.partial(_inproj_body, layer=layer, transpose_v=transpose_v),
        grid=grid,
        in_specs=in_specs,
        out_specs=out_specs,
        out_shape=out_shape,
        compiler_params=_params(("arbitrary", "arbitrary")),
        name="inproj",
    )(x, sh, sc, *consts, tabs["cosq"], tabs["sinq"], *consts2, tabs["cosk"], tabs["sink"], *consts3)


MAX_UNSHIFTED_LOGIT = 60.0
SMALL_SCORE_TILE = 64 * 1024


def _attend_bounded(q, k, v, lf_ref, accf_ref, bias):
    tq, tk = q.shape[0], k.shape[0]

    def scores(hd):
        sl = slice(HEAD_SLAB * hd, HEAD_SLAB * (hd + 1))
        s = _dot_nt(q[:, sl], k[:, sl])
        return s if bias is None else s + bias

    def accumulate(hd, s):
        p = jnp.exp2(s)
        if tk % LANES == 0:
            part = p[:, 0:LANES]
            for c in range(1, tk // LANES):
                part = part + p[:, LANES * c:LANES * (c + 1)]
        else:
            lane0 = lax.broadcasted_iota(jnp.int32, (tq, LANES), 1) == 0
            part = jnp.where(lane0, jnp.sum(p, axis=-1, keepdims=True), 0.0)
        lf_ref[hd] += part
        hp = hd // 2
        accf_ref[hd] += jnp.dot(p.astype(BF16), v[:, LANES * hp:LANES * (hp + 1)], preferred_element_type=F32)

    if tq * tk <= SMALL_SCORE_TILE:
        ss = [scores(hd) for hd in range(MLA_HEADS)]
        for hd in range(MLA_HEADS):
            accumulate(hd, ss[hd])
    else:
        for hd in range(MLA_HEADS):
            accumulate(hd, scores(hd))


def _finish_bounded(lf_ref, accf_ref):
    tq = accf_ref.shape[1]
    low = lax.broadcasted_iota(jnp.int32, (tq, LANES), 1) < MLA_V
    outs = []
    for hp in range(MLA_HEADS // 2):
        la = jnp.sum(lf_ref[2 * hp], axis=-1, keepdims=True)
        lb = jnp.sum(lf_ref[2 * hp + 1], axis=-1, keepdims=True)
        outs.append(jnp.where(low, accf_ref[2 * hp] / la, accf_ref[2 * hp + 1] / lb))
    return jnp.concatenate(outs, axis=-1)


def _attend_online(q, k, v, m_ref, l_ref, acc_ref, bias):
    tq = q.shape[0]
    low = lax.broadcasted_iota(jnp.int32, (tq, LANES), 1) < MLA_V
    for hp in range(MLA_HEADS // 2):
        vp = v[:, LANES * hp:LANES * (hp + 1)]
        pvs, alphas = [], []
        for e in range(2):
            hd = 2 * hp + e
            sl = slice(HEAD_SLAB * hd, HEAD_SLAB * (hd + 1))
            s = _dot_nt(q[:, sl], k[:, sl])
            if bias is not None:
                s = s + bias
            m_prev = m_ref[hd]
            m_new = jnp.maximum(m_prev, jnp.max(s, axis=-1, keepdims=True))
            p = jnp.exp2(s - m_new)
            alpha = jnp.exp2(m_prev - m_new)
            l_ref[hd] = alpha * l_ref[hd] + jnp.sum(p, axis=-1, keepdims=True)
            m_ref[hd] = m_new
            pvs.append(jnp.dot(p.astype(BF16), vp, preferred_element_type=F32))
            alphas.append(alpha)
        sl2 = slice(LANES * hp, LANES * (hp + 1))
        acc_ref[:, sl2] = jnp.where(low, alphas[0], alphas[1]) * acc_ref[:, sl2] + jnp.where(low, pvs[0], pvs[1])


def _finish_online(l_ref, acc_ref):
    tq = acc_ref.shape[0]
    low = lax.broadcasted_iota(jnp.int32, (tq, LANES), 1) < MLA_V
    outs = []
    for hp in range(MLA_HEADS // 2):
        l2 = jnp.where(low, l_ref[2 * hp], l_ref[2 * hp + 1])
        outs.append(acc_ref[:, LANES * hp:LANES * (hp + 1)] / l2)
    return jnp.concatenate(outs, axis=-1)


def _softmax_scratch(tq):
    return [pltpu.VMEM((MLA_HEADS, tq, 1), F32), pltpu.VMEM((MLA_HEADS, tq, 1), F32), pltpu.VMEM((tq, MLA_WIDTH), F32),
            pltpu.VMEM((MLA_HEADS, tq, LANES), F32), pltpu.VMEM((MLA_HEADS, tq, LANES), F32)]


def _softmax_init(m_ref, l_ref, acc_ref, lf_ref, accf_ref):
    m_ref[...] = jnp.full(m_ref.shape, NEG_INF, F32)
    for ref in (l_ref, acc_ref, lf_ref, accf_ref):
        ref[...] = jnp.zeros(ref.shape, F32)


def _attend(bounded, cond, get_qkv, scratch, bias_ref):
    m_ref, l_ref, acc_ref, lf_ref, accf_ref = scratch

    @pl.when(jnp.logical_and(cond, bounded))
    def _():
        _attend_bounded(*get_qkv(), lf_ref, accf_ref, None if bias_ref is None else bias_ref[...])

    @pl.when(jnp.logical_and(cond, jnp.logical_not(bounded)))
    def _():
        _attend_online(*get_qkv(), m_ref, l_ref, acc_ref, None if bias_ref is None else bias_ref[...])


def _softmax_finish(bounded, cond, o_ref, scratch):
    m_ref, l_ref, acc_ref, lf_ref, accf_ref = scratch

    @pl.when(jnp.logical_and(cond, bounded))
    def _():
        o_ref[0] = _finish_bounded(lf_ref, accf_ref)

    @pl.when(jnp.logical_and(cond, jnp.logical_not(bounded)))
    def _():
        o_ref[0] = _finish_online(l_ref, acc_ref)


def _attend_t_bounded(q, k, vt, lf_ref, acc_ref, bias_t):
    tq, tk = q.shape[0], k.shape[0]

    def scores_t(hd):
        sl = slice(HEAD_SLAB * hd, HEAD_SLAB * (hd + 1))
        st = _dot_nt(k[:, sl], q[:, sl])
        return st if bias_t is None else st + bias_t

    st_next = scores_t(0)
    for hd in range(MLA_HEADS):
        st = st_next
        if hd + 1 < MLA_HEADS:
            st_next = scores_t(hd + 1)
        pt = jnp.exp2(st)
        lf_ref[hd] += jnp.sum(pt.reshape(tk // 8, 8, tq), axis=0)
        rows = slice(MLA_V * hd, MLA_V * (hd + 1))
        acc_ref[rows, :] += jnp.dot(vt[rows, :], pt.astype(BF16), preferred_element_type=F32)


def _attend_t_online(q, k, vt, m_ref, l_ref, acc_ref, bias_t):
    for hd in range(MLA_HEADS):
        sl = slice(HEAD_SLAB * hd, HEAD_SLAB * (hd + 1))
        st = _dot_nt(k[:, sl], q[:, sl])
        if bias_t is not None:
            st = st + bias_t
        m_prev = m_ref[hd]
        m_new = jnp.maximum(m_prev, jnp.max(st, axis=0, keepdims=True))
        pt = jnp.exp2(st - m_new)
        alpha = jnp.exp2(m_prev - m_new)
        l_ref[hd] = alpha * l_ref[hd] + jnp.sum(pt, axis=0, keepdims=True)
        m_ref[hd] = m_new
        rows = slice(MLA_V * hd, MLA_V * (hd + 1))
        acc_ref[rows, :] = alpha * acc_ref[rows, :] + jnp.dot(vt[rows, :], pt.astype(BF16), preferred_element_type=F32)


def _finish_t(l_of_head, acc_ref):
    outs = []
    for hd in range(MLA_HEADS):
        outs.append(acc_ref[MLA_V * hd:MLA_V * (hd + 1), :] / l_of_head(hd))
    return jnp.concatenate(outs, axis=0).T


def _attn_prompt_body(qi_ref, ki_ref, bound_ref, bias_ref, q_ref, k_ref, vt_ref, o_ref, m_ref, l_ref, lf_ref, acc_ref):
    step = pl.program_id(1)
    i = qi_ref[step]
    j = ki_ref[step]
    bounded = bound_ref[0] <= MAX_UNSHIFTED_LOGIT
    unbounded = jnp.logical_not(bounded)
    diag = j == i

    @pl.when(j == 0)
    def _():
        m_ref[...] = jnp.full(m_ref.shape, NEG_INF, F32)
        for ref in (l_ref, lf_ref, acc_ref):
            ref[...] = jnp.zeros(ref.shape, F32)

    for on_diag in (False, True):
        cond = diag if on_diag else j < i

        @pl.when(jnp.logical_and(cond, bounded))
        def _():
            _attend_t_bounded(q_ref[0], k_ref[0], vt_ref[0, 0], lf_ref, acc_ref, bias_ref[...] if on_diag else None)

        @pl.when(jnp.logical_and(cond, unbounded))
        def _():
            _attend_t_online(q_ref[0], k_ref[0], vt_ref[0, 0], m_ref, l_ref, acc_ref, bias_ref[...] if on_diag else None)

    @pl.when(jnp.logical_and(diag, bounded))
    def _():
        o_ref[0] = _finish_t(lambda hd: jnp.sum(lf_ref[hd], axis=0, keepdims=True), acc_ref)

    @pl.when(jnp.logical_and(diag, unbounded))
    def _():
        o_ref[0] = _finish_t(lambda hd: l_ref[hd], acc_ref)


def _attn_prompt_call(bound, q, k, vt, tile):
    b, t, _ = q.shape
    n = t // tile
    assert vt.shape == (b, n, MLA_WIDTH, tile)
    pairs = [(i, j) for i in range(n) for j in range(i + 1)]
    qi = jnp.asarray([p[0] for p in pairs], jnp.int32)
    ki = jnp.asarray([p[1] for p in pairs], jnp.int32)
    q_map = lambda bi, st, qi_ref, ki_ref: (bi, qi_ref[st], 0)
    k_map = lambda bi, st, qi_ref, ki_ref: (bi, ki_ref[st], 0)
    vt_map = lambda bi, st, qi_ref, ki_ref: (bi, ki_ref[st], 0, 0)
    chunk_of = np.arange(tile) // CHUNK
    bias_t = jnp.asarray(np.where(chunk_of[:, None] <= chunk_of[None, :], 0.0, NEG_INF), F32)
    grid_spec = pltpu.PrefetchScalarGridSpec(
        num_scalar_prefetch=2,
        grid=(b, len(pairs)),
        in_specs=[pl.BlockSpec(memory_space=pltpu.SMEM), pl.BlockSpec((tile, tile), lambda *_: (0, 0)),
                  pl.BlockSpec((1, tile, q.shape[2]), q_map),
                  pl.BlockSpec((1, tile, k.shape[2]), k_map),
                  pl.BlockSpec((1, 1, MLA_WIDTH, tile), vt_map)],
        out_specs=pl.BlockSpec((1, tile, MLA_WIDTH), q_map),
        scratch_shapes=[pltpu.VMEM((MLA_HEADS, 1, tile), F32), pltpu.VMEM((MLA_HEADS, 1, tile), F32),
                        pltpu.VMEM((MLA_HEADS, 8, tile), F32), pltpu.VMEM((MLA_WIDTH, tile), F32)])
    return pl.pallas_call(
        _attn_prompt_body,
        grid_spec=grid_spec,
        out_shape=jax.ShapeDtypeStruct((b, t, MLA_WIDTH), F32),
        compiler_params=_params(("arbitrary", "arbitrary")),
        name="attn_prompt",
    )(qi, ki, bound, bias_t, q, k, vt)


def _attn_sample_body(bound_ref, q_ref, kn_ref, vn_ref, ckv_ref, kr_ref, wuk_ref, wuv_ref, mk_ref, kgk_ref, tile_ref,
                      o_ref, *scratch):
    j = pl.program_id(1)
    bounded = bound_ref[0] <= MAX_UNSHIFTED_LOGIT
    always = j >= 0

    @pl.when(j == 0)
    def _():
        _softmax_init(*scratch)

    _attend(bounded, j == 0, lambda: (q_ref[0], kn_ref[0], vn_ref[0]), scratch, None)

    def past_qkv():
        k, v = _mla_kv(ckv_ref[0, 0].astype(BF16), kr_ref[0, 0].astype(BF16), wuk_ref, wuv_ref, mk_ref, kgk_ref,
                       tile_ref, kro_transposed=True)
        return q_ref[0], k, v

    _attend(bounded, always, past_qkv, scratch, None)
    _softmax_finish(bounded, j == pl.num_programs(1) - 1, o_ref, scratch)


def _attn_sample_call(bound, q, k_new, v_new, cache_ckv, cache_kr_t, lw, layer, tkc):
    b, t, _ = q.shape
    past = cache_ckv.shape[2]
    consts = [lw["wuk"], lw["wuv"], lw["mk"], lw["kgk"], lw["tile"]]
    new = lambda w: pl.BlockSpec((1, t, w), lambda bi, j: (bi, 0, 0))
    return pl.pallas_call(
        _attn_sample_body,
        grid=(b, past // tkc),
        in_specs=[pl.BlockSpec(memory_space=pltpu.SMEM), new(q.shape[2]), new(k_new.shape[2]), new(v_new.shape[2]),
                  pl.BlockSpec((1, 1, tkc, MLA_KV_RANK), lambda bi, j: (layer, bi, j, 0)),
                  pl.BlockSpec((1, 1, MLA_ROPE, tkc), lambda bi, j: (layer, bi, 0, j))]
                 + [_const_spec(a.shape) for a in consts],
        out_specs=new(MLA_WIDTH),
        out_shape=jax.ShapeDtypeStruct((b, t, MLA_WIDTH), F32),
        scratch_shapes=_softmax_scratch(t),
        compiler_params=_params(("arbitrary", "arbitrary")),
        name="attn_sample",
    )(bound, q, k_new, v_new, cache_ckv, cache_kr_t, *consts)


def _cumsum_matrix(tt, blk):
    r = np.arange(tt)
    return (((r[:, None] // blk) == (r[None, :] // blk)) & (r[None, :] <= r[:, None])).astype(np.float32)


def _block_cumsum(g, lmat_ref, blk):
    tt, c = g.shape
    lmat = lmat_ref[...]
    g1 = g.astype(BF16)
    r1 = g - g1.astype(F32)
    g2 = r1.astype(BF16)
    g3 = (r1 - g2.astype(F32)).astype(BF16)
    bb = jnp.dot(lmat, jnp.concatenate([g1, g2, g3], axis=-1), preferred_element_type=F32)
    b = (bb[:, 2 * c:3 * c] + bb[:, c:2 * c]) + bb[:, 0:c]
    last = [jnp.broadcast_to(b[blk * (n + 1) - 1:blk * (n + 1)], (blk, c)) for n in range(tt // blk)]
    return b, jnp.concatenate(last, axis=0)


def _state_chain(blk, qe, ke, vb, dec, o_in_block, bd_ref, o_ref, s_ref):
    tt = qe.shape[0]
    nb = tt // blk
    picked = [dec[blk * n:blk * n + 1] for n in range(nb)]
    if nb % 8:
        picked.append(jnp.zeros((8 - nb % 8, dec.shape[1]), F32))
    dsel = jnp.concatenate(picked, axis=0).T
    us = [_dot_tn(ke[blk * n:blk * (n + 1)], vb[blk * n:blk * (n + 1)]) * bd_ref[...] for n in range(nb)]
    s = s_ref[...]
    for n in range(nb):
        rows = slice(blk * n, blk * (n + 1))
        o_ref[0, rows, :] = jnp.dot(qe[rows], s.astype(BF16), preferred_element_type=F32) + o_in_block(n)
        s = dsel[:, n:n + 1] * s + us[n]
    s_ref[...] = s


def _recur_factorised(q, k, v, b2, bt2, chunk, bd_ref, o_ref, s_ref, hk):
    heads = 4
    dk = hk // heads
    tt = q.shape[0]
    ref = 0.5 * bt2
    qe = (q * jnp.exp2(b2)).astype(BF16)
    qh = (q * jnp.exp2(b2 - ref)).astype(BF16)
    kh = (k * jnp.exp2(ref - b2)).astype(BF16)
    ke = (k * jnp.exp2(bt2 - b2)).astype(BF16)
    dec = jnp.exp2(bt2)
    vb = v.astype(BF16)
    rows_k = lax.broadcasted_iota(jnp.int32, (heads * chunk, hk), 0) // chunk
    same_k = rows_k == lax.broadcasted_iota(jnp.int32, (heads * chunk, hk), 1) // dk
    rows_v = lax.broadcasted_iota(jnp.int32, (heads * chunk, 256), 0) // chunk
    same_v = rows_v == lax.broadcasted_iota(jnp.int32, (heads * chunk, 256), 1) // 64
    ii = lax.broadcasted_iota(jnp.int32, (chunk, heads * chunk), 0)
    jj = lax.broadcasted_iota(jnp.int32, (chunk, heads * chunk), 1) % chunk
    causal = jj <= ii
    chunks = [slice(chunk * n, chunk * (n + 1)) for n in range(tt // chunk)]
    scs = []
    for rows in chunks:
        kbd = jnp.where(same_k, jnp.concatenate([kh[rows]] * heads, axis=0), jnp.zeros(same_k.shape, BF16))
        scs.append(_dot_nt(qh[rows], kbd))
    ods = []
    for rows, sc in zip(chunks, scs):
        vbd = jnp.where(same_v, jnp.concatenate([vb[rows]] * heads, axis=0), jnp.zeros(same_v.shape, BF16))
        ods.append(jnp.dot(jnp.where(causal, sc, 0.0).astype(BF16), vbd, preferred_element_type=F32))
    _state_chain(chunk, qe, ke, vb, dec, lambda n: ods[n], bd_ref, o_ref, s_ref)


def _recur_pairwise(q, k, g, v, lsub_ref, e2_ref, sel_ref, bd_ref, o_ref, s_ref):
    tt = q.shape[0]
    nb = tt // SUB
    b, btot = _block_cumsum(g, lsub_ref, SUB)
    b2 = b * LOG2E
    bt2 = btot * LOG2E
    qe = (q * jnp.exp2(b2)).astype(BF16)
    ke = (k * jnp.exp2(bt2 - b2)).astype(BF16)
    dec = jnp.exp2(bt2)
    vb = v.astype(BF16)
    xs = []
    for n in range(nb):
        r0 = SUB * n
        bj, kj = b2[r0:r0 + SUB], k[r0:r0 + SUB]
        for i0 in range(SUB):
            bi = b2[r0 + i0:r0 + i0 + 1]
            qi = q[r0 + i0:r0 + i0 + 1]
            xs.append((jnp.exp2(jnp.minimum(bi - bj, 0.0)) * (kj * qi)).astype(BF16))
    x = jnp.concatenate(xs, axis=0)
    r = jnp.dot(x, e2_ref[...], preferred_element_type=F32)
    y = (r.reshape(nb, SUB, SUB, 256) * v.reshape(nb, 1, SUB, 256)).reshape(nb * SUB * SUB, 256).astype(BF16)
    ods = [jnp.dot(sel_ref[...], y[SUB * SUB * n:SUB * SUB * (n + 1)], preferred_element_type=F32) for n in range(nb)]
    _state_chain(SUB, qe, ke, vb, dec, lambda n: ods[n], bd_ref, o_ref, s_ref)


def _recur_body(rec_ref, s0_ref, lchunk_ref, lsub_ref, e2_ref, sel_ref, bd_ref, o_ref, s_o, s_ref, *, hk):
    t = pl.program_id(1)

    @pl.when(t == 0)
    def _():
        s_ref[...] = s0_ref[0]

    rec = rec_ref[0]
    q = rec[:, 0:hk]
    k = rec[:, hk:2 * hk]
    g = rec[:, 2 * hk:3 * hk]
    v = rec[:, 3 * hk:3 * hk + 256]

    tt, c = g.shape
    chunk = min(CHUNK, tt)
    half = chunk // 2
    b, btot = _block_cumsum(g, lchunk_ref, chunk)
    b2 = b * LOG2E
    bt2 = btot * LOG2E
    whole_ok = jnp.min(b2) >= -2.0 * MAX_CENTRED_EXPONENT

    @pl.when(whole_ok)
    def _():
        _recur_factorised(q, k, v, b2, bt2, chunk, bd_ref, o_ref, s_ref, hk)

    @pl.when(jnp.logical_not(whole_ok))
    def _():
        bh, bth = [], []
        for n in range(tt // chunk):
            mid = jnp.broadcast_to(b2[chunk * n + half - 1:chunk * n + half], (half, c))
            end = jnp.broadcast_to(b2[chunk * (n + 1) - 1:chunk * (n + 1)], (half, c))
            bh += [b2[chunk * n:chunk * n + half], b2[chunk * n + half:chunk * (n + 1)] - mid]
            bth += [mid, end - mid]
        bh = jnp.concatenate(bh, axis=0)
        bth = jnp.concatenate(bth, axis=0)
        half_ok = jnp.min(bh) >= -2.0 * MAX_CENTRED_EXPONENT

        @pl.when(half_ok)
        def _():
            _recur_factorised(q, k, v, bh, bth, half, bd_ref, o_ref, s_ref, hk)

        @pl.when(jnp.logical_not(half_ok))
        def _():
            _recur_pairwise(q, k, g, v, lsub_ref, e2_ref, sel_ref, bd_ref, o_ref, s_ref)

    @pl.when(t == pl.num_programs(1) - 1)
    def _():
        s_o[0] = s_ref[...]


def _recur_call(rec, s0, hk, tt, name):
    b, t, _ = rec.shape
    w = 3 * hk + 256
    heads = 4
    dk = hk // heads
    e2 = np.kron(np.eye(heads), np.ones((dk, 64))).astype(np.float32)
    sel = np.zeros((SUB, SUB * SUB), np.float32)
    for i in range(SUB):
        sel[i, SUB * i:SUB * i + i + 1] = 1.0
    return pl.pallas_call(
        functools.partial(_recur_body, hk=hk),
        grid=(b, t // tt),
        in_specs=[pl.BlockSpec((1, tt, w), lambda bi, ti: (bi, ti, 0)),
                  pl.BlockSpec((1, hk, 256), lambda bi, ti: (bi, 0, 0)),
                  _const_spec((tt, tt)), _const_spec((tt, tt)),
                  _const_spec((hk, 256)), _const_spec((SUB, SUB * SUB)), _const_spec((hk, 256))],
        out_specs=[pl.BlockSpec((1, tt, 256), lambda bi, ti: (bi, ti, 0)),
                   pl.BlockSpec((1, hk, 256), lambda bi, ti: (bi, 0, 0))],
        out_shape=[jax.ShapeDtypeStruct((b, t, 256), F32), jax.ShapeDtypeStruct((b, hk, 256), F32)],
        scratch_shapes=[pltpu.VMEM((hk, 256), F32)],
        compiler_params=_params(("arbitrary", "arbitrary")),
        name=name,
    )(rec, s0, jnp.asarray(_cumsum_matrix(tt, min(CHUNK, tt)), BF16), jnp.asarray(_cumsum_matrix(tt, SUB), BF16),
      jnp.asarray(e2, BF16), jnp.asarray(sel, BF16), jnp.asarray(e2, F32))


def _recur_pair_body(rh_ref, rg_ref, s0h_ref, s0g_ref, lchunk_ref, lsub_ref, e2h_ref, e2g_ref, sel_ref, bdh_ref,
                     bdg_ref, oh_ref, og_ref, soh, sog, sh_ref, sg_ref, *, hk_h, hk_g):
    _recur_body(rh_ref, s0h_ref, lchunk_ref, lsub_ref, e2h_ref, sel_ref, bdh_ref, oh_ref, soh, sh_ref, hk=hk_h)
    _recur_body(rg_ref, s0g_ref, lchunk_ref, lsub_ref, e2g_ref, sel_ref, bdg_ref, og_ref, sog, sg_ref, hk=hk_g)


def _recur_pair_call(rec_h, rec_g, s0h, s0g, hk_h, hk_g, tt):
    b, t, _ = rec_h.shape
    heads = 4

    def expand(hk):
        return np.kron(np.eye(heads), np.ones((hk // heads, 64))).astype(np.float32)

    sel = np.zeros((SUB, SUB * SUB), np.float32)
    for i in range(SUB):
        sel[i, SUB * i:SUB * i + i + 1] = 1.0
    tile = lambda w: pl.BlockSpec((1, tt, w), lambda bi, ti: (bi, ti, 0))
    state = lambda hk: pl.BlockSpec((1, hk, 256), lambda bi, ti: (bi, 0, 0))
    return pl.pallas_call(
        functools.partial(_recur_pair_body, hk_h=hk_h, hk_g=hk_g),
        grid=(b, t // tt),
        in_specs=[tile(3 * hk_h + 256), tile(3 * hk_g + 256), state(hk_h), state(hk_g),
                  _const_spec((tt, tt)), _const_spec((tt, tt)), _const_spec((hk_h, 256)), _const_spec((hk_g, 256)),
                  _const_spec((SUB, SUB * SUB)), _const_spec((hk_h, 256)), _const_spec((hk_g, 256))],
        out_specs=[tile(256), tile(256), state(hk_h), state(hk_g)],
        out_shape=[jax.ShapeDtypeStruct((b, t, 256), F32), jax.ShapeDtypeStruct((b, t, 256), F32),
                   jax.ShapeDtypeStruct((b, hk_h, 256), F32), jax.ShapeDtypeStruct((b, hk_g, 256), F32)],
        scratch_shapes=[pltpu.VMEM((hk_h, 256), F32), pltpu.VMEM((hk_g, 256), F32)],
        compiler_params=_params(("arbitrary", "arbitrary")),
        name="recur_pair",
    )(rec_h, rec_g, s0h, s0g, jnp.asarray(_cumsum_matrix(tt, min(CHUNK, tt)), BF16),
      jnp.asarray(_cumsum_matrix(tt, SUB), BF16), jnp.asarray(expand(hk_h), BF16), jnp.asarray(expand(hk_g), BF16),
      jnp.asarray(sel, BF16), jnp.asarray(expand(hk_h), F32), jnp.asarray(expand(hk_g), F32))


def _outffn_body(x_ref, om_ref, oh_ref, og_ref, gate_ref, g1_ref, sh2_ref, sc2_ref, g2_ref,
                 mon_ref, hon_ref, gon_ref, nf_ref, m64_ref, wo_ref, wfi_ref, wfo_ref, y_ref, *, ff_chunk):
    bb, tt, d = x_ref.shape
    tm = bb * tt
    om = om_ref[...].reshape(tm, MLA_WIDTH)
    omn = _rms(om) * mon_ref[...]

    def headnorm(o, gain_ref, gate):
        ms = jnp.dot((o * o).astype(BF16), m64_ref[...], preferred_element_type=F32)
        return o * lax.rsqrt(ms + EPS) * gain_ref[...] * gate

    gate = gate_ref[...].reshape(tm, 512)
    ohn = headnorm(oh_ref[...].reshape(tm, 256), hon_ref, gate[:, 0:256])
    ogn = headnorm(og_ref[...].reshape(tm, 256), gon_ref, gate[:, 256:512])
    mix_in = jnp.concatenate([omn, ohn, ogn], axis=-1).astype(BF16)
    mix = jnp.dot(mix_in, wo_ref[...], preferred_element_type=F32)
    x1 = x_ref[...] + g1_ref[...] * mix.reshape(bb, tt, d)
    h2 = (_rms(x1) * nf_ref[...] * (1.0 + sc2_ref[...]) + sh2_ref[...]).reshape(tm, d).astype(BF16)
    acc = jnp.zeros((tm, d), F32)
    for c in range(D_FF // ff_chunk):
        a = jnp.dot(h2, wfi_ref[:, ff_chunk * c:ff_chunk * (c + 1)], preferred_element_type=F32)
        u = jnp.dot(h2, wfi_ref[:, D_FF + ff_chunk * c:D_FF + ff_chunk * (c + 1)], preferred_element_type=F32)
        act = (_silu(a) * u).astype(BF16)
        acc = acc + jnp.dot(act, wfo_ref[ff_chunk * c:ff_chunk * (c + 1), :], preferred_element_type=F32)
    y_ref[...] = x1 + g2_ref[...] * acc.reshape(bb, tt, d)


def _outffn_call(x, om, oh, og, gate, mods, lw, bb, tt):
    b, t, d = x.shape
    tok = lambda w: pl.BlockSpec((bb, tt, w), lambda i, j: (i, j, 0))
    modspec = pl.BlockSpec((bb, 1, d), lambda i, j: (i, 0, 0))
    consts = [lw["mon"], lw["hon"], lw["gon"], lw["nf"], lw["m64"], lw["wo"], lw["wfi"], lw["wfo"]]
    single = lambda a: pl.BlockSpec(a.shape, lambda *_: (0,) * a.ndim, pipeline_mode=pl.Buffered(1))
    return pl.pallas_call(
        functools.partial(_outffn_body, ff_chunk=256),
        grid=(b // bb, t // tt),
        in_specs=[tok(d), tok(MLA_WIDTH), tok(256), tok(256), tok(512)] + [modspec] * 4 + [single(a) for a in consts],
        out_specs=tok(d),
        out_shape=jax.ShapeDtypeStruct((b, t, d), F32),
        compiler_params=_params(("arbitrary", "arbitrary")),
        name="outffn",
    )(x, om, oh, og, gate, *mods, *consts)


def _layer_weights(l, w_in, norm_mix, norm_ffn, mla_cq_norm, mla_ckv_norm, mla_w_uq, mla_w_uk, mla_w_uv, mla_q_norm,
                   mla_k_norm, mla_out_norm, hg_lb_logits, hg_out_norm, gla_w_a2, gla_b_a, gla_out_norm, w_o,
                   w_ffn_in, w_ffn_out):
    wi = w_in[l]
    d = wi.shape[0]
    o = dict(cq=0, ckv=256, kr=512, hq=544, hf=800, hi=1056, hg=1312, gq=1568, gk=1696, gv=1824, gg=2080, ga=2336)
    cols = [wi[:, o["cq"]:o["cq"] + 256], wi[:, o["ckv"]:o["ckv"] + 256], wi[:, o["hq"]:o["hq"] + 256],
            wi[:, o["hf"]:o["hf"] + 256], wi[:, o["hi"]:o["hi"] + 256], wi[:, o["hg"]:o["hg"] + 256],
            wi[:, o["gq"]:o["gq"] + 128], wi[:, o["gk"]:o["gk"] + 128], wi[:, o["gv"]:o["gv"] + 256],
            wi[:, o["gg"]:o["gg"] + 256], wi[:, o["kr"]:o["kr"] + 32], wi[:, o["ga"]:o["ga"] + 16],
            jnp.zeros((d, LANES - MLA_ROPE - GLA_RANK), wi.dtype)]
    win = jnp.concatenate(cols, axis=1).astype(BF16)
    assert win.shape[1] == ZCOLS

    uq = mla_w_uq[l]
    r = uq.shape[0]
    half = MLA_ROPE // 2
    zpad = jnp.zeros((r, MLA_HEADS, HEAD_SLAB - MLA_NOPE - MLA_ROPE), uq.dtype)
    uq_a = jnp.concatenate([uq, zpad], axis=-1).reshape(r, -1)
    uq_b = jnp.concatenate([jnp.zeros((r, MLA_HEADS, MLA_NOPE), uq.dtype), uq[..., MLA_NOPE + half:],
                            uq[..., MLA_NOPE:MLA_NOPE + half], zpad], axis=-1).reshape(r, -1)
    wuq = jnp.concatenate([uq_a, uq_b], axis=1).astype(BF16)
    uk = mla_w_uk[l]
    wuk = jnp.concatenate([uk, jnp.zeros((r, MLA_HEADS, HEAD_SLAB - MLA_NOPE), uk.dtype)], axis=-1)
    wuk = wuk.reshape(r, -1).astype(BF16)
    wuv = mla_w_uv[l].reshape(r, -1).astype(BF16)
    wuvt = wuv.T

    qn, kn = mla_q_norm[l], mla_k_norm[l]
    z32 = jnp.zeros((HEAD_SLAB - MLA_NOPE - MLA_ROPE,), F32)
    qga = jnp.concatenate([qn, z32])[None]
    qgb = jnp.concatenate([jnp.zeros((MLA_NOPE,), F32), qn[MLA_NOPE + half:], qn[MLA_NOPE:MLA_NOPE + half], z32])[None]
    kgk = jnp.concatenate([kn[:MLA_NOPE], jnp.zeros((HEAD_SLAB - MLA_NOPE,), F32)])[None]
    gkr = kn[MLA_NOPE:][None]

    blk = np.zeros((HEAD_SLAB, HEAD_SLAB), np.float32)
    blk[:MLA_NOPE, :MLA_NOPE] = 1.0 / MLA_NOPE
    mk = np.kron(np.eye(2), blk)
    blk_q = blk.copy()
    blk_q[MLA_NOPE:MLA_NOPE + MLA_ROPE, MLA_NOPE:MLA_NOPE + MLA_ROPE] = 1.0 / MLA_ROPE
    mq = np.kron(np.eye(2), blk_q)
    m64 = np.kron(np.eye(4), np.full((64, 64), 1.0 / 64, np.float32))
    tile = np.zeros((MLA_ROPE, 2 * HEAD_SLAB), np.float32)
    for hd in range(2):
        tile[np.arange(MLA_ROPE), HEAD_SLAB * hd + MLA_NOPE + np.arange(MLA_ROPE)] = 1.0

    def norm_bound(g):
        return jnp.sqrt(MLA_NOPE * jnp.max(jnp.abs(g[:MLA_NOPE])) ** 2 + MLA_ROPE * jnp.max(jnp.abs(g[MLA_NOPE:])) ** 2)
    logit_bound = (MLA_SCALE * norm_bound(qn) * norm_bound(kn)).reshape(1).astype(F32)

    return dict(
        logit_bound=logit_bound, nm=norm_mix[l][None], win=win, cqn=mla_cq_norm[l][None], ckvn=mla_ckv_norm[l][None], wuq=wuq, wuk=wuk,
        wuv=wuv, wuvt=wuvt, qga=qga, qgb=qgb, mq=jnp.asarray(mq, BF16), mk=jnp.asarray(mk, BF16), kgk=kgk, gkr=gkr,
        tile=jnp.asarray(tile, BF16), lbl=hg_lb_logits, wa2=gla_w_a2[l].astype(BF16), ba=gla_b_a[l][None],
        mon=mla_out_norm[l][None], hon=jnp.tile(hg_out_norm[l], HG_HEADS)[None],
        gon=jnp.tile(gla_out_norm[l], GLA_HEADS)[None], nf=norm_ffn[l][None], m64=jnp.asarray(m64, BF16),
        wo=w_o[l].astype(BF16), wfi=w_ffn_in[l].astype(BF16), wfo=w_ffn_out[l].astype(BF16))


def _rope_tables(pos, reps):
    inv = (np.float32(ROPE_THETA) ** (-(np.arange(0, MLA_ROPE, 2, dtype=np.float32) / np.float32(MLA_ROPE)))).astype(np.float32)
    ang = pos.astype(np.float32)[:, None] * inv[None, :]
    cos, sin = np.cos(ang).astype(np.float32), np.sin(ang).astype(np.float32)
    n = pos.shape[0]
    pad = HEAD_SLAB - MLA_NOPE - MLA_ROPE
    cosq = np.concatenate([np.ones((n, MLA_NOPE), np.float32), cos, cos, np.zeros((n, pad), np.float32)], axis=1)
    sinq = np.concatenate([np.zeros((n, MLA_NOPE), np.float32), -sin, sin, np.zeros((n, pad), np.float32)], axis=1)
    cosk = np.concatenate([cos, cos], axis=1)
    sink = np.concatenate([-sin, sin], axis=1)
    tabs = dict(cosq=cosq, sinq=sinq, cosk=cosk, sink=sink)
    return {k: jnp.asarray(np.tile(v, (reps, 1))) for k, v in tabs.items()}


def _block_diag_state(state):
    b, hh, dk, dv = state.shape
    eye = jnp.eye(hh, dtype=state.dtype)
    return (state[:, :, :, None, :] * eye[None, :, None, :, None]).reshape(b, hh * dk, hh * dv)


def _head_states(s_bd, hh):
    dk, dv = s_bd.shape[1] // hh, s_bd.shape[2] // hh
    return jnp.stack([s_bd[:, hd * dk:(hd + 1) * dk, hd * dv:(hd + 1) * dv] for hd in range(hh)], axis=1)


def _group_layer(x, mods, lw, tabs, layer, bb, tt, rtt, s_hg, s_gla, attn_fn, transpose_v):
    sh1, sc1, g1, sh2, sc2, g2 = mods
    ckv, kr, q, k, v, hrec, grec, gate = _inproj_call(x, sh1, sc1, lw, tabs, bb, tt, layer, transpose_v)
    om = attn_fn(q, k, v)
    oh, og, shg, sgl = _recur_pair_call(hrec, grec, s_hg, s_gla, HG_HEADS * HG_DK, GLA_HEADS * GLA_DK, rtt)
    y = _outffn_call(x, om, oh, og, gate, (g1, sh2, sc2, g2), lw, bb, tt)
    return y, ckv, kr, _head_states(shg, HG_HEADS), _head_states(sgl, GLA_HEADS)


def kernel(x_prompt, x_sample, c_prompt, c_sample, cache_mla_ckv, cache_mla_krope, state_hgrn, state_gla, w_mod, b_mod, norm_mix, norm_ffn, w_in, mla_cq_norm, mla_ckv_norm, mla_w_uq, mla_w_uk, mla_w_uv, mla_q_norm, mla_k_norm, mla_out_norm, hg_lb_logits, hg_out_norm, gla_w_a2, gla_b_a, gla_out_norm, w_o, w_ffn_in, w_ffn_out):
    bp, t_p, d = x_prompt.shape
    bs, t_s, _ = x_sample.shape
    depth = w_mod.shape[0]
    tm = 512
    tt_p = min(tm, t_p)
    bb_s = max(1, min(bs, tm // t_s))
    rtt_p = min(512, t_p)
    rtt_s = min(128, t_s)
    attn_tile = tt_p
    tkc = min(1024, cache_mla_ckv.shape[2])

    tabs_p = _rope_tables(np.arange(t_p), 1)
    tabs_s = _rope_tables(PAST_LEN + np.arange(t_s), bb_s)

    n_c = bp + bs
    c_rows = -(-n_c // 16) * 16
    c_pad = jnp.concatenate([c_prompt, c_sample, jnp.zeros((c_rows - n_c, d), c_prompt.dtype)], axis=0)

    cache_kr_t = jnp.swapaxes(cache_mla_krope, 2, 3)
    xp, xs = x_prompt, x_sample
    outs = [[] for _ in range(8)]
    for l in range(depth):
        lw = _layer_weights(l, w_in, norm_mix, norm_ffn, mla_cq_norm, mla_ckv_norm, mla_w_uq, mla_w_uk, mla_w_uv,
                            mla_q_norm, mla_k_norm, mla_out_norm, hg_lb_logits, hg_out_norm, gla_w_a2, gla_b_a,
                            gla_out_norm, w_o, w_ffn_in, w_ffn_out)
        mod = _mod_call(c_pad, w_mod, b_mod[:, None, :], l)
        mods_p = [mod[0:bp, i * d:(i + 1) * d][:, None, :] for i in range(6)]
        mods_s = [mod[bp:n_c, i * d:(i + 1) * d][:, None, :] for i in range(6)]

        zero_hg = jnp.zeros((bp, HG_HEADS * HG_DK, HG_HEADS * HG_DV), F32)
        zero_gla = jnp.zeros((bp, GLA_HEADS * GLA_DK, GLA_HEADS * GLA_DV), F32)
        xp, a, b_, c_, d_ = _group_layer(xp, mods_p, lw, tabs_p, l, 1, tt_p, rtt_p, zero_hg, zero_gla,
                                         lambda q, k, vt: _attn_prompt_call(lw["logit_bound"], q, k, vt, attn_tile), True)
        for lst, val in zip(outs[0:4], (a, b_, c_, d_)):
            lst.append(val)
        xs, a, b_, c_, d_ = _group_layer(xs, mods_s, lw, tabs_s, l, bb_s, t_s, rtt_s,
                                         _block_diag_state(state_hgrn[l]), _block_diag_state(state_gla[l]),
                                         lambda q, k, v: _attn_sample_call(lw["logit_bound"], q, k, v, cache_mla_ckv,
                                                                           cache_kr_t, lw, l, tkc), False)
        for lst, val in zip(outs[4:8], (a, b_, c_, d_)):
            lst.append(val)

    return (xp, xs) + tuple(jnp.stack(o) for o in outs)
```
